```python
import math
import jax
import jax.numpy as jnp
from jax import lax
import numpy as np

D_MODEL = 1024
BATCH = 32
SEQ = 2048
DEPTH = 4

N_MIXERS = 4
N_S5_LAYERS = (DEPTH + 3) // 4
N_MLSTM_LAYERS = (DEPTH + 2) // 4
N_FOX_LAYERS = (DEPTH + 1) // 4
N_RWKV_LAYERS = DEPTH // 4

NORM_EPS = 1e-6
D_FF = 4 * D_MODEL

S5_GROUP_CH = 16
S5_GROUPS = D_MODEL // S5_GROUP_CH
S5_STATE = 64
S5_DT_MIN = 1e-3
S5_DT_MAX = 1e-1

M_HEADS = 8
M_V_DIM = D_MODEL // M_HEADS
M_QK_DIM = M_V_DIM // 2
M_CONV = 4
M_CHUNK = 64
M_NORM_EPS = 1e-6
ML_IN = 2 * M_HEADS * M_QK_DIM + 2 * M_HEADS * M_V_DIM + 2 * M_HEADS

F_HEAD_DIM = 64
F_HEADS = D_MODEL // F_HEAD_DIM
F_QBLOCK = 128
F_IN = 4 * F_HEADS * F_HEAD_DIM + F_HEADS

R_HEAD_DIM = 64
R_HEADS = D_MODEL // R_HEAD_DIM
R_DECAY_LORA = int(max(32, round(1.8 * D_MODEL ** 0.5 / 32) * 32))
R_AAA_LORA = int(max(32, round(1.8 * D_MODEL ** 0.5 / 32) * 32))
R_GATE_LORA = int(max(32, round(0.6 * D_MODEL ** 0.8 / 32) * 32))
R_LN_EPS = 64e-5
R_IN = 3 * D_MODEL + R_DECAY_LORA + R_AAA_LORA + R_GATE_LORA

kernel_name = 'hybrid_s5_mlstm_fox_rwkv7_trunk'


def rms_norm(x, g):
    xf = x.astype(jnp.float32)
    y = xf * lax.rsqrt(jnp.mean(xf * xf, axis=-1, keepdims=True) + NORM_EPS)
    return (y * g.astype(jnp.float32)).astype(x.dtype)


def head_layer_norm(y, eps):
    y = y.astype(jnp.float32)
    yc = y - jnp.mean(y, axis=-1, keepdims=True)
    out = yc * lax.rsqrt(jnp.mean(yc * yc, axis=-1, keepdims=True) + eps)
    return out.reshape(out.shape[:-2] + (-1,))


def sq_relu_mlp(x, w_up, w_down):
    hid = jax.nn.relu(x @ w_up)
    return (hid * hid) @ w_down


def causal_depthwise_conv(x, w, b):
    width = w.shape[0]
    out = lax.conv_general_dilated(
        x, w[:, None, :].astype(x.dtype), window_strides=(1,), padding=[(width - 1, 0)],
        dimension_numbers=('NWC', 'WIO', 'NWC'), feature_group_count=x.shape[-1])
    return out + b


def _complex_linear_combine(left, right):
    a1r, a1i, b1r, b1i = left
    a2r, a2i, b2r, b2i = right
    return (a2r * a1r - a2i * a1i, a2r * a1i + a2i * a1r,
            a2r * b1r - a2i * b1i + b2r, a2r * b1i + a2i * b1r + b2i)


def s5_mixer(u, a_re, a_im, log_dt, b_re, b_im, c_re, c_im, d_skip, w_glu, b_glu):
    f32 = jnp.float32
    bsz, seq, d = u.shape
    dt = jnp.exp(log_dt.astype(f32))[:, None]
    ar, ai = a_re.astype(f32), a_im.astype(f32)
    mag = jnp.exp(dt * ar)
    abar_re, abar_im = mag * jnp.cos(dt * ai), mag * jnp.sin(dt * ai)
    inv = 1.0 / (ar * ar + ai * ai)
    zr, zi = abar_re - 1.0, abar_im
    coef_re = (zr * ar + zi * ai) * inv
    coef_im = (zi * ar - zr * ai) * inv
    br, bi = b_re.astype(f32), b_im.astype(f32)
    bbar_re = coef_re[..., None] * br - coef_im[..., None] * bi
    bbar_im = coef_re[..., None] * bi + coef_im[..., None] * br
    ug = u.astype(f32).reshape(bsz, seq, S5_GROUPS, S5_GROUP_CH)
    bu_re = jnp.einsum('blgh,gph->blgp', ug, bbar_re)
    bu_im = jnp.einsum('blgh,gph->blgp', ug, bbar_im)
    a_seq_re = jnp.broadcast_to(abar_re, (seq, S5_GROUPS, S5_STATE))
    a_seq_im = jnp.broadcast_to(abar_im, (seq, S5_GROUPS, S5_STATE))

    def scan_one(bre, bim):
        _, _, s_re, s_im = lax.associative_scan(
            _complex_linear_combine, (a_seq_re, a_seq_im, bre, bim), axis=0)
        return s_re, s_im

    s_re, s_im = jax.vmap(scan_one)(bu_re, bu_im)
    y = (jnp.einsum('blgp,ghp->blgh', s_re, c_re.astype(f32))
         - jnp.einsum('blgp,ghp->blgh', s_im, c_im.astype(f32)))
    y = y.reshape(bsz, seq, d) + d_skip.astype(f32) * u.astype(f32)
    z = jax.nn.gelu(y).astype(u.dtype) @ w_glu + b_glu
    out = z[..., :d] * jax.nn.sigmoid(z[..., d:])
    return out.astype(u.dtype)


def mlstm_mixer(x, w_in, conv_w, conv_b, b_i, b_f, head_g, w_out):
    f32 = jnp.float32
    bsz, seq, _ = x.shape
    nh, dk, dv, ch = M_HEADS, M_QK_DIM, M_V_DIM, M_CHUNK
    nc = seq // ch
    z = x @ w_in
    o0 = 2 * nh * dk
    o1 = o0 + nh * dv
    o2 = o1 + nh
    o3 = o2 + nh
    qk = jax.nn.silu(causal_depthwise_conv(z[..., :o0], conv_w, conv_b))
    q = qk[..., :nh * dk].reshape(bsz, seq, nh, dk) * (dk ** -0.5)
    k = qk[..., nh * dk:].reshape(bsz, seq, nh, dk)
    v = z[..., o0:o1].reshape(bsz, seq, nh, dv)
    log_i = (z[..., o1:o2] + b_i).astype(f32)
    log_f = jax.nn.log_sigmoid((z[..., o2:o3] + b_f).astype(f32))
    o_gate = jax.nn.sigmoid(z[..., o3:])

    def chunks4(t):
        return t.astype(f32).reshape(bsz, nc, ch, nh, -1).transpose(1, 0, 3, 2, 4)

    def chunks3(t):
        return t.reshape(bsz, nc, ch, nh).transpose(1, 0, 3, 2)

    causal = jnp.tril(jnp.ones((ch, ch), dtype=bool))

    def step(carry, inp):
        c_mat, n_vec, m_prev = carry
        qc, kc, vc, lic, lfc = inp
        b = jnp.cumsum(lfc, axis=-1)
        g = b + m_prev[..., None]
        d_log = jnp.where(causal, b[..., :, None] - b[..., None, :] + lic[..., None, :], -jnp.inf)
        m_t = jnp.maximum(g, jnp.max(d_log, axis=-1))
        w_inter = jnp.exp(g - m_t)
        s = jnp.einsum('bhtd,bhsd->bhts', qc, kc) * jnp.exp(d_log - m_t[..., None])
        num = (w_inter[..., None] * jnp.einsum('bhtd,bhde->bhte', qc, c_mat)
               + jnp.einsum('bhts,bhse->bhte', s, vc))
        den = w_inter * jnp.einsum('bhtd,bhd->bht', qc, n_vec) + jnp.sum(s, axis=-1)
        h = num / jnp.maximum(jnp.abs(den), jnp.exp(-m_t))[..., None]
        m_new = m_t[..., -1]
        w_state = jnp.exp(b[..., -1:] - b + lic - m_new[..., None])
        decay = jnp.exp(b[..., -1] + m_prev - m_new)
        c_new = decay[..., None, None] * c_mat + jnp.einsum('bhsd,bhse->bhde', kc * w_state[..., None], vc)
        n_new = decay[..., None] * n_vec + jnp.einsum('bhs,bhsd->bhd', w_state, kc)
        return (c_new, n_new, m_new), h

    init = (jnp.zeros((bsz, nh, dk, dv), f32), jnp.zeros((bsz, nh, dk), f32), jnp.zeros((bsz, nh), f32))
    _, h = lax.scan(step, init, (chunks4(q), chunks4(k), chunks4(v), chunks3(log_i), chunks3(log_f)))
    h = h.transpose(1, 0, 3, 2, 4).reshape(bsz, seq, nh, dv)
    h = head_layer_norm(h, M_NORM_EPS) * head_g.astype(f32)
    out = (h * o_gate.astype(f32)).astype(x.dtype) @ w_out
    return out.astype(x.dtype)


def fox_mixer(x, w_in, b_f, w_out):
    f32 = jnp.float32
    bsz, seq, _ = x.shape
    nh, dh, qb = F_HEADS, F_HEAD_DIM, F_QBLOCK
    hd = nh * dh
    z = x @ w_in

    def heads(t):
        return t.reshape(bsz, seq, nh, dh).transpose(0, 2, 1, 3)

    q, k, v = heads(z[..., :hd]), heads(z[..., hd:2 * hd]), heads(z[..., 2 * hd:3 * hd])
    log_f = jax.nn.log_sigmoid((z[..., 3 * hd:3 * hd + nh] + b_f).astype(f32))
    o_gate = jax.nn.sigmoid(z[..., 3 * hd + nh:])
    cum = jnp.cumsum(log_f, axis=1).transpose(0, 2, 1)
    scale = dh ** -0.5
    outs = []
    for blk in range(seq // qb):
        t0, t1 = blk * qb, (blk + 1) * qb
        s = jnp.einsum('bhtd,bhsd->bhts', q[:, :, t0:t1], k[:, :, :t1]).astype(f32) * scale
        s = s + (cum[:, :, t0:t1, None] - cum[:, :, None, :t1])
        mask = (t0 + jnp.arange(qb))[:, None] >= jnp.arange(t1)[None, :]
        p = jax.nn.softmax(jnp.where(mask, s, -jnp.inf), axis=-1)
        outs.append(jnp.einsum('bhts,bhsd->bhtd', p.astype(v.dtype), v[:, :, :t1]))
    o = jnp.concatenate(outs, axis=2).transpose(0, 2, 1, 3).reshape(bsz, seq, hd)
    return ((o * o_gate) @ w_out).astype(x.dtype)


def rwkv7_mixer(x, mu, w_in, w0, w_up, a0, a_up, g_up, k_k, k_a, r_k, ln_g, ln_b, w_out):
    f32 = jnp.float32
    bsz, seq, d = x.shape
    nh, dh = R_HEADS, R_HEAD_DIM
    x_prev = jnp.pad(x, ((0, 0), (1, 0), (0, 0)))[:, :-1]
    dx = x_prev - x
    xr, xw, xk, xv, xa, xg = (x + dx * mu[idx] for idx in range(6))
    c0 = 3 * d
    c1 = c0 + R_DECAY_LORA
    c2 = c1 + R_AAA_LORA
    r = xr @ w_in[:, :d]
    k = xk @ w_in[:, d:2 * d]
    v = xv @ w_in[:, 2 * d:c0]
    w_log = -jax.nn.softplus(-(w0 + jnp.tanh(xw @ w_in[:, c0:c1]) @ w_up).astype(f32)) - 0.5
    decay = jnp.exp(-jnp.exp(w_log))
    a = jax.nn.sigmoid((a0 + (xa @ w_in[:, c1:c2]) @ a_up).astype(f32))
    g = jax.nn.sigmoid(xg @ w_in[:, c2:]) @ g_up

    def heads(t):
        return t.astype(f32).reshape(bsz, seq, nh, dh)

    kk = heads(k * k_k)
    kk = kk / jnp.maximum(jnp.sqrt(jnp.sum(kk * kk, axis=-1, keepdims=True)), 1e-12)
    k_mod = k.astype(f32) * (1.0 + (a - 1.0) * k_a.astype(f32))
    rh, wh, kh, vh, ah = heads(r), heads(decay), heads(k_mod), heads(v), heads(a)

    def time_major(t):
        return t.transpose(1, 0, 2, 3)

    def step(state, inp):
        r_t, w_t, k_t, v_t, kk_t, a_t = inp
        sa = jnp.einsum('bhij,bhj->bhi', state, -kk_t)
        state = (state * w_t[:, :, None, :] + sa[..., None] * (kk_t * a_t)[:, :, None, :]
                 + v_t[..., None] * k_t[:, :, None, :])
        return state, jnp.einsum('bhij,bhj->bhi', state, r_t)

    s0 = jnp.zeros((bsz, nh, dh, dh), f32)
    _, y = lax.scan(step, s0, (time_major(rh), time_major(wh), time_major(kh),
                               time_major(vh), time_major(kk), time_major(ah)))
    y = time_major(y)
    bonus = jnp.sum(rh * kh * r_k.astype(f32), axis=-1, keepdims=True) * vh
    y = (head_layer_norm(y, R_LN_EPS) * ln_g.astype(f32) + ln_b.astype(f32)
         + bonus.reshape(bsz, seq, d)) * g.astype(f32)
    return (y.astype(x.dtype) @ w_out).astype(x.dtype)


def setup_inputs(seed: int = 0) -> dict:
    key = jax.random.key(seed)
    ks = iter(jax.random.split(key, 64))
    f32 = jnp.float32
    d = D_MODEL
    n_a, n_b, n_c, n_d = N_S5_LAYERS, N_MLSTM_LAYERS, N_FOX_LAYERS, N_RWKV_LAYERS

    def nrm(shape, std):
        return std * jax.random.normal(next(ks), shape, f32)

    def gain(shape):
        return 1.0 + nrm(shape, 0.02)

    inp = {}
    inp['x'] = jax.random.normal(next(ks), (BATCH, SEQ, d), f32)
    inp['mlp_norm_g'] = gain((DEPTH, d))
    inp['mlp_w_up'] = nrm((DEPTH, d, D_FF), d ** -0.5)
    inp['mlp_w_down'] = nrm((DEPTH, D_FF, d), D_FF ** -0.5)
    inp['final_norm_g'] = gain((d,))
    n_idx = jnp.arange(S5_STATE, dtype=f32)
    inp['s5_norm_g'] = gain((n_a, d))
    inp['s5_a_re'] = -0.5 + nrm((n_a, S5_GROUPS, S5_STATE), 0.01)
    inp['s5_a_im'] = jnp.pi * n_idx + nrm((n_a, S5_GROUPS, S5_STATE), 0.01)
    inp['s5_log_dt'] = jax.random.uniform(next(ks), (n_a, S5_GROUPS), f32,
                                          math.log(S5_DT_MIN), math.log(S5_DT_MAX))
    inp['s5_b_re'] = nrm((n_a, S5_GROUPS, S5_STATE, S5_GROUP_CH), (2 * S5_GROUP_CH) ** -0.5)
    inp['s5_b_im'] = nrm((n_a, S5_GROUPS, S5_STATE, S5_GROUP_CH), (2 * S5_GROUP_CH) ** -0.5)
    inp['s5_c_re'] = nrm((n_a, S5_GROUPS, S5_GROUP_CH, S5_STATE), S5_STATE ** -0.5)
    inp['s5_c_im'] = nrm((n_a, S5_GROUPS, S5_GROUP_CH, S5_STATE), S5_STATE ** -0.5)
    inp['s5_d'] = nrm((n_a, d), 1.0)
    inp['s5_w_glu'] = nrm((n_a, d, 2 * d), d ** -0.5)
    inp['s5_b_glu'] = nrm((n_a, 2 * d), 0.01)
    inp['ml_norm_g'] = gain((n_b, d))
    inp['ml_w_in'] = nrm((n_b, d, ML_IN), d ** -0.5)
    inp['ml_conv_w'] = nrm((n_b, M_CONV, 2 * M_HEADS * M_QK_DIM), M_CONV ** -0.5)
    inp['ml_conv_b'] = nrm((n_b, 2 * M_HEADS * M_QK_DIM), 0.01)
    inp['ml_b_i'] = nrm((n_b, M_HEADS), 0.1)
    inp['ml_b_f'] = jnp.linspace(3.0, 6.0, M_HEADS, dtype=f32) + nrm((n_b, M_HEADS), 0.01)
    inp['ml_head_g'] = gain((n_b, M_HEADS * M_V_DIM))
    inp['ml_w_out'] = nrm((n_b, M_HEADS * M_V_DIM, d), (M_HEADS * M_V_DIM) ** -0.5)
    inp['fox_norm_g'] = gain((n_c, d))
    inp['fox_w_in'] = nrm((n_c, d, F_IN), d ** -0.5)
    inp['fox_b_f'] = jnp.linspace(1.0, 6.0, F_HEADS, dtype=f32) + nrm((n_c, F_HEADS), 0.01)
    inp['fox_w_out'] = nrm((n_c, F_HEADS * F_HEAD_DIM, d), (F_HEADS * F_HEAD_DIM) ** -0.5)
    inp['rw_norm_g'] = gain((n_d, d))
    inp['rw_mu'] = jax.random.uniform(next(ks), (n_d, 6, d), f32)
    inp['rw_w_in'] = nrm((n_d, d, R_IN), d ** -0.5)
    inp['rw_w0'] = jnp.linspace(-6.5, -1.5, d, dtype=f32) + nrm((n_d, d), 0.01)
    inp['rw_w_up'] = nrm((n_d, R_DECAY_LORA, d), 0.1 * R_DECAY_LORA ** -0.5)
    inp['rw_a0'] = nrm((n_d, d), 0.1)
    inp['rw_a_up'] = nrm((n_d, R_AAA_LORA, d), R_AAA_LORA ** -0.5)
    inp['rw_g_up'] = nrm((n_d, R_GATE_LORA, d), R_GATE_LORA ** -0.5)
    inp['rw_k_k'] = 0.85 + nrm((n_d, d), 0.02)
    inp['rw_k_a'] = 1.0 + nrm((n_d, d), 0.02)
    inp['rw_r_k'] = -0.04 + nrm((n_d, R_HEADS, R_HEAD_DIM), 0.02)
    inp['rw_ln_g'] = gain((n_d, d))
    inp['rw_ln_b'] = nrm((n_d, d), 0.01)
    inp['rw_w_out'] = nrm((n_d, d, d), d ** -0.5)
    return inp


def reference(x, mlp_norm_g, mlp_w_up, mlp_w_down, final_norm_g,
              s5_norm_g, s5_a_re, s5_a_im, s5_log_dt, s5_b_re, s5_b_im, s5_c_re, s5_c_im,
              s5_d, s5_w_glu, s5_b_glu,
              ml_norm_g, ml_w_in, ml_conv_w, ml_conv_b, ml_b_i, ml_b_f, ml_head_g, ml_w_out,
              fox_norm_g, fox_w_in, fox_b_f, fox_w_out,
              rw_norm_g, rw_mu, rw_w_in, rw_w0, rw_w_up, rw_a0, rw_a_up, rw_g_up,
              rw_k_k, rw_k_a, rw_r_k, rw_ln_g, rw_ln_b, rw_w_out):
    h = x
    for i in range(DEPTH):
        kind, j = i % N_MIXERS, i // N_MIXERS
        if kind == 0:
            mix = s5_mixer(rms_norm(h, s5_norm_g[j]), s5_a_re[j], s5_a_im[j], s5_log_dt[j],
                           s5_b_re[j], s5_b_im[j], s5_c_re[j], s5_c_im[j], s5_d[j],
                           s5_w_glu[j], s5_b_glu[j])
        elif kind == 1:
            mix = mlstm_mixer(rms_norm(h, ml_norm_g[j]), ml_w_in[j], ml_conv_w[j], ml_conv_b[j],
                              ml_b_i[j], ml_b_f[j], ml_head_g[j], ml_w_out[j])
        elif kind == 2:
            mix = fox_mixer(rms_norm(h, fox_norm_g[j]), fox_w_in[j], fox_b_f[j], fox_w_out[j])
        else:
            mix = rwkv7_mixer(rms_norm(h, rw_norm_g[j]), rw_mu[j], rw_w_in[j], rw_w0[j], rw_w_up[j],
                              rw_a0[j], rw_a_up[j], rw_g_up[j], rw_k_k[j], rw_k_a[j], rw_r_k[j],
                              rw_ln_g[j], rw_ln_b[j], rw_w_out[j])
        h = h + mix
        h = h + sq_relu_mlp(rms_norm(h, mlp_norm_g[i]), mlp_w_up[i], mlp_w_down[i]).astype(h.dtype)
    return rms_norm(h, final_norm_g)
```

```python
import functools
import math

import jax
import jax.numpy as jnp
from jax import lax
from jax.experimental import pallas as pl
from jax.experimental.pallas import tpu as pltpu

F32 = jnp.float32
BF16 = jnp.bfloat16

NORM_EPS = 1e-6
NEG_BIG = -1e30
LANES = 128
VMEM_LIMIT = 56 * 1024 * 1024

M_HEADS = 8
M_QK_DIM = 64
M_V_DIM = 128
M_CONV = 4
M_NORM_EPS = 1e-6


def _cparams(*sem):
    return pltpu.CompilerParams(dimension_semantics=sem, vmem_limit_bytes=VMEM_LIMIT)


def _rms(x, g):
    return x * lax.rsqrt(jnp.mean(x * x, axis=-1, keepdims=True) + NORM_EPS) * g


def _dot(a, b):
    return jnp.dot(a, b, preferred_element_type=F32)


def _dot_nt(a, b):
    return lax.dot_general(a, b, (((1,), (1,)), ((), ())), preferred_element_type=F32)


def _split3(x):
    hi = x.astype(BF16)
    r = x - hi.astype(F32)
    mid = r.astype(BF16)
    lo = (r - mid.astype(F32)).astype(BF16)
    return hi, mid, lo


def _dot_exact_l(ones_mat, x):
    hi, mid, lo = _split3(x)
    return _dot(ones_mat, hi) + _dot(ones_mat, mid) + _dot(ones_mat, lo)


def _dot_exact_r(x, ones_mat):
    hi, mid, lo = _split3(x)
    return _dot(hi, ones_mat) + _dot(mid, ones_mat) + _dot(lo, ones_mat)


def _log_sigmoid(x):
    return jnp.minimum(x, 0.0) - jnp.log(1.0 + jnp.exp(-jnp.abs(x)))


def _sigmoid(x):
    return 1.0 / (1.0 + jnp.exp(-x))


def _const_spec(shape):
    nd = len(shape)
    return pl.BlockSpec(shape, lambda *_: (0,) * nd)


def _mlp_kernel(h_ref, g_ref, wup_ref, wdn_ref, fg_ref, out_ref, *, fc, final):
    x = h_ref[...]
    xn = _rms(x, g_ref[...]).astype(BF16)
    acc = x
    for c in range(wup_ref.shape[1] // fc):
        hid = jnp.maximum(_dot(xn, wup_ref[:, c * fc:(c + 1) * fc]), 0.0)
        acc = acc + _dot((hid * hid).astype(BF16), wdn_ref[c * fc:(c + 1) * fc, :])
    out_ref[...] = _rms(acc, fg_ref[...]) if final else acc


def _mlp_layer(h2d, norm_g, w_up, w_down, final_g=None, *, tm=512, fc=512):
    t, d = h2d.shape
    ff = w_up.shape[1]
    fg = (norm_g if final_g is None else final_g).reshape(1, d)
    return pl.pallas_call(
        functools.partial(_mlp_kernel, fc=fc, final=final_g is not None),
        grid=(t // tm,),
        in_specs=[pl.BlockSpec((tm, d), lambda i: (i, 0)), _const_spec((1, d)),
                  _const_spec((d, ff)), _const_spec((ff, d)), _const_spec((1, d))],
        out_specs=pl.BlockSpec((tm, d), lambda i: (i, 0)),
        out_shape=jax.ShapeDtypeStruct((t, d), F32),
        compiler_params=_cparams("parallel"),
        name="mlp",
    )(h2d, norm_g.reshape(1, d), w_up.astype(BF16), w_down.astype(BF16), fg)


def _ml_proj_kernel(h_ref, g_ref, wqk_ref, wv_ref, wo_ref, wg_ref, wgt_ref,
                    qk_ref, v_ref, o_ref, gc_ref, gr_ref):
    xn = _rms(h_ref[...], g_ref[...]).astype(BF16)
    qk_ref[...] = _dot(xn, wqk_ref[...])
    v_ref[...] = _dot(xn, wv_ref[...]).astype(v_ref.dtype)
    o_ref[...] = _dot(xn, wo_ref[...])
    gc_ref[...] = _dot(xn, wg_ref[...])
    gr_ref[...] = _dot_nt(wgt_ref[...], xn)


def _ml_kernel(qk_ref, v_ref, o_ref, gc_ref, gr_ref, h_ref, cw_ref, cb_ref, bcol_ref, brow_ref,
               hg_ref, wout_ref, out_ref, ext_scr, c_scr, m_scr, hbuf, *, lt, lc):
    nh, dk, dv = M_HEADS, M_QK_DIM, M_V_DIM
    hd = nh * dk

    @pl.when(pl.program_id(1) == 0)
    def _():
        c_scr[...] = jnp.zeros_like(c_scr)
        m_scr[...] = jnp.zeros_like(m_scr)
        ext_scr[0:8, :] = jnp.zeros((8, 2 * hd), F32)

    ext_scr[8:8 + lt, :] = qk_ref[...]
    conv = cb_ref[...] + jnp.zeros((lt, 2 * hd), F32)
    for kk in range(M_CONV):
        conv = conv + cw_ref[kk:kk + 1, :] * ext_scr[8 - (M_CONV - 1) + kk:8 - (M_CONV - 1) + kk + lt, :]
    ext_scr[0:8, :] = ext_scr[lt:lt + 8, :]
    act = conv * _sigmoid(conv)

    lane = lax.broadcasted_iota(jnp.int32, (1, LANES), 1)
    r_i = lax.broadcasted_iota(jnp.int32, (lc, lc), 0)
    c_i = lax.broadcasted_iota(jnp.int32, (lc, lc), 1)
    causal = c_i <= r_i
    tri_l = causal.astype(BF16)
    tri_u = (r_i <= c_i).astype(BF16)
    ones_col = (lax.broadcasted_iota(jnp.int32, (lc, LANES), 1) == 0).astype(BF16)
    row16 = lax.broadcasted_iota(jnp.int32, (2 * nh, 1), 0)

    for c in range(lt // lc):
        rows = slice(c * lc, (c + 1) * lc)
        gcb = gc_ref[rows, :] + bcol_ref[...]
        gate_c = jnp.where(lane < nh, gcb, jnp.where(lane < 2 * nh, _log_sigmoid(gcb), 0.0))
        cum_c = _dot_exact_l(tri_l, jnp.where(lane < nh, 0.0, gate_c))
        grb = gr_ref[:, rows] + brow_ref[...]
        gate_r = jnp.where(row16 < nh, grb, _log_sigmoid(grb))
        cum_r = _dot_exact_r(jnp.where(row16 < nh, 0.0, gate_r), tri_u)

        for h in range(nh):
            p, e = h // 2, h % 2
            head_mask = (lane >= e * dk) & (lane < (e + 1) * dk)
            q_m = jnp.where(head_mask, act[rows, p * LANES:(p + 1) * LANES], 0.0) * (dk ** -0.5)
            k_m = jnp.where(head_mask, act[rows, hd + p * LANES:hd + (p + 1) * LANES], 0.0)
            v_aug = jnp.concatenate([v_ref[rows, h * dv:(h + 1) * dv].astype(BF16), ones_col], axis=1)

            b_col = cum_c[:, nh + h:nh + h + 1]
            li_col = gate_c[:, h:h + 1]
            b_row = cum_r[nh + h:nh + h + 1, :]
            li_row = gate_r[h:h + 1, :]
            m_prev = m_scr[h:h + 1, 0:1]

            g = b_col + m_prev
            d_log = jnp.where(causal, b_col - b_row + li_row, NEG_BIG)
            m_t = jnp.maximum(g, jnp.max(d_log, axis=-1, keepdims=True))
            w_inter = jnp.exp(g - m_t)
            s = _dot_nt(q_m.astype(BF16), k_m.astype(BF16)) * jnp.exp(d_log - m_t)
            c_aug = c_scr[h]
            r = w_inter * _dot(q_m.astype(BF16), c_aug.astype(BF16)) + _dot(s.astype(BF16), v_aug)
            den = r[:, dv:dv + 1]
            hh = r[:, :dv] / jnp.maximum(jnp.abs(den), jnp.exp(-m_t))

            m_new = m_t[lc - 1:lc, :]
            b_last = b_col[lc - 1:lc, :]
            w_state = jnp.exp(b_last - b_col + li_col - m_new)
            decay = jnp.exp(b_last + m_prev - m_new)
            kw_t = jnp.transpose(k_m * w_state).astype(BF16)
            c_scr[h] = decay * c_aug + _dot(kw_t, v_aug)
            m_scr[h:h + 1, :] = jnp.broadcast_to(m_new, (1, LANES))

            yc = hh - jnp.mean(hh, axis=-1, keepdims=True)
            yn = yc * lax.rsqrt(jnp.mean(yc * yc, axis=-1, keepdims=True) + M_NORM_EPS)
            cols = slice(h * dv, (h + 1) * dv)
            hbuf[rows, cols] = yn * hg_ref[:, cols] * _sigmoid(o_ref[rows, cols])

    out_ref[...] = h_ref[...] + _dot(hbuf[...].astype(BF16), wout_ref[...])


def _mlstm_layer(h2d, bsz, seq, norm_g, w_in, conv_w, conv_b, b_i, b_f, head_g, w_out, *, tm=512, lt=256, lc=64):
    t, d = h2d.shape
    nh, dk, dv = M_HEADS, M_QK_DIM, M_V_DIM
    o0 = 2 * nh * dk
    o1 = o0 + nh * dv
    o3 = o1 + 2 * nh
    wqk = w_in[:, :o0].astype(BF16)
    wv = w_in[:, o0:o1].astype(BF16)
    wo = w_in[:, o3:].astype(BF16)
    wg = jnp.pad(w_in[:, o1:o3], ((0, 0), (0, LANES - 2 * nh))).astype(BF16)
    wgt = jnp.transpose(w_in[:, o1:o3]).astype(BF16)
    g2 = norm_g.reshape(1, d)

    qk, v, o, gc, gr = pl.pallas_call(
        _ml_proj_kernel,
        grid=(t // tm,),
        in_specs=[pl.BlockSpec((tm, d), lambda i: (i, 0)), _const_spec((1, d)),
                  _const_spec(wqk.shape), _const_spec(wv.shape), _const_spec(wo.shape),
                  _const_spec(wg.shape), _const_spec(wgt.shape)],
        out_specs=[pl.BlockSpec((tm, o0), lambda i: (i, 0)), pl.BlockSpec((tm, nh * dv), lambda i: (i, 0)),
                   pl.BlockSpec((tm, nh * dv), lambda i: (i, 0)), pl.BlockSpec((tm, LANES), lambda i: (i, 0)),
                   pl.BlockSpec((2 * nh, tm), lambda i: (0, i))],
        out_shape=[jax.ShapeDtypeStruct((t, o0), F32), jax.ShapeDtypeStruct((t, nh * dv), BF16),
                   jax.ShapeDtypeStruct((t, nh * dv), F32), jax.ShapeDtypeStruct((t, LANES), F32),
                   jax.ShapeDtypeStruct((2 * nh, t), F32)],
        compiler_params=_cparams("parallel"),
        name="ml_proj",
    )(h2d, g2, wqk, wv, wo, wg, wgt)

    bcol = jnp.pad(jnp.concatenate([b_i, b_f]), (0, LANES - 2 * nh)).reshape(1, LANES)
    brow = jnp.concatenate([b_i, b_f]).reshape(2 * nh, 1)
    nt = seq // lt
    row_spec = lambda w: pl.BlockSpec((lt, w), lambda b, i: (b * nt + i, 0))
    return pl.pallas_call(
        functools.partial(_ml_kernel, lt=lt, lc=lc),
        grid=(bsz, nt),
        in_specs=[row_spec(o0), row_spec(nh * dv), row_spec(nh * dv), row_spec(LANES),
                  pl.BlockSpec((2 * nh, lt), lambda b, i: (0, b * nt + i)), row_spec(d),
                  _const_spec((M_CONV, o0)), _const_spec((1, o0)), _const_spec((1, LANES)),
                  _const_spec((2 * nh, 1)), _const_spec((1, nh * dv)), _const_spec((nh * dv, d))],
        out_specs=row_spec(d),
        out_shape=jax.ShapeDtypeStruct((t, d), F32),
        scratch_shapes=[pltpu.VMEM((lt + 8, o0), F32), pltpu.VMEM((nh, LANES, 2 * LANES), F32),
                        pltpu.VMEM((nh, LANES), F32), pltpu.VMEM((lt, nh * dv), F32)],
        compiler_params=_cparams("parallel", "arbitrary"),
        name="ml_mix",
    )(qk, v, o, gc, gr, h2d, conv_w, conv_b.reshape(1, o0), bcol, brow,
      head_g.reshape(1, nh * dv), w_out.astype(BF16))


S5_GROUP_CH = 16
S5_STATE = 64
S5_CHUNK_GROUPS = 4
S5_CHUNK = S5_CHUNK_GROUPS * S5_STATE


def _s5_disc_kernel(ar_ref, ai_ref, ldt_ref, arx_ref, aix_ref, br_ref, bi_ref,
                    abr_ref, abi_ref, bbr_ref, bbi_ref):
    dt = jnp.exp(ldt_ref[...])

    def abar(ar, ai):
        mag = jnp.exp(dt * ar)
        return mag * jnp.cos(dt * ai), mag * jnp.sin(dt * ai)

    abr_ref[...], abi_ref[...] = abar(ar_ref[...], ai_ref[...])
    ar, ai = arx_ref[...], aix_ref[...]
    xr, xi = abar(ar, ai)
    inv = 1.0 / (ar * ar + ai * ai)
    zr, zi = xr - 1.0, xi
    coef_re = (zr * ar + zi * ai) * inv
    coef_im = (zi * ar - zr * ai) * inv
    br, bi = br_ref[...], bi_ref[...]
    bbr_ref[...] = coef_re * br - coef_im * bi
    bbi_ref[...] = coef_re * bi + coef_im * br


def _s5_kernel(h_ref, g_ref, abr_ref, abi_ref, bre_ref, bim_ref, cre_ref, cim_ref, dsk_ref, wglu_ref, bglu_ref,
               out_ref, re_scr, im_scr, sr_scr, si_scr, *, tt, bsz):
    d = h_ref.shape[1]
    nchunk = re_scr.shape[0]
    per_slab = nchunk // (d // S5_CHUNK)

    @pl.when(pl.program_id(0) == 0)
    def _():
        sr_scr[...] = jnp.zeros_like(sr_scr)
        si_scr[...] = jnp.zeros_like(si_scr)

    x = h_ref[...]
    u = _rms(x, g_ref[...])
    ub = u.astype(BF16)
    for c in range(nchunk):
        slab = c // per_slab
        us = ub[:, slab * S5_CHUNK:(slab + 1) * S5_CHUNK]
        re_scr[c] = _dot(us, bre_ref[c])
        im_scr[c] = _dot(us, bim_ref[c])

    def scan_chunk(c, carry):
        ar, ai = abr_ref[c], abi_ref[c]
        sr, si = sr_scr[c], si_scr[c]
        for t in range(tt):
            rows = slice(t * bsz, (t + 1) * bsz)
            nr = ar * sr - ai * si + re_scr[c, rows, :]
            ni = ar * si + ai * sr + im_scr[c, rows, :]
            re_scr[c, rows, :] = nr
            im_scr[c, rows, :] = ni
            sr, si = nr, ni
        sr_scr[c] = sr
        si_scr[c] = si
        return carry

    lax.fori_loop(0, nchunk, scan_chunk, 0)

    ys = []
    for slab in range(d // S5_CHUNK):
        y = jnp.zeros((tt * bsz, S5_CHUNK), F32)
        for c in range(slab * per_slab, (slab + 1) * per_slab):
            y = y + _dot(re_scr[c].astype(BF16), cre_ref[c]) - _dot(im_scr[c].astype(BF16), cim_ref[c])
        ys.append(y)
    y = jnp.concatenate(ys, axis=1) + dsk_ref[...] * u
    gelu = 0.5 * y * (1.0 + jnp.tanh(math.sqrt(2.0 / math.pi) * (y + 0.044715 * (y * y * y))))
    z = _dot(gelu.astype(BF16), wglu_ref[...]) + bglu_ref[...]
    out_ref[...] = x + z[:, :d] * _sigmoid(z[:, d:])


def _s5_layer(h2d, bsz, seq, norm_g, a_re, a_im, log_dt, b_re, b_im, c_re, c_im, d_skip, w_glu, b_glu, *, tt=8):
    t, d = h2d.shape
    ng, ns, hg = a_re.shape[0], S5_STATE, S5_GROUP_CH
    cg = S5_CHUNK_GROUPS
    nchunk = ng // cg
    per_slab = S5_CHUNK // (cg * hg)

    rep = lambda a: jnp.repeat(a, hg, axis=1)
    abr, abi, bbr, bbi = pl.pallas_call(
        _s5_disc_kernel,
        out_shape=[jax.ShapeDtypeStruct((ng, ns), F32)] * 2 + [jax.ShapeDtypeStruct((ng, ns * hg), F32)] * 2,
        name="s5_disc",
    )(a_re, a_im, log_dt.reshape(ng, 1), rep(a_re), rep(a_im), b_re.reshape(ng, ns * hg), b_im.reshape(ng, ns * hg))

    eye = jnp.eye(cg, dtype=F32)
    place = jax.nn.one_hot(jnp.arange(nchunk) % per_slab, per_slab, dtype=F32)

    def b_blocks(bb):
        m = bb.reshape(nchunk, cg, ns, hg)
        blk = jnp.einsum('cgph,gq->cghqp', m, eye).reshape(nchunk, cg * hg, cg * ns)
        return jnp.einsum('crs,cj->cjrs', blk, place).reshape(nchunk, per_slab * cg * hg, cg * ns).astype(BF16)

    def c_blocks(cc):
        m = cc.reshape(nchunk, cg, hg, ns)
        blk = jnp.einsum('cghp,gq->cgpqh', m, eye).reshape(nchunk, cg * ns, cg * hg)
        return jnp.einsum('csr,cj->csjr', blk, place).reshape(nchunk, cg * ns, per_slab * cg * hg).astype(BF16)

    rows = tt * bsz
    ht = h2d.reshape(bsz, seq, d).transpose(1, 0, 2).reshape(t, d)
    out = pl.pallas_call(
        functools.partial(_s5_kernel, tt=tt, bsz=bsz),
        grid=(seq // tt,),
        in_specs=[pl.BlockSpec((rows, d), lambda i: (i, 0)), _const_spec((1, d)),
                  _const_spec((nchunk, 1, S5_CHUNK)), _const_spec((nchunk, 1, S5_CHUNK)),
                  _const_spec((nchunk, S5_CHUNK, S5_CHUNK)), _const_spec((nchunk, S5_CHUNK, S5_CHUNK)),
                  _const_spec((nchunk, S5_CHUNK, S5_CHUNK)), _const_spec((nchunk, S5_CHUNK, S5_CHUNK)),
                  _const_spec((1, d)), _const_spec((d, 2 * d)), _const_spec((1, 2 * d))],
        out_specs=pl.BlockSpec((rows, d), lambda i: (i, 0)),
        out_shape=jax.ShapeDtypeStruct((t, d), F32),
        scratch_shapes=[pltpu.VMEM((nchunk, rows, S5_CHUNK), F32), pltpu.VMEM((nchunk, rows, S5_CHUNK), F32),
                        pltpu.VMEM((nchunk, bsz, S5_CHUNK), F32), pltpu.VMEM((nchunk, bsz, S5_CHUNK), F32)],
        compiler_params=_cparams("arbitrary"),
        name="s5_mix",
    )(ht, norm_g.reshape(1, d), abr.reshape(nchunk, 1, S5_CHUNK), abi.reshape(nchunk, 1, S5_CHUNK),
      b_blocks(bbr), b_blocks(bbi), c_blocks(c_re), c_blocks(c_im),
      d_skip.reshape(1, d), w_glu.astype(BF16), b_glu.reshape(1, 2 * d))
    return out.reshape(seq, bsz, d).transpose(1, 0, 2).reshape(t, d)


F_HEADS = 16
F_HEAD_DIM = 64


def _fox_proj_kernel(h_ref, g_ref, wq_ref, wk_ref, wv_ref, wo_ref, wg_ref, wgt_ref,
                     q_ref, k_ref, v_ref, o_ref, gc_ref, gr_ref):
    xn = _rms(h_ref[...], g_ref[...]).astype(BF16)
    q_ref[...] = (_dot(xn, wq_ref[...]) * (F_HEAD_DIM ** -0.5)).astype(BF16)
    k_ref[...] = _dot(xn, wk_ref[...]).astype(BF16)
    v_ref[...] = _dot(xn, wv_ref[...]).astype(BF16)
    o_ref[...] = _dot(xn, wo_ref[...])
    gc_ref[...] = _dot(xn, wg_ref[...])
    gr_ref[...] = _dot_nt(wgt_ref[...], xn)


def _fox_kernel(q_ref, k_ref, v_ref, o_ref, gc_ref, gr_ref, h_ref, bcol_ref, brow_ref, wout_ref,
                out_ref, cc_scr, cr_scr, obuf, *, tq, seq):
    nh, dh = F_HEADS, F_HEAD_DIM
    i = pl.program_id(1)
    lane = lax.broadcasted_iota(jnp.int32, (1, LANES), 1)

    @pl.when(i == 0)
    def _():
        r_i = lax.broadcasted_iota(jnp.int32, (LANES, LANES), 0)
        c_i = lax.broadcasted_iota(jnp.int32, (LANES, LANES), 1)
        tri_l = (c_i <= r_i).astype(BF16)
        tri_u = (r_i <= c_i).astype(BF16)
        carry_c = jnp.zeros((1, LANES), F32)
        carry_r = jnp.zeros((nh, 1), F32)
        per = tq // LANES
        for blk in range(seq // LANES):
            rows = slice(blk * LANES, (blk + 1) * LANES)
            cb = _dot_exact_l(tri_l, _log_sigmoid(gc_ref[rows, :] + bcol_ref[...])) + carry_c
            cc_scr[rows, :] = cb
            carry_c = cb[LANES - 1:LANES, :]
            cr = _dot_exact_r(_log_sigmoid(gr_ref[:, rows] + brow_ref[...]), tri_u) + carry_r
            cr_scr[blk // per, :, (blk % per) * LANES:(blk % per + 1) * LANES] = cr
            carry_r = cr[:, LANES - 1:LANES]

    q0 = pl.multiple_of(i * tq, tq)
    ccq = cc_scr[pl.ds(q0, tq), :]
    r_i = lax.broadcasted_iota(jnp.int32, (tq, tq), 0)
    c_i = lax.broadcasted_iota(jnp.int32, (tq, tq), 1)
    causal = c_i <= r_i

    for p in range(nh // 2):
        cols = slice(p * LANES, (p + 1) * LANES)
        qp = q_ref[:, cols]
        q_m = [jnp.where((lane >= e * dh) & (lane < (e + 1) * dh), qp, jnp.zeros_like(qp)) for e in range(2)]
        ccol = [ccq[:, 2 * p + e:2 * p + e + 1] for e in range(2)]

        def step(j, carry, masked):
            k0 = pl.multiple_of(j * tq, tq)
            kp = k_ref[pl.ds(k0, tq), cols]
            vp = v_ref[pl.ds(k0, tq), cols]
            crow = cr_scr[j]
            acc = carry[4]
            new = []
            scaled = []
            for e in range(2):
                m_old, l_old = carry[2 * e], carry[2 * e + 1]
                s = _dot_nt(q_m[e], kp) + (ccol[e] - crow[2 * p + e:2 * p + e + 1, :])
                if masked:
                    s = jnp.where(causal, s, NEG_BIG)
                m_new = jnp.maximum(m_old, jnp.max(s, axis=-1, keepdims=True))
                alpha = jnp.exp(m_old - m_new)
                pm = jnp.exp(s - m_new)
                new += [m_new, alpha * l_old + jnp.sum(pm, axis=-1, keepdims=True)]
                scaled.append(acc * alpha + _dot(pm.astype(BF16), vp))
            return (new[0], new[1], new[2], new[3], jnp.where(lane < dh, scaled[0], scaled[1]))

        col0 = jnp.zeros((tq, 1), F32)
        init = (col0 + NEG_BIG, col0, col0 + NEG_BIG, col0, jnp.zeros((tq, LANES), F32))
        carry = lax.fori_loop(0, i, lambda j, c: step(j, c, False), init)
        _, l0, _, l1, acc = step(i, carry, True)
        obuf[:, cols] = acc / jnp.where(lane < dh, l0, l1) * _sigmoid(o_ref[:, cols])

    out_ref[...] = h_ref[...] + _dot(obuf[...].astype(BF16), wout_ref[...])


def _fox_layer(h2d, bsz, seq, norm_g, w_in, b_f, w_out, *, tm=512, tq=256):
    t, d = h2d.shape
    nh, dh = F_HEADS, F_HEAD_DIM
    hd = nh * dh
    wq, wk, wv = (w_in[:, j * hd:(j + 1) * hd].astype(BF16) for j in range(3))
    wf = w_in[:, 3 * hd:3 * hd + nh]
    wo = w_in[:, 3 * hd + nh:].astype(BF16)
    wg = jnp.pad(wf, ((0, 0), (0, LANES - nh))).astype(BF16)
    wgt = jnp.transpose(wf).astype(BF16)
    tile = lambda w: pl.BlockSpec((tm, w), lambda i: (i, 0))
    q, k, v, o, gc, gr = pl.pallas_call(
        _fox_proj_kernel,
        grid=(t // tm,),
        in_specs=[tile(d), _const_spec((1, d)), _const_spec(wq.shape), _const_spec(wk.shape),
                  _const_spec(wv.shape), _const_spec(wo.shape), _const_spec(wg.shape), _const_spec(wgt.shape)],
        out_specs=[tile(hd), tile(hd), tile(hd), tile(hd), tile(LANES), pl.BlockSpec((nh, tm), lambda i: (0, i))],
        out_shape=[jax.ShapeDtypeStruct((t, hd), BF16)] * 3 + [jax.ShapeDtypeStruct((t, hd), F32),
                   jax.ShapeDtypeStruct((t, LANES), F32), jax.ShapeDtypeStruct((nh, t), F32)],
        compiler_params=_cparams("parallel"),
        name="fox_proj",
    )(h2d, norm_g.reshape(1, d), wq, wk, wv, wo, wg, wgt)

    nt = seq // tq
    bcol = jnp.pad(b_f, (0, LANES - nh)).reshape(1, LANES)
    brow = b_f.reshape(nh, 1)
    q_spec = lambda w: pl.BlockSpec((tq, w), lambda b, i: (b * nt + i, 0))
    seq_spec = lambda w: pl.BlockSpec((seq, w), lambda b, i: (b, 0))
    return pl.pallas_call(
        functools.partial(_fox_kernel, tq=tq, seq=seq),
        grid=(bsz, nt),
        in_specs=[q_spec(hd), seq_spec(hd), seq_spec(hd), q_spec(hd), seq_spec(LANES),
                  pl.BlockSpec((nh, seq), lambda b, i: (0, b)), q_spec(d),
                  _const_spec((1, LANES)), _const_spec((nh, 1)), _const_spec((hd, d))],
        out_specs=q_spec(d),
        out_shape=jax.ShapeDtypeStruct((t, d), F32),
        scratch_shapes=[pltpu.VMEM((seq, LANES), F32), pltpu.VMEM((nt, nh, tq), F32), pltpu.VMEM((tq, hd), F32)],
        compiler_params=_cparams("parallel", "arbitrary"),
        name="fox_attn",
    )(q, k, v, o, gc, gr, h2d, bcol, brow, w_out.astype(BF16))


R_HEADS = 16
R_HEAD_DIM = 64
R_LN_EPS = 64e-5
R_SUB = 16


def _softplus(z):
    return jnp.maximum(z, 0.0) + jnp.log(1.0 + jnp.exp(-jnp.abs(z)))


def _head_sum(x, lane):
    lo = lane < R_HEAD_DIM
    s0 = jnp.sum(jnp.where(lo, x, 0.0), axis=-1, keepdims=True)
    s1 = jnp.sum(jnp.where(lo, 0.0, x), axis=-1, keepdims=True)
    return jnp.where(lo, s0, s1)


def _rw_proj_kernel(h_ref, g_ref, mu_ref, wr_ref, wk_ref, wv_ref, wd_ref, wa_ref, wg_ref, wup_ref, aup_ref,
                    gup_ref, w0_ref, a0_ref, kk_ref, ka_ref,
                    r_ref, lw_ref, k_ref, v_ref, al_ref, be_ref, g_out_ref, ext_scr, *, tm, seq):
    d = h_ref.shape[1]
    i = pl.program_id(0)
    xn = _rms(h_ref[...], g_ref[...])
    @pl.when((i * tm) % seq == 0)
    def _():
        ext_scr[7:8, :] = jnp.zeros((1, d), F32)

    @pl.when((i * tm) % seq != 0)
    def _():
        ext_scr[7:8, :] = ext_scr[7 + tm:8 + tm, :]

    ext_scr[8:8 + tm, :] = xn
    dx = ext_scr[7:7 + tm, :] - xn
    mix = lambda j: (xn + dx * mu_ref[j:j + 1, :]).astype(BF16)
    r_ref[...] = _dot(mix(0), wr_ref[...])
    w_in = w0_ref[...] + _dot(jnp.tanh(_dot(mix(1), wd_ref[...])).astype(BF16), wup_ref[...])
    lw_ref[...] = -jnp.exp(-_softplus(-w_in) - 0.5)
    k = _dot(mix(2), wk_ref[...])
    v_ref[...] = _dot(mix(3), wv_ref[...])
    a = _sigmoid(a0_ref[...] + _dot(_dot(mix(4), wa_ref[...]).astype(BF16), aup_ref[...]))
    g_out_ref[...] = _dot(_sigmoid(_dot(mix(5), wg_ref[...])).astype(BF16), gup_ref[...])
    k_ref[...] = k * (1.0 + (a - 1.0) * ka_ref[...])
    lane = lax.broadcasted_iota(jnp.int32, (1, LANES), 1)
    for p in range(d // LANES):
        cols = slice(p * LANES, (p + 1) * LANES)
        kk = k[:, cols] * kk_ref[:, cols]
        kk = kk / jnp.maximum(jnp.sqrt(_head_sum(kk * kk, lane)), 1e-12)
        al_ref[:, cols] = kk
        be_ref[:, cols] = kk * a[:, cols]


def _rw_solve(a_ab, rhs, lc):
    nb = lc // R_SUB
    blocks = []
    for bi in range(nb):
        rows = slice(bi * R_SUB, (bi + 1) * R_SUB)
        x = rhs[rows, :]
        if bi > 0:
            prev = jnp.concatenate(blocks + [jnp.zeros(((nb - bi) * R_SUB, LANES), F32)], axis=0)
            x = x - _dot(a_ab[rows, :].astype(BF16), prev.astype(BF16))
        for s in range(R_SUB - 1):
            x = x - a_ab[rows, bi * R_SUB + s:bi * R_SUB + s + 1] * x[s:s + 1, :]
        blocks.append(x)
    return jnp.concatenate(blocks, axis=0)


def _rw_kernel(r_ref, lw_ref, k_ref, v_ref, al_ref, be_ref, g_ref, h_ref, rk_ref, lng_ref, lnb_ref, wout_ref,
               out_ref, ht_scr, ybuf, *, lt, lc):
    d = h_ref.shape[1]
    dh = R_HEAD_DIM
    npair = d // LANES

    @pl.when(pl.program_id(1) == 0)
    def _():
        ht_scr[...] = jnp.zeros_like(ht_scr)

    lane = lax.broadcasted_iota(jnp.int32, (1, LANES), 1)
    hmask = [lane < dh, lane >= dh]
    r_i = lax.broadcasted_iota(jnp.int32, (lc, lc), 0)
    c_i = lax.broadcasted_iota(jnp.int32, (lc, lc), 1)
    strict = c_i < r_i
    incl = c_i <= r_i
    tri_l = incl.astype(BF16)
    rr = lax.broadcasted_iota(jnp.int32, (LANES, LANES), 0)
    cc = lax.broadcasted_iota(jnp.int32, (LANES, LANES), 1)
    bdiag = (rr < dh) == (cc < dh)

    def chunk(c, carry):
        rows = pl.ds(pl.multiple_of(c * lc, lc), lc)
        lw = lw_ref[rows, :]
        cum = _dot_exact_l(tri_l, lw)
        cum_last = cum[lc - 1:lc, :]
        e_pos = jnp.exp(cum)
        e_neg = jnp.exp(-cum)
        e_rem = jnp.exp(cum_last - cum)
        r_t = r_ref[rows, :] * e_pos
        a_t = al_ref[rows, :] * jnp.exp(cum - lw)
        k_all = k_ref[rows, :]
        b_all = be_ref[rows, :]
        k_t, b_t = k_all * e_neg, b_all * e_neg
        k_h, b_h = k_all * e_rem, b_all * e_rem
        gam = jnp.exp(cum_last)

        for p in range(npair):
            cols = slice(p * LANES, (p + 1) * LANES)
            a_p, r_p = a_t[:, cols], r_t[:, cols]
            vp = v_ref[rows, cols].astype(BF16)
            zero = jnp.zeros_like(a_p)
            a_m = [jnp.where(hmask[e], a_p, zero) for e in range(2)]
            lhs = jnp.concatenate(a_m + [jnp.where(hmask[e], r_p, zero) for e in range(2)], axis=0)
            rhs = jnp.concatenate([b_t[:, cols], k_t[:, cols]], axis=0)
            sc = _dot_nt(lhs.astype(BF16), rhs.astype(BF16))
            xs, a_rb, a_rk = [], [], []
            for e in range(2):
                top = sc[e * lc:(e + 1) * lc, :]
                bot = sc[(2 + e) * lc:(3 + e) * lc, :]
                a_ab = jnp.where(strict, top[:, :lc], 0.0)
                a_ak = jnp.where(strict, top[:, lc:], 0.0)
                a_rb.append(jnp.where(incl, bot[:, :lc], 0.0))
                a_rk.append(jnp.where(incl, bot[:, lc:], 0.0))
                av = jnp.where(hmask[e], _dot(a_ak.astype(BF16), vp), 0.0)
                xs.append(_rw_solve(a_ab, a_m[e] + pltpu.roll(av, dh, axis=1), lc))
            w_pair = jnp.where(hmask[0], xs[0], xs[1])
            uv_pair = pltpu.roll(jnp.where(hmask[0], xs[1], xs[0]), dh, axis=1)
            ht = ht_scr[p]
            htb = ht.astype(BF16)
            u = _dot_nt(w_pair.astype(BF16), htb) + uv_pair
            ub = u.astype(BF16)
            vu = jnp.concatenate([vp, -ub], axis=0)
            y = _dot_nt(r_p.astype(BF16), htb)
            for e in range(2):
                wgt = jnp.concatenate([a_rk[e], a_rb[e]], axis=1).astype(BF16)
                y = y + jnp.where(hmask[e], _dot(wgt, vu), 0.0)
            ybuf[rows, cols] = y
            kb = jnp.concatenate([k_h[:, cols], b_h[:, cols]], axis=0).astype(BF16)
            upd = lax.dot_general(vu, kb, (((0,), (0,)), ((), ())), preferred_element_type=F32)
            ht_scr[p] = ht * gam[:, cols] + jnp.where(bdiag, upd, 0.0)
        return carry

    lax.fori_loop(0, lt // lc, chunk, 0)

    for p in range(npair):
        cols = slice(p * LANES, (p + 1) * LANES)
        y = ybuf[:, cols]
        r, k, v = r_ref[:, cols], k_ref[:, cols], v_ref[:, cols]
        bonus = _head_sum(r * k * rk_ref[:, cols], lane) * v
        yc = y - _head_sum(y, lane) * (1.0 / dh)
        yn = yc * lax.rsqrt(_head_sum(yc * yc, lane) * (1.0 / dh) + R_LN_EPS)
        ybuf[:, cols] = (yn * lng_ref[:, cols] + lnb_ref[:, cols] + bonus) * g_ref[:, cols]
    out_ref[...] = h_ref[...] + _dot(ybuf[...].astype(BF16), wout_ref[...])


def _rwkv_layer(h2d, bsz, seq, norm_g, mu, w_in, w0, w_up, a0, a_up, g_up, k_k, k_a, r_k, ln_g, ln_b, w_out,
                *, tm=256, lt=256, lc=64):
    t, d = h2d.shape
    c0 = 3 * d
    c1 = c0 + w_up.shape[0]
    c2 = c1 + a_up.shape[0]
    pad_c = lambda w, n: jnp.pad(w, ((0, 0), (0, n - w.shape[1]))).astype(BF16)
    pad_r = lambda w, n: jnp.pad(w, ((0, n - w.shape[0]), (0, 0))).astype(BF16)
    up = lambda n: -(-n // LANES) * LANES
    nw, na, ng = up(c1 - c0), up(c2 - c1), up(w_in.shape[1] - c2)
    wr, wk, wv = (w_in[:, j * d:(j + 1) * d].astype(BF16) for j in range(3))
    wd, wa, wg = pad_c(w_in[:, c0:c1], nw), pad_c(w_in[:, c1:c2], na), pad_c(w_in[:, c2:], ng)
    row = lambda a: a.reshape(1, d)
    tile = pl.BlockSpec((tm, d), lambda i: (i, 0))
    outs = pl.pallas_call(
        functools.partial(_rw_proj_kernel, tm=tm, seq=seq),
        grid=(t // tm,),
        in_specs=[tile, _const_spec((1, d)), _const_spec((6, d)), _const_spec((d, d)), _const_spec((d, d)),
                  _const_spec((d, d)), _const_spec((d, nw)), _const_spec((d, na)), _const_spec((d, ng)),
                  _const_spec((nw, d)), _const_spec((na, d)), _const_spec((ng, d)),
                  _const_spec((1, d)), _const_spec((1, d)), _const_spec((1, d)), _const_spec((1, d))],
        out_specs=[tile] * 7,
        out_shape=[jax.ShapeDtypeStruct((t, d), F32)] * 7,
        scratch_shapes=[pltpu.VMEM((tm + 8, d), F32)],
        compiler_params=_cparams("arbitrary"),
        name="rw_proj",
    )(h2d, row(norm_g), mu, wr, wk, wv, wd, wa, wg, pad_r(w_up, nw), pad_r(a_up, na), pad_r(g_up, ng),
      row(w0), row(a0), row(k_k), row(k_a))
    r, lw, k, v, al, be, g = outs

    nt = seq // lt
    spec = pl.BlockSpec((lt, d), lambda b, i: (b * nt + i, 0))
    return pl.pallas_call(
        functools.partial(_rw_kernel, lt=lt, lc=lc),
        grid=(bsz, nt),
        in_specs=[spec] * 8 + [_const_spec((1, d))] * 3 + [_const_spec((d, d))],
        out_specs=spec,
        out_shape=jax.ShapeDtypeStruct((t, d), F32),
        scratch_shapes=[pltpu.VMEM((d // LANES, LANES, LANES), F32), pltpu.VMEM((lt, d), F32)],
        compiler_params=_cparams("parallel", "arbitrary"),
        name="rw_mix",
    )(r, lw, k, v, al, be, g, h2d, r_k.reshape(1, d), row(ln_g), row(ln_b), w_out.astype(BF16))


def kernel(x, mlp_norm_g, mlp_w_up, mlp_w_down, final_norm_g, s5_norm_g, s5_a_re, s5_a_im, s5_log_dt, s5_b_re, s5_b_im, s5_c_re, s5_c_im, s5_d, s5_w_glu, s5_b_glu, ml_norm_g, ml_w_in, ml_conv_w, ml_conv_b, ml_b_i, ml_b_f, ml_head_g, ml_w_out, fox_norm_g, fox_w_in, fox_b_f, fox_w_out, rw_norm_g, rw_mu, rw_w_in, rw_w0, rw_w_up, rw_a0, rw_a_up, rw_g_up, rw_k_k, rw_k_a, rw_r_k, rw_ln_g, rw_ln_b, rw_w_out):
    bsz, seq, d = x.shape
    depth = mlp_w_up.shape[0]
    h = x.reshape(bsz * seq, d)
    for i in range(depth):
        kind, j = i % 4, i // 4
        if kind == 0:
            h = _s5_layer(h, bsz, seq, s5_norm_g[j], s5_a_re[j], s5_a_im[j], s5_log_dt[j], s5_b_re[j], s5_b_im[j],
                          s5_c_re[j], s5_c_im[j], s5_d[j], s5_w_glu[j], s5_b_glu[j])
        elif kind == 1:
            h = _mlstm_layer(h, bsz, seq, ml_norm_g[j], ml_w_in[j], ml_conv_w[j], ml_conv_b[j], ml_b_i[j],
                             ml_b_f[j], ml_head_g[j], ml_w_out[j])
        elif kind == 2:
            h = _fox_layer(h, bsz, seq, fox_norm_g[j], fox_w_in[j], fox_b_f[j], fox_w_out[j])
        else:
            h = _rwkv_layer(h, bsz, seq, rw_norm_g[j], rw_mu[j], rw_w_in[j], rw_w0[j], rw_w_up[j], rw_a0[j],
                            rw_a_up[j], rw_g_up[j], rw_k_k[j], rw_k_a[j], rw_r_k[j], rw_ln_g[j], rw_ln_b[j],
                            rw_w_out[j])
        h = _mlp_layer(h, mlp_norm_g[i], mlp_w_up[i], mlp_w_down[i],
                       final_g=final_norm_g if i == depth - 1 else None)
    return h.reshape(bsz, seq, d)
```

```python
import functools
import math

import jax
import jax.numpy as jnp
from jax import lax
from jax.experimental import pallas as pl
from jax.experimental.pallas import tpu as pltpu

F32 = jnp.float32
BF16 = jnp.bfloat16

NORM_EPS = 1e-6
NEG_BIG = -1e30
LANES = 128
VMEM_LIMIT = 56 * 1024 * 1024

M_HEADS = 8
M_QK_DIM = 64
M_V_DIM = 128
M_CONV = 4
M_NORM_EPS = 1e-6


def _cparams(*sem):
    return pltpu.CompilerParams(dimension_semantics=sem, vmem_limit_bytes=VMEM_LIMIT)


def _rms(x, g):
    return x * lax.rsqrt(jnp.mean(x * x, axis=-1, keepdims=True) + NORM_EPS) * g


def _dot(a, b):
    return jnp.dot(a, b, preferred_element_type=F32)


def _dot_nt(a, b):
    return lax.dot_general(a, b, (((1,), (1,)), ((), ())), preferred_element_type=F32)


def _split3(x):
    hi = x.astype(BF16)
    r = x - hi.astype(F32)
    mid = r.astype(BF16)
    lo = (r - mid.astype(F32)).astype(BF16)
    return hi, mid, lo


def _dot_exact_l(ones_mat, x):
    hi, mid, lo = _split3(x)
    return _dot(ones_mat, hi) + _dot(ones_mat, mid) + _dot(ones_mat, lo)


def _dot_exact_r(x, ones_mat):
    hi, mid, lo = _split3(x)
    return _dot(hi, ones_mat) + _dot(mid, ones_mat) + _dot(lo, ones_mat)


def _log_sigmoid(x):
    return jnp.minimum(x, 0.0) - jnp.log(1.0 + jnp.exp(-jnp.abs(x)))


def _sigmoid(x):
    return 1.0 / (1.0 + jnp.exp(-x))


def _const_spec(shape):
    nd = len(shape)
    return pl.BlockSpec(shape, lambda *_: (0,) * nd)


def _mlp_kernel(h_ref, g_ref, wup_ref, wdn_ref, fg_ref, out_ref, *, fc, final):
    x = h_ref[...]
    xn = _rms(x, g_ref[...]).astype(BF16)
    acc = x
    for c in range(wup_ref.shape[1] // fc):
        hid = jnp.maximum(_dot(xn, wup_ref[:, c * fc:(c + 1) * fc]), 0.0)
        acc = acc + _dot((hid * hid).astype(BF16), wdn_ref[c * fc:(c + 1) * fc, :])
    out_ref[...] = _rms(acc, fg_ref[...]) if final else acc


def _mlp_layer(h2d, norm_g, w_up, w_down, final_g=None, *, tm=512, fc=512):
    t, d = h2d.shape
    ff = w_up.shape[1]
    fg = (norm_g if final_g is None else final_g).reshape(1, d)
    return pl.pallas_call(
        functools.partial(_mlp_kernel, fc=fc, final=final_g is not None),
        grid=(t // tm,),
        in_specs=[pl.BlockSpec((tm, d), lambda i: (i, 0)), _const_spec((1, d)),
                  _const_spec((d, ff)), _const_spec((ff, d)), _const_spec((1, d))],
        out_specs=pl.BlockSpec((tm, d), lambda i: (i, 0)),
        out_shape=jax.ShapeDtypeStruct((t, d), F32),
        compiler_params=_cparams("parallel"),
        name="mlp",
    )(h2d, norm_g.reshape(1, d), w_up.astype(BF16), w_down.astype(BF16), fg)


def _ml_proj_kernel(h_ref, g_ref, wqk_ref, wv_ref, wo_ref, wg_ref, wgt_ref,
                    qk_ref, v_ref, o_ref, gc_ref, gr_ref):
    xn = _rms(h_ref[...], g_ref[...]).astype(BF16)
    qk_ref[...] = _dot(xn, wqk_ref[...])
    v_ref[...] = _dot(xn, wv_ref[...]).astype(v_ref.dtype)
    o_ref[...] = _dot(xn, wo_ref[...])
    gc_ref[...] = _dot(xn, wg_ref[...])
    gr_ref[...] = _dot_nt(wgt_ref[...], xn)


def _ml_kernel(qk_ref, v_ref, o_ref, gc_ref, gr_ref, h_ref, cw_ref, cb_ref, bcol_ref, brow_ref,
               hg_ref, wout_ref, out_ref, ext_scr, c_scr, m_scr, hbuf, *, lt, lc):
    nh, dk, dv = M_HEADS, M_QK_DIM, M_V_DIM
    hd = nh * dk

    @pl.when(pl.program_id(1) == 0)
    def _():
        c_scr[...] = jnp.zeros_like(c_scr)
        m_scr[...] = jnp.zeros_like(m_scr)
        ext_scr[0:8, :] = jnp.zeros((8, 2 * hd), F32)

    ext_scr[8:8 + lt, :] = qk_ref[...]
    conv = cb_ref[...] + jnp.zeros((lt, 2 * hd), F32)
    for kk in range(M_CONV):
        conv = conv + cw_ref[kk:kk + 1, :] * ext_scr[8 - (M_CONV - 1) + kk:8 - (M_CONV - 1) + kk + lt, :]
    ext_scr[0:8, :] = ext_scr[lt:lt + 8, :]
    act = conv * _sigmoid(conv)

    lane = lax.broadcasted_iota(jnp.int32, (1, LANES), 1)
    r_i = lax.broadcasted_iota(jnp.int32, (lc, lc), 0)
    c_i = lax.broadcasted_iota(jnp.int32, (lc, lc), 1)
    causal = c_i <= r_i
    tri_l = causal.astype(BF16)
    tri_u = (r_i <= c_i).astype(BF16)
    ones_col = (lax.broadcasted_iota(jnp.int32, (lc, LANES), 1) == 0).astype(BF16)
    row16 = lax.broadcasted_iota(jnp.int32, (2 * nh, 1), 0)

    for c in range(lt // lc):
        rows = slice(c * lc, (c + 1) * lc)
        gcb = gc_ref[rows, :] + bcol_ref[...]
        gate_c = jnp.where(lane < nh, gcb, jnp.where(lane < 2 * nh, _log_sigmoid(gcb), 0.0))
        cum_c = _dot_exact_l(tri_l, jnp.where(lane < nh, 0.0, gate_c))
        grb = gr_ref[:, rows] + brow_ref[...]
        gate_r = jnp.where(row16 < nh, grb, _log_sigmoid(grb))
        cum_r = _dot_exact_r(jnp.where(row16 < nh, 0.0, gate_r), tri_u)

        for h in range(nh):
            p, e = h // 2, h % 2
            head_mask = (lane >= e * dk) & (lane < (e + 1) * dk)
            q_m = jnp.where(head_mask, act[rows, p * LANES:(p + 1) * LANES], 0.0) * (dk ** -0.5)
            k_m = jnp.where(head_mask, act[rows, hd + p * LANES:hd + (p + 1) * LANES], 0.0)
            v_aug = jnp.concatenate([v_ref[rows, h * dv:(h + 1) * dv].astype(BF16), ones_col], axis=1)

            b_col = cum_c[:, nh + h:nh + h + 1]
            li_col = gate_c[:, h:h + 1]
            b_row = cum_r[nh + h:nh + h + 1, :]
            li_row = gate_r[h:h + 1, :]
            m_prev = m_scr[h:h + 1, 0:1]

            g = b_col + m_prev
            d_log = jnp.where(causal, b_col - b_row + li_row, NEG_BIG)
            m_t = jnp.maximum(g, jnp.max(d_log, axis=-1, keepdims=True))
            w_inter = jnp.exp(g - m_t)
            s = _dot_nt(q_m.astype(BF16), k_m.astype(BF16)) * jnp.exp(d_log - m_t)
            c_aug = c_scr[h]
            r = w_inter * _dot(q_m.astype(BF16), c_aug.astype(BF16)) + _dot(s.astype(BF16), v_aug)
            den = r[:, dv:dv + 1]
            hh = r[:, :dv] / jnp.maximum(jnp.abs(den), jnp.exp(-m_t))

            m_new = m_t[lc - 1:lc, :]
            b_last = b_col[lc - 1:lc, :]
            w_state = jnp.exp(b_last - b_col + li_col - m_new)
            decay = jnp.exp(b_last + m_prev - m_new)
            kw_t = jnp.transpose(k_m * w_state).astype(BF16)
            c_scr[h] = decay * c_aug + _dot(kw_t, v_aug)
            m_scr[h:h + 1, :] = jnp.broadcast_to(m_new, (1, LANES))

            yc = hh - jnp.mean(hh, axis=-1, keepdims=True)
            yn = yc * lax.rsqrt(jnp.mean(yc * yc, axis=-1, keepdims=True) + M_NORM_EPS)
            cols = slice(h * dv, (h + 1) * dv)
            hbuf[rows, cols] = yn * hg_ref[:, cols] * _sigmoid(o_ref[rows, cols])

    out_ref[...] = h_ref[...] + _dot(hbuf[...].astype(BF16), wout_ref[...])


def _mlstm_layer(h2d, bsz, seq, norm_g, w_in, conv_w, conv_b, b_i, b_f, head_g, w_out, *, tm=512, lt=256, lc=64):
    t, d = h2d.shape
    nh, dk, dv = M_HEADS, M_QK_DIM, M_V_DIM
    o0 = 2 * nh * dk
    o1 = o0 + nh * dv
    o3 = o1 + 2 * nh
    wqk = w_in[:, :o0].astype(BF16)
    wv = w_in[:, o0:o1].astype(BF16)
    wo = w_in[:, o3:].astype(BF16)
    wg = jnp.pad(w_in[:, o1:o3], ((0, 0), (0, LANES - 2 * nh))).astype(BF16)
    wgt = jnp.transpose(w_in[:, o1:o3]).astype(BF16)
    g2 = norm_g.reshape(1, d)

    qk, v, o, gc, gr = pl.pallas_call(
        _ml_proj_kernel,
        grid=(t // tm,),
        in_specs=[pl.BlockSpec((tm, d), lambda i: (i, 0)), _const_spec((1, d)),
                  _const_spec(wqk.shape), _const_spec(wv.shape), _const_spec(wo.shape),
                  _const_spec(wg.shape), _const_spec(wgt.shape)],
        out_specs=[pl.BlockSpec((tm, o0), lambda i: (i, 0)), pl.BlockSpec((tm, nh * dv), lambda i: (i, 0)),
                   pl.BlockSpec((tm, nh * dv), lambda i: (i, 0)), pl.BlockSpec((tm, LANES), lambda i: (i, 0)),
                   pl.BlockSpec((2 * nh, tm), lambda i: (0, i))],
        out_shape=[jax.ShapeDtypeStruct((t, o0), F32), jax.ShapeDtypeStruct((t, nh * dv), BF16),
                   jax.ShapeDtypeStruct((t, nh * dv), F32), jax.ShapeDtypeStruct((t, LANES), F32),
                   jax.ShapeDtypeStruct((2 * nh, t), F32)],
        compiler_params=_cparams("parallel"),
        name="ml_proj",
    )(h2d, g2, wqk, wv, wo, wg, wgt)

    bcol = jnp.pad(jnp.concatenate([b_i, b_f]), (0, LANES - 2 * nh)).reshape(1, LANES)
    brow = jnp.concatenate([b_i, b_f]).reshape(2 * nh, 1)
    nt = seq // lt
    row_spec = lambda w: pl.BlockSpec((lt, w), lambda b, i: (b * nt + i, 0))
    return pl.pallas_call(
        functools.partial(_ml_kernel, lt=lt, lc=lc),
        grid=(bsz, nt),
        in_specs=[row_spec(o0), row_spec(nh * dv), row_spec(nh * dv), row_spec(LANES),
                  pl.BlockSpec((2 * nh, lt), lambda b, i: (0, b * nt + i)), row_spec(d),
                  _const_spec((M_CONV, o0)), _const_spec((1, o0)), _const_spec((1, LANES)),
                  _const_spec((2 * nh, 1)), _const_spec((1, nh * dv)), _const_spec((nh * dv, d))],
        out_specs=row_spec(d),
        out_shape=jax.ShapeDtypeStruct((t, d), F32),
        scratch_shapes=[pltpu.VMEM((lt + 8, o0), F32), pltpu.VMEM((nh, LANES, 2 * LANES), F32),
                        pltpu.VMEM((nh, LANES), F32), pltpu.VMEM((lt, nh * dv), F32)],
        compiler_params=_cparams("parallel", "arbitrary"),
        name="ml_mix",
    )(qk, v, o, gc, gr, h2d, conv_w, conv_b.reshape(1, o0), bcol, brow,
      head_g.reshape(1, nh * dv), w_out.astype(BF16))


S5_GROUP_CH = 16
S5_STATE = 64
S5_CHUNK_GROUPS = 4
S5_CHUNK = S5_CHUNK_GROUPS * S5_STATE


def _s5_disc_kernel(ar_ref, ai_ref, ldt_ref, arx_ref, aix_ref, br_ref, bi_ref,
                    abr_ref, abi_ref, bbr_ref, bbi_ref):
    dt = jnp.exp(ldt_ref[...])

    def abar(ar, ai):
        mag = jnp.exp(dt * ar)
        return mag * jnp.cos(dt * ai), mag * jnp.sin(dt * ai)

    abr_ref[...], abi_ref[...] = abar(ar_ref[...], ai_ref[...])
    ar, ai = arx_ref[...], aix_ref[...]
    xr, xi = abar(ar, ai)
    inv = 1.0 / (ar * ar + ai * ai)
    zr, zi = xr - 1.0, xi
    coef_re = (zr * ar + zi * ai) * inv
    coef_im = (zi * ar - zr * ai) * inv
    br, bi = br_ref[...], bi_ref[...]
    bbr_ref[...] = coef_re * br - coef_im * bi
    bbi_ref[...] = coef_re * bi + coef_im * br


def _s5_kernel(h_ref, g_ref, abr_ref, abi_ref, bre_ref, bim_ref, cre_ref, cim_ref, dsk_ref, wglu_ref, bglu_ref,
               out_ref, re_scr, im_scr, sr_scr, si_scr, *, tt, bsz):
    d = h_ref.shape[1]
    nchunk = re_scr.shape[0]
    per_slab = nchunk // (d // S5_CHUNK)

    @pl.when(pl.program_id(0) == 0)
    def _():
        sr_scr[...] = jnp.zeros_like(sr_scr)
        si_scr[...] = jnp.zeros_like(si_scr)

    x = h_ref[...]
    u = _rms(x, g_ref[...])
    ub = u.astype(BF16)
    for c in range(nchunk):
        slab = c // per_slab
        us = ub[:, slab * S5_CHUNK:(slab + 1) * S5_CHUNK]
        re_scr[c] = _dot(us, bre_ref[c])
        im_scr[c] = _dot(us, bim_ref[c])

    def scan_chunk(c, carry):
        ar, ai = abr_ref[c], abi_ref[c]
        sr, si = sr_scr[c], si_scr[c]
        for t in range(tt):
            rows = slice(t * bsz, (t + 1) * bsz)
            nr = ar * sr - ai * si + re_scr[c, rows, :]
            ni = ar * si + ai * sr + im_scr[c, rows, :]
            re_scr[c, rows, :] = nr
            im_scr[c, rows, :] = ni
            sr, si = nr, ni
        sr_scr[c] = sr
        si_scr[c] = si
        return carry

    lax.fori_loop(0, nchunk, scan_chunk, 0)

    ys = []
    for slab in range(d // S5_CHUNK):
        y = jnp.zeros((tt * bsz, S5_CHUNK), F32)
        for c in range(slab * per_slab, (slab + 1) * per_slab):
            y = y + _dot(re_scr[c].astype(BF16), cre_ref[c]) - _dot(im_scr[c].astype(BF16), cim_ref[c])
        ys.append(y)
    y = jnp.concatenate(ys, axis=1) + dsk_ref[...] * u
    gelu = 0.5 * y * (1.0 + jnp.tanh(math.sqrt(2.0 / math.pi) * (y + 0.044715 * (y * y * y))))
    z = _dot(gelu.astype(BF16), wglu_ref[...]) + bglu_ref[...]
    out_ref[...] = x + z[:, :d] * _sigmoid(z[:, d:])


def _s5_layer(h2d, bsz, seq, norm_g, a_re, a_im, log_dt, b_re, b_im, c_re, c_im, d_skip, w_glu, b_glu, *, tt=8):
    t, d = h2d.shape
    ng, ns, hg = a_re.shape[0], S5_STATE, S5_GROUP_CH
    cg = S5_CHUNK_GROUPS
    nchunk = ng // cg
    per_slab = S5_CHUNK // (cg * hg)

    rep = lambda a: jnp.repeat(a, hg, axis=1)
    abr, abi, bbr, bbi = pl.pallas_call(
        _s5_disc_kernel,
        out_shape=[jax.ShapeDtypeStruct((ng, ns), F32)] * 2 + [jax.ShapeDtypeStruct((ng, ns * hg), F32)] * 2,
        name="s5_disc",
    )(a_re, a_im, log_dt.reshape(ng, 1), rep(a_re), rep(a_im), b_re.reshape(ng, ns * hg), b_im.reshape(ng, ns * hg))

    eye = jnp.eye(cg, dtype=F32)
    place = jax.nn.one_hot(jnp.arange(nchunk) % per_slab, per_slab, dtype=F32)

    def b_blocks(bb):
        m = bb.reshape(nchunk, cg, ns, hg)
        blk = jnp.einsum('cgph,gq->cghqp', m, eye).reshape(nchunk, cg * hg, cg * ns)
        return jnp.einsum('crs,cj->cjrs', blk, place).reshape(nchunk, per_slab * cg * hg, cg * ns).astype(BF16)

    def c_blocks(cc):
        m = cc.reshape(nchunk, cg, hg, ns)
        blk = jnp.einsum('cghp,gq->cgpqh', m, eye).reshape(nchunk, cg * ns, cg * hg)
        return jnp.einsum('csr,cj->csjr', blk, place).reshape(nchunk, cg * ns, per_slab * cg * hg).astype(BF16)

    rows = tt * bsz
    ht = h2d.reshape(bsz, seq, d).transpose(1, 0, 2).reshape(t, d)
    out = pl.pallas_call(
        functools.partial(_s5_kernel, tt=tt, bsz=bsz),
        grid=(seq // tt,),
        in_specs=[pl.BlockSpec((rows, d), lambda i: (i, 0)), _const_spec((1, d)),
                  _const_spec((nchunk, 1, S5_CHUNK)), _const_spec((nchunk, 1, S5_CHUNK)),
                  _const_spec((nchunk, S5_CHUNK, S5_CHUNK)), _const_spec((nchunk, S5_CHUNK, S5_CHUNK)),
                  _const_spec((nchunk, S5_CHUNK, S5_CHUNK)), _const_spec((nchunk, S5_CHUNK, S5_CHUNK)),
                  _const_spec((1, d)), _const_spec((d, 2 * d)), _const_spec((1, 2 * d))],
        out_specs=pl.BlockSpec((rows, d), lambda i: (i, 0)),
        out_shape=jax.ShapeDtypeStruct((t, d), F32),
        scratch_shapes=[pltpu.VMEM((nchunk, rows, S5_CHUNK), F32), pltpu.VMEM((nchunk, rows, S5_CHUNK), F32),
                        pltpu.VMEM((nchunk, bsz, S5_CHUNK), F32), pltpu.VMEM((nchunk, bsz, S5_CHUNK), F32)],
        compiler_params=_cparams("arbitrary"),
        name="s5_mix",
    )(ht, norm_g.reshape(1, d), abr.reshape(nchunk, 1, S5_CHUNK), abi.reshape(nchunk, 1, S5_CHUNK),
      b_blocks(bbr), b_blocks(bbi), c_blocks(c_re), c_blocks(c_im),
      d_skip.reshape(1, d), w_glu.astype(BF16), b_glu.reshape(1, 2 * d))
    return out.reshape(seq, bsz, d).transpose(1, 0, 2).reshape(t, d)


F_HEADS = 16
F_HEAD_DIM = 64
F_QROWS = 256


def _fox_proj_kernel(h_ref, g_ref, wq_ref, wk_ref, wv_ref, wo_ref, wgt_ref,
                     q_ref, k_ref, v_ref, o_ref, gr_ref):
    xn = _rms(h_ref[...], g_ref[...]).astype(BF16)
    q_ref[...] = (_dot(xn, wq_ref[...]) * (F_HEAD_DIM ** -0.5)).astype(BF16)
    k_ref[...] = _dot(xn, wk_ref[...]).astype(BF16)
    v_ref[...] = _dot(xn, wv_ref[...]).astype(BF16)
    o_ref[...] = _dot(xn, wo_ref[...])
    gr_ref[...] = _dot_nt(wgt_ref[...], xn)


def _fox_kernel(q_ref, k_ref, v_ref, o_ref, gr_ref, h_ref, brow_ref, wout_ref,
                out_ref, cr_scr, obuf, m_scr, l_scr, acc_scr, *, tq, tk, seq):
    nh, dh = F_HEADS, F_HEAD_DIM
    i = pl.program_id(1)
    lane = lax.broadcasted_iota(jnp.int32, (1, LANES), 1)
    hmask = [lane < dh, lane >= dh]
    th = F_QROWS

    @pl.when(i == 0)
    def _():
        r_i = lax.broadcasted_iota(jnp.int32, (LANES, LANES), 0)
        c_i = lax.broadcasted_iota(jnp.int32, (LANES, LANES), 1)
        tri_u = (r_i <= c_i).astype(BF16)
        carry_r = jnp.zeros((nh, 1), F32)
        per = tk // LANES
        for blk in range(seq // LANES):
            rows = slice(blk * LANES, (blk + 1) * LANES)
            cr = _dot_exact_r(_log_sigmoid(gr_ref[:, rows] + brow_ref[...]), tri_u) + carry_r
            cr_scr[blk // per, :, (blk % per) * LANES:(blk % per + 1) * LANES] = cr
            carry_r = cr[:, LANES - 1:LANES]

    r_i = lax.broadcasted_iota(jnp.int32, (th, tk), 0)
    c_i = lax.broadcasted_iota(jnp.int32, (th, tk), 1)

    m_scr[...] = jnp.full(m_scr.shape, NEG_BIG, F32)
    l_scr[...] = jnp.zeros_like(l_scr)
    acc_scr[...] = jnp.zeros_like(acc_scr)

    def step(j, masked):
        k0 = pl.multiple_of(j * tk, tk)
        crow = cr_scr[j]
        for p in range(nh // 2):
            cols = slice(p * LANES, (p + 1) * LANES)
            kp = k_ref[pl.ds(k0, tk), cols]
            vp = v_ref[pl.ds(k0, tk), cols]
            for half in range(tq // th):
                qrows = slice(half * th, (half + 1) * th)
                qp = q_ref[qrows, cols]
                acc = acc_scr[p, qrows, :]
                scaled = []
                for e in range(2):
                    h = 2 * p + e
                    q_m = jnp.where(hmask[e], qp, jnp.zeros_like(qp))
                    s = _dot_nt(q_m, kp) - crow[h:h + 1, :]
                    if masked:
                        s = jnp.where(c_i <= r_i + half * th, s, NEG_BIG)
                    m_prev = m_scr[h, qrows, :]
                    m_next = jnp.maximum(m_prev, jnp.max(s, axis=-1, keepdims=True))
                    alpha = jnp.exp(m_prev - m_next)
                    pm = jnp.exp(s - jnp.concatenate([m_next] * (tk // LANES), axis=1))
                    l_scr[h, qrows, :] = alpha * l_scr[h, qrows, :] + jnp.sum(pm, axis=-1, keepdims=True)
                    m_scr[h, qrows, :] = m_next
                    scaled.append(acc * alpha + _dot(pm.astype(BF16), vp))
                acc_scr[p, qrows, :] = jnp.where(hmask[0], scaled[0], scaled[1])

    def body(j, carry):
        step(j, False)
        return carry

    lax.fori_loop(0, i, body, 0)
    step(i, True)
    for p in range(nh // 2):
        cols = slice(p * LANES, (p + 1) * LANES)
        denom = jnp.where(hmask[0], l_scr[2 * p], l_scr[2 * p + 1])
        obuf[:, cols] = acc_scr[p] / denom * _sigmoid(o_ref[:, cols])

    out_ref[...] = h_ref[...] + _dot(obuf[...].astype(BF16), wout_ref[...])


def _fox_layer(h2d, bsz, seq, norm_g, w_in, b_f, w_out, *, tm=512, tq=256):
    t, d = h2d.shape
    nh, dh = F_HEADS, F_HEAD_DIM
    hd = nh * dh
    wq, wk, wv = (w_in[:, j * hd:(j + 1) * hd].astype(BF16) for j in range(3))
    wf = w_in[:, 3 * hd:3 * hd + nh]
    wo = w_in[:, 3 * hd + nh:].astype(BF16)
    wgt = jnp.transpose(wf).astype(BF16)
    tile = lambda w: pl.BlockSpec((tm, w), lambda i: (i, 0))
    q, k, v, o, gr = pl.pallas_call(
        _fox_proj_kernel,
        grid=(t // tm,),
        in_specs=[tile(d), _const_spec((1, d)), _const_spec(wq.shape), _const_spec(wk.shape),
                  _const_spec(wv.shape), _const_spec(wo.shape), _const_spec(wgt.shape)],
        out_specs=[tile(hd), tile(hd), tile(hd), tile(hd), pl.BlockSpec((nh, tm), lambda i: (0, i))],
        out_shape=[jax.ShapeDtypeStruct((t, hd), BF16)] * 3 + [jax.ShapeDtypeStruct((t, hd), F32),
                   jax.ShapeDtypeStruct((nh, t), F32)],
        compiler_params=_cparams("parallel"),
        name="fox_proj",
    )(h2d, norm_g.reshape(1, d), wq, wk, wv, wo, wgt)

    nt = seq // tq
    q_spec = lambda w: pl.BlockSpec((tq, w), lambda b, i: (b * nt + i, 0))
    seq_spec = lambda w: pl.BlockSpec((seq, w), lambda b, i: (b, 0))
    return pl.pallas_call(
        functools.partial(_fox_kernel, tq=tq, tk=tq, seq=seq),
        grid=(bsz, nt),
        in_specs=[q_spec(hd), seq_spec(hd), seq_spec(hd), q_spec(hd),
                  pl.BlockSpec((nh, seq), lambda b, i: (0, b)), q_spec(d),
                  _const_spec((nh, 1)), _const_spec((hd, d))],
        out_specs=q_spec(d),
        out_shape=jax.ShapeDtypeStruct((t, d), F32),
        scratch_shapes=[pltpu.VMEM((nt, nh, tq), F32), pltpu.VMEM((tq, hd), F32),
                        pltpu.VMEM((nh, tq, LANES), F32), pltpu.VMEM((nh, tq, LANES), F32),
                        pltpu.VMEM((nh // 2, tq, LANES), F32)],
        compiler_params=_cparams("parallel", "arbitrary"),
        name="fox_attn",
    )(q, k, v, o, gr, h2d, b_f.reshape(nh, 1), w_out.astype(BF16))


R_HEADS = 16
R_HEAD_DIM = 64
R_LN_EPS = 64e-5
R_SUB = 16
R_HALF = R_SUB // 2


def _softplus(z):
    return jnp.maximum(z, 0.0) + jnp.log(1.0 + jnp.exp(-jnp.abs(z)))


def _head_sum(x, lane):
    lo = lane < R_HEAD_DIM
    s0 = jnp.sum(jnp.where(lo, x, 0.0), axis=-1, keepdims=True)
    s1 = jnp.sum(jnp.where(lo, 0.0, x), axis=-1, keepdims=True)
    return jnp.where(lo, s0, s1)


def _rw_proj_kernel(h_ref, g_ref, mu_ref, wr_ref, wk_ref, wv_ref, wd_ref, wa_ref, wg_ref, wup_ref, aup_ref,
                    gup_ref, w0_ref, a0_ref, kk_ref, ka_ref,
                    r_ref, lw_ref, k_ref, v_ref, al_ref, be_ref, g_out_ref, ext_scr, *, tm, seq):
    d = h_ref.shape[1]
    i = pl.program_id(0)
    xn = _rms(h_ref[...], g_ref[...])
    @pl.when((i * tm) % seq == 0)
    def _():
        ext_scr[7:8, :] = jnp.zeros((1, d), F32)

    @pl.when((i * tm) % seq != 0)
    def _():
        ext_scr[7:8, :] = ext_scr[7 + tm:8 + tm, :]

    ext_scr[8:8 + tm, :] = xn
    dx = ext_scr[7:7 + tm, :] - xn
    mix = lambda j: (xn + dx * mu_ref[j:j + 1, :]).astype(BF16)
    r_ref[...] = _dot(mix(0), wr_ref[...])
    w_in = w0_ref[...] + _dot(jnp.tanh(_dot(mix(1), wd_ref[...])).astype(BF16), wup_ref[...])
    lw_ref[...] = -jnp.exp(-_softplus(-w_in) - 0.5)
    k = _dot(mix(2), wk_ref[...])
    v_ref[...] = _dot(mix(3), wv_ref[...])
    a = _sigmoid(a0_ref[...] + _dot(_dot(mix(4), wa_ref[...]).astype(BF16), aup_ref[...]))
    g_out_ref[...] = _dot(_sigmoid(_dot(mix(5), wg_ref[...])).astype(BF16), gup_ref[...])
    k_ref[...] = k * (1.0 + (a - 1.0) * ka_ref[...])
    lane = lax.broadcasted_iota(jnp.int32, (1, LANES), 1)
    for p in range(d // LANES):
        cols = slice(p * LANES, (p + 1) * LANES)
        kk = k[:, cols] * kk_ref[:, cols]
        kk = kk / jnp.maximum(jnp.sqrt(_head_sum(kk * kk, lane)), 1e-12)
        al_ref[:, cols] = kk
        be_ref[:, cols] = kk * a[:, cols]


def _head_sum_mxu(x, ones_bd):
    hi = x.astype(BF16)
    lo = (x - hi.astype(F32)).astype(BF16)
    return _dot(jnp.concatenate([hi, lo], axis=1), ones_bd)


def _rw_kernel(r_ref, lw_ref, k_ref, v_ref, al_ref, be_ref, g_ref, h_ref, rk_ref, lng_ref, lnb_ref, wout_ref,
               sel_ref, out_ref, ht_scr, ybuf, aab_scr, aoff_scr, arkb_scr, x_scr, coef_scr, *, lt, lc):
    d = h_ref.shape[1]
    dh = R_HEAD_DIM
    npair = d // LANES
    nb = lc // R_SUB

    @pl.when(pl.program_id(1) == 0)
    def _():
        ht_scr[...] = jnp.zeros_like(ht_scr)

    lane = lax.broadcasted_iota(jnp.int32, (1, LANES), 1)
    hmask = [lane < dh, lane >= dh]
    lane2 = lax.broadcasted_iota(jnp.int32, (1, 2 * LANES), 1) % LANES
    hmask2 = [lane2 < dh, lane2 >= dh]
    r_i = lax.broadcasted_iota(jnp.int32, (lc, lc), 0)
    c_i = lax.broadcasted_iota(jnp.int32, (lc, lc), 1)
    tri_l = (c_i <= r_i).astype(BF16)
    r_p2 = lax.broadcasted_iota(jnp.int32, (lc, LANES), 0)
    c_p2 = lax.broadcasted_iota(jnp.int32, (lc, LANES), 1) % lc
    strict2 = c_p2 < r_p2
    incl2 = c_p2 <= r_p2
    below2 = (c_p2 // R_SUB) < (r_p2 // R_SUB)
    rr = lax.broadcasted_iota(jnp.int32, (LANES, LANES), 0)
    cc = lax.broadcasted_iota(jnp.int32, (LANES, LANES), 1)
    bdiag = (rr < dh) == (cc < dh)

    def chunk(c, carry):
        rows = pl.ds(pl.multiple_of(c * lc, lc), lc)
        lw = lw_ref[rows, :]
        cum = _dot_exact_l(tri_l, lw)
        cum_last = cum[lc - 1:lc, :]
        e_pos = jnp.exp(cum)
        e_neg = jnp.exp(-cum)
        e_rem = jnp.exp(cum_last - cum)
        r_t = r_ref[rows, :] * e_pos
        a_t = al_ref[rows, :] * jnp.exp(cum - lw)
        k_all = k_ref[rows, :]
        b_all = be_ref[rows, :]
        k_t, b_t = k_all * e_neg, b_all * e_neg
        k_h, b_h = k_all * e_rem, b_all * e_rem
        gam = jnp.exp(cum_last)

        nheads = 2 * npair
        vps = [v_ref[rows, p * LANES:(p + 1) * LANES].astype(BF16) for p in range(npair)]
        vp2s = [jnp.concatenate([vp, vp], axis=0) for vp in vps]

        scs = []
        for p in range(npair):
            cols = slice(p * LANES, (p + 1) * LANES)
            lhs = jnp.concatenate([a_t[:, cols], r_t[:, cols]], axis=0).astype(BF16)
            b_p, k_p = b_t[:, cols], k_t[:, cols]
            rhs = jnp.concatenate([jnp.where(hmask[0], b_p, 0.0), jnp.where(hmask[1], b_p, 0.0),
                                   jnp.where(hmask[0], k_p, 0.0), jnp.where(hmask[1], k_p, 0.0)], axis=0)
            scs.append(_dot_nt(lhs, rhs.astype(BF16)))
        aaks = []
        for p in range(npair):
            sc = scs[p]
            top_b, top_k = sc[:lc, :LANES], sc[:lc, LANES:]
            bot_b, bot_k = sc[lc:, :LANES], sc[lc:, LANES:]
            aab_scr[p] = jnp.where(strict2, top_b, 0.0)
            aoff_scr[p] = jnp.where(below2, top_b, 0.0).astype(BF16)
            arkb_scr[p] = jnp.concatenate([jnp.where(incl2, bot_k, 0.0), jnp.where(incl2, bot_b, 0.0)],
                                          axis=1).astype(BF16)
            a_ak = jnp.where(strict2, top_k, 0.0)
            aaks.append(jnp.concatenate([jnp.where(hmask[0], a_ak, 0.0), jnp.where(hmask[1], a_ak, 0.0)],
                                        axis=0).astype(BF16))
        for p in range(npair):
            cols = slice(p * LANES, (p + 1) * LANES)
            av = _dot(aaks[p], vp2s[p])
            for e in range(2):
                av_e = jnp.where(hmask[e], av[e * lc:(e + 1) * lc], 0.0)
                x_scr[2 * p + e] = jnp.where(hmask[e], a_t[:, cols], 0.0) + pltpu.roll(av_e, dh, axis=1)

        for bi in range(nb):
            brows = slice(bi * R_SUB, (bi + 1) * R_SUB)
            a_blk = aab_scr[:, brows, :].reshape(npair * R_SUB, LANES)
            a_hi = a_blk.astype(BF16)
            a_lo = (a_blk - a_hi.astype(F32)).astype(BF16)
            coef_scr[...] = _dot(jnp.concatenate([a_hi, a_lo], axis=1), sel_ref[bi])
            xs = [x_scr[h, brows, :] for h in range(nheads)]
            if bi > 0:
                for p in range(npair):
                    slab = aoff_scr[p, brows, :]
                    lhs = jnp.concatenate([jnp.where(hmask[0], slab, jnp.zeros_like(slab)),
                                           jnp.where(hmask[1], slab, jnp.zeros_like(slab))], axis=0)
                    xst = x_scr[2 * p:2 * p + 2].reshape(2 * lc, LANES).astype(BF16)
                    corr = _dot(lhs, xst)
                    for e in range(2):
                        xs[2 * p + e] = xs[2 * p + e] - corr[e * R_SUB:(e + 1) * R_SUB]
            x_lo = [x[:R_HALF] for x in xs]
            x_hi = [x[R_HALF:] for x in xs]
            for s in range(R_HALF - 1):
                for h in range(nheads):
                    p, e = h // 2, h % 2
                    tile = e * (R_HALF - 1) + s
                    coef = coef_scr[p * R_SUB:(p + 1) * R_SUB, tile * LANES:(tile + 1) * LANES]
                    row = x_lo[h][s:s + 1, :]
                    x_lo[h] = x_lo[h] - coef[:R_HALF] * row
                    x_hi[h] = x_hi[h] - coef[R_HALF:] * row
            for s in range(R_HALF - 1, R_SUB - 1):
                for h in range(nheads):
                    p, e = h // 2, h % 2
                    col = e * lc + bi * R_SUB + s
                    coef = aab_scr[p, bi * R_SUB + R_HALF:(bi + 1) * R_SUB, col:col + 1]
                    row = x_lo[h][s:s + 1, :] if s < R_HALF else x_hi[h][s - R_HALF:s - R_HALF + 1, :]
                    x_hi[h] = x_hi[h] - coef * row
            for h in range(nheads):
                x_scr[h, bi * R_SUB:bi * R_SUB + R_HALF, :] = x_lo[h]
                x_scr[h, bi * R_SUB + R_HALF:(bi + 1) * R_SUB, :] = x_hi[h]

        hts, ubs = [], []
        for p in range(npair):
            x0, x1 = x_scr[2 * p], x_scr[2 * p + 1]
            w_pair = jnp.where(hmask[0], x0, x1)
            uv_pair = pltpu.roll(jnp.where(hmask[0], x1, x0), dh, axis=1)
            ht = ht_scr[p]
            hts.append(ht)
            ubs.append((_dot_nt(w_pair.astype(BF16), ht.astype(BF16)) + uv_pair).astype(BF16))
        for p in range(npair):
            cols = slice(p * LANES, (p + 1) * LANES)
            arkb = arkb_scr[p]
            lhs = jnp.concatenate([jnp.where(hmask2[e], arkb, jnp.zeros_like(arkb)) for e in range(2)], axis=0)
            rhs = jnp.concatenate([vp2s[p], -ubs[p], -ubs[p]], axis=0)
            intra = _dot(lhs, rhs)
            y = _dot_nt(r_t[:, cols].astype(BF16), hts[p].astype(BF16))
            ybuf[rows, cols] = y + jnp.where(hmask[0], intra[:lc], intra[lc:])
        for p in range(npair):
            cols = slice(p * LANES, (p + 1) * LANES)
            vu = jnp.concatenate([vps[p], -ubs[p]], axis=0)
            kb = jnp.concatenate([k_h[:, cols], b_h[:, cols]], axis=0).astype(BF16)
            upd = lax.dot_general(vu, kb, (((0,), (0,)), ((), ())), preferred_element_type=F32)
            ht_scr[p] = hts[p] * gam[:, cols] + jnp.where(bdiag, upd, 0.0)
        return carry

    lax.fori_loop(0, lt // lc, chunk, 0)

    rr2 = lax.broadcasted_iota(jnp.int32, (2 * LANES, LANES), 0)
    cc2 = lax.broadcasted_iota(jnp.int32, (2 * LANES, LANES), 1)
    ones_bd = (((rr2 % LANES) < dh) == (cc2 < dh)).astype(BF16)
    for p in range(npair):
        cols = slice(p * LANES, (p + 1) * LANES)
        y = ybuf[:, cols]
        r, k, v = r_ref[:, cols], k_ref[:, cols], v_ref[:, cols]
        bonus = _head_sum(r * k * rk_ref[:, cols], lane) * v
        yc = y - _head_sum_mxu(y, ones_bd) * (1.0 / dh)
        yn = yc * lax.rsqrt(_head_sum_mxu(yc * yc, ones_bd) * (1.0 / dh) + R_LN_EPS)
        ybuf[:, cols] = (yn * lng_ref[:, cols] + lnb_ref[:, cols] + bonus) * g_ref[:, cols]
    out_ref[...] = h_ref[...] + _dot(ybuf[...].astype(BF16), wout_ref[...])


def _rwkv_layer(h2d, bsz, seq, norm_g, mu, w_in, w0, w_up, a0, a_up, g_up, k_k, k_a, r_k, ln_g, ln_b, w_out,
                *, tm=256, lt=256, lc=64):
    t, d = h2d.shape
    c0 = 3 * d
    c1 = c0 + w_up.shape[0]
    c2 = c1 + a_up.shape[0]
    pad_c = lambda w, n: jnp.pad(w, ((0, 0), (0, n - w.shape[1]))).astype(BF16)
    pad_r = lambda w, n: jnp.pad(w, ((0, n - w.shape[0]), (0, 0))).astype(BF16)
    up = lambda n: -(-n // LANES) * LANES
    nw, na, ng = up(c1 - c0), up(c2 - c1), up(w_in.shape[1] - c2)
    wr, wk, wv = (w_in[:, j * d:(j + 1) * d].astype(BF16) for j in range(3))
    wd, wa, wg = pad_c(w_in[:, c0:c1], nw), pad_c(w_in[:, c1:c2], na), pad_c(w_in[:, c2:], ng)
    row = lambda a: a.reshape(1, d)
    tile = pl.BlockSpec((tm, d), lambda i: (i, 0))
    outs = pl.pallas_call(
        functools.partial(_rw_proj_kernel, tm=tm, seq=seq),
        grid=(t // tm,),
        in_specs=[tile, _const_spec((1, d)), _const_spec((6, d)), _const_spec((d, d)), _const_spec((d, d)),
                  _const_spec((d, d)), _const_spec((d, nw)), _const_spec((d, na)), _const_spec((d, ng)),
                  _const_spec((nw, d)), _const_spec((na, d)), _const_spec((ng, d)),
                  _const_spec((1, d)), _const_spec((1, d)), _const_spec((1, d)), _const_spec((1, d))],
        out_specs=[tile] * 7,
        out_shape=[jax.ShapeDtypeStruct((t, d), F32)] * 7,
        scratch_shapes=[pltpu.VMEM((tm + 8, d), F32)],
        compiler_params=_cparams("arbitrary"),
        name="rw_proj",
    )(h2d, row(norm_g), mu, wr, wk, wv, wd, wa, wg, pad_r(w_up, nw), pad_r(a_up, na), pad_r(g_up, ng),
      row(w0), row(a0), row(k_k), row(k_a))
    r, lw, k, v, al, be, g = outs

    nt = seq // lt
    assert lc == R_HEAD_DIM, "pair tiles put the two heads' (lc, lc) matrices side by side on 128 lanes"
    nb, nmx = lc // R_SUB, R_HALF - 1
    nsel = 2 * nmx * LANES
    klane = jnp.arange(2 * LANES)[None, :, None] % LANES
    otile = jnp.arange(nsel)[None, None, :] // LANES
    sel = ((klane // lc == otile // nmx)
           & (klane % lc == jnp.arange(nb)[:, None, None] * R_SUB + otile % nmx)).astype(BF16)
    spec = pl.BlockSpec((lt, d), lambda b, i: (b * nt + i, 0))
    return pl.pallas_call(
        functools.partial(_rw_kernel, lt=lt, lc=lc),
        grid=(bsz, nt),
        in_specs=[spec] * 8 + [_const_spec((1, d))] * 3 + [_const_spec((d, d)), _const_spec(sel.shape)],
        out_specs=spec,
        out_shape=jax.ShapeDtypeStruct((t, d), F32),
        scratch_shapes=[pltpu.VMEM((d // LANES, LANES, LANES), F32), pltpu.VMEM((lt, d), F32),
                        pltpu.VMEM((d // LANES, lc, LANES), F32), pltpu.VMEM((d // LANES, lc, LANES), BF16),
                        pltpu.VMEM((d // LANES, lc, 2 * LANES), BF16),
                        pltpu.VMEM((R_HEADS, lc, LANES), F32), pltpu.VMEM((d // LANES * R_SUB, nsel), F32)],
        compiler_params=_cparams("parallel", "arbitrary"),
        name="rw_mix",
    )(r, lw, k, v, al, be, g, h2d, r_k.reshape(1, d), row(ln_g), row(ln_b), w_out.astype(BF16), sel)


def kernel(x, mlp_norm_g, mlp_w_up, mlp_w_down, final_norm_g, s5_norm_g, s5_a_re, s5_a_im, s5_log_dt, s5_b_re, s5_b_im, s5_c_re, s5_c_im, s5_d, s5_w_glu, s5_b_glu, ml_norm_g, ml_w_in, ml_conv_w, ml_conv_b, ml_b_i, ml_b_f, ml_head_g, ml_w_out, fox_norm_g, fox_w_in, fox_b_f, fox_w_out, rw_norm_g, rw_mu, rw_w_in, rw_w0, rw_w_up, rw_a0, rw_a_up, rw_g_up, rw_k_k, rw_k_a, rw_r_k, rw_ln_g, rw_ln_b, rw_w_out):
    bsz, seq, d = x.shape
    depth = mlp_w_up.shape[0]
    h = x.reshape(bsz * seq, d)
    for i in range(depth):
        kind, j = i % 4, i // 4
        if kind == 0:
            h = _s5_layer(h, bsz, seq, s5_norm_g[j], s5_a_re[j], s5_a_im[j], s5_log_dt[j], s5_b_re[j], s5_b_im[j],
                          s5_c_re[j], s5_c_im[j], s5_d[j], s5_w_glu[j], s5_b_glu[j])
        elif kind == 1:
            h = _mlstm_layer(h, bsz, seq, ml_norm_g[j], ml_w_in[j], ml_conv_w[j], ml_conv_b[j], ml_b_i[j],
                             ml_b_f[j], ml_head_g[j], ml_w_out[j])
        elif kind == 2:
            h = _fox_layer(h, bsz, seq, fox_norm_g[j], fox_w_in[j], fox_b_f[j], fox_w_out[j])
        else:
            h = _rwkv_layer(h, bsz, seq, rw_norm_g[j], rw_mu[j], rw_w_in[j], rw_w0[j], rw_w_up[j], rw_a0[j],
                            rw_a_up[j], rw_g_up[j], rw_k_k[j], rw_k_a[j], rw_r_k[j], rw_ln_g[j], rw_ln_b[j],
                            rw_w_out[j])
        h = _mlp_layer(h, mlp_norm_g[i], mlp_w_up[i], mlp_w_down[i],
                       final_g=final_norm_g if i == depth - 1 else None)
    return h.reshape(bsz, seq, d)
```

```python
import functools
import math

import jax
import jax.numpy as jnp
from jax import lax
from jax.experimental import pallas as pl
from jax.experimental.pallas import tpu as pltpu

F32 = jnp.float32
BF16 = jnp.bfloat16

NORM_EPS = 1e-6
NEG_BIG = -1e30
LANES = 128
VMEM_LIMIT = 56 * 1024 * 1024

M_HEADS = 8
M_QK_DIM = 64
M_V_DIM = 128
M_CONV = 4
M_NORM_EPS = 1e-6


def _cparams(*sem):
    return pltpu.CompilerParams(dimension_semantics=sem, vmem_limit_bytes=VMEM_LIMIT)


def _rms(x, g):
    return x * lax.rsqrt(jnp.mean(x * x, axis=-1, keepdims=True) + NORM_EPS) * g


def _dot(a, b):
    return jnp.dot(a, b, preferred_element_type=F32)


def _dot_nt(a, b):
    return lax.dot_general(a, b, (((1,), (1,)), ((), ())), preferred_element_type=F32)


def _split3(x):
    hi = x.astype(BF16)
    r = x - hi.astype(F32)
    mid = r.astype(BF16)
    lo = (r - mid.astype(F32)).astype(BF16)
    return hi, mid, lo


def _dot_exact_l(ones_mat, x):
    hi, mid, lo = _split3(x)
    return _dot(ones_mat, hi) + _dot(ones_mat, mid) + _dot(ones_mat, lo)


def _dot_exact_r(x, ones_mat):
    hi, mid, lo = _split3(x)
    return _dot(hi, ones_mat) + _dot(mid, ones_mat) + _dot(lo, ones_mat)


def _log_sigmoid(x):
    return jnp.minimum(x, 0.0) - jnp.log(1.0 + jnp.exp(-jnp.abs(x)))


def _sigmoid(x):
    return 1.0 / (1.0 + jnp.exp(-x))


def _const_spec(shape):
    nd = len(shape)
    return pl.BlockSpec(shape, lambda *_: (0,) * nd)


def _mlp_kernel(h_ref, g_ref, wup_ref, wdn_ref, fg_ref, out_ref, *, fc, final):
    x = h_ref[...]
    xn = _rms(x, g_ref[...]).astype(BF16)
    acc = x
    for c in range(wup_ref.shape[1] // fc):
        hid = jnp.maximum(_dot(xn, wup_ref[:, c * fc:(c + 1) * fc]), 0.0)
        acc = acc + _dot((hid * hid).astype(BF16), wdn_ref[c * fc:(c + 1) * fc, :])
    out_ref[...] = _rms(acc, fg_ref[...]) if final else acc


def _mlp_layer(h2d, norm_g, w_up, w_down, final_g=None, *, tm=512, fc=512):
    t, d = h2d.shape
    ff = w_up.shape[1]
    fg = (norm_g if final_g is None else final_g).reshape(1, d)
    return pl.pallas_call(
        functools.partial(_mlp_kernel, fc=fc, final=final_g is not None),
        grid=(t // tm,),
        in_specs=[pl.BlockSpec((tm, d), lambda i: (i, 0)), _const_spec((1, d)),
                  _const_spec((d, ff)), _const_spec((ff, d)), _const_spec((1, d))],
        out_specs=pl.BlockSpec((tm, d), lambda i: (i, 0)),
        out_shape=jax.ShapeDtypeStruct((t, d), F32),
        compiler_params=_cparams("parallel"),
        name="mlp",
    )(h2d, norm_g.reshape(1, d), w_up.astype(BF16), w_down.astype(BF16), fg)


def _ml_proj_kernel(h_ref, g_ref, wqk_ref, wv_ref, wo_ref, wg_ref, wgt_ref,
                    qk_ref, v_ref, o_ref, gc_ref, gr_ref):
    xn = _rms(h_ref[...], g_ref[...]).astype(BF16)
    qk_ref[...] = _dot(xn, wqk_ref[...])
    v_ref[...] = _dot(xn, wv_ref[...]).astype(v_ref.dtype)
    o_ref[...] = _dot(xn, wo_ref[...])
    gc_ref[...] = _dot(xn, wg_ref[...])
    gr_ref[...] = _dot_nt(wgt_ref[...], xn)


def _ml_kernel(qk_ref, v_ref, o_ref, gc_ref, gr_ref, h_ref, cw_ref, cb_ref, bcol_ref, brow_ref,
               hg_ref, wout_ref, rep_ref, out_ref, ext_scr, c_scr, m_scr, hbuf, *, lt, lc):
    nh, dk, dv = M_HEADS, M_QK_DIM, M_V_DIM
    hd = nh * dk

    @pl.when(pl.program_id(1) == 0)
    def _():
        c_scr[...] = jnp.zeros_like(c_scr)
        m_scr[...] = jnp.zeros_like(m_scr)
        ext_scr[0:8, :] = jnp.zeros((8, 2 * hd), F32)

    ext_scr[8:8 + lt, :] = qk_ref[...]
    conv = cb_ref[...] + jnp.zeros((lt, 2 * hd), F32)
    for kk in range(M_CONV):
        conv = conv + cw_ref[kk:kk + 1, :] * ext_scr[8 - (M_CONV - 1) + kk:8 - (M_CONV - 1) + kk + lt, :]
    ext_scr[0:8, :] = ext_scr[lt:lt + 8, :]
    act = conv * _sigmoid(conv)

    lane = lax.broadcasted_iota(jnp.int32, (1, LANES), 1)
    r_i = lax.broadcasted_iota(jnp.int32, (lc, lc), 0)
    c_i = lax.broadcasted_iota(jnp.int32, (lc, lc), 1)
    causal = c_i <= r_i
    tri_l = causal.astype(BF16)
    tri_u = (r_i <= c_i).astype(BF16)
    ones_blk = jnp.ones((lc, LANES), BF16)
    row16 = lax.broadcasted_iota(jnp.int32, (2 * nh, 1), 0)

    def both(x):
        return jnp.concatenate([x, x], axis=1)

    local = [[None] * nh for _ in range(lt // lc)]
    for c in range(lt // lc):
        rows = slice(c * lc, (c + 1) * lc)
        gcb = gc_ref[rows, :] + bcol_ref[...]
        gate_c = jnp.where(lane < nh, gcb, jnp.where(lane < 2 * nh, _log_sigmoid(gcb), 0.0))
        cum_c = _dot_exact_l(tri_l, jnp.where(lane < nh, 0.0, gate_c))
        hi, mid, lo = _split3(jnp.where(lane < nh, gate_c, cum_c))
        rep = _dot(jnp.concatenate([hi, mid, lo], axis=1), rep_ref[...])
        grb = gr_ref[:, rows] + brow_ref[...]
        gate_r = jnp.where(row16 < nh, grb, _log_sigmoid(grb))
        cum_r = _dot_exact_r(jnp.where(row16 < nh, 0.0, gate_r), tri_u)

        for h in range(nh):
            p, e = h // 2, h % 2
            head_mask = (lane >= e * dk) & (lane < (e + 1) * dk)
            q_m = (jnp.where(head_mask, act[rows, p * LANES:(p + 1) * LANES], 0.0) * (dk ** -0.5)).astype(BF16)
            k_m = jnp.where(head_mask, act[rows, hd + p * LANES:hd + (p + 1) * LANES], 0.0)
            v_aug = jnp.concatenate([v_ref[rows, h * dv:(h + 1) * dv].astype(BF16), ones_blk], axis=1)
            li_c = rep[:, h * LANES:(h + 1) * LANES]
            b_c = rep[:, (nh + h) * LANES:(nh + h + 1) * LANES]
            b_row = cum_r[nh + h:nh + h + 1, :]
            li_row = gate_r[h:h + 1, :]
            d_log = jnp.where(causal, b_c[:, :lc] - b_row + li_row, NEG_BIG)
            m_loc = jnp.max(d_log, axis=-1, keepdims=True) + jnp.zeros((lc, LANES), F32)
            s_loc = _dot_nt(q_m, k_m.astype(BF16)) * jnp.exp(d_log - m_loc[:, :lc])
            r_loc = _dot(s_loc.astype(BF16), v_aug)
            b_last = b_c[lc - 1:lc, :]
            m_loc_last = m_loc[lc - 1:lc, :]
            w_state = jnp.exp(b_last - b_c + li_c - m_loc_last)
            kv_loc = _dot(jnp.transpose(k_m * w_state).astype(BF16), v_aug)
            local[c][h] = (q_m, b_c, m_loc, r_loc, kv_loc, b_last, m_loc_last)

    for c in range(lt // lc):
        rows = slice(c * lc, (c + 1) * lc)
        for h in range(nh):
            q_m, b_c, m_loc, r_loc, kv_loc, b_last, m_loc_last = local[c][h]
            m_prev = m_scr[h:h + 1, :]
            g = b_c + m_prev
            m_t = jnp.maximum(g, m_loc)
            c_aug = c_scr[h]
            r = both(jnp.exp(g - m_t)) * _dot(q_m, c_aug.astype(BF16)) + both(jnp.exp(m_loc - m_t)) * r_loc
            hh = r[:, :dv] / jnp.maximum(jnp.abs(r[:, dv:]), jnp.exp(-m_t))

            m_new = m_t[lc - 1:lc, :]
            decay = jnp.exp(b_last + m_prev - m_new)
            c_scr[h] = both(decay) * c_aug + both(jnp.exp(m_loc_last - m_new)) * kv_loc
            m_scr[h:h + 1, :] = m_new

            yc = hh - jnp.mean(hh, axis=-1, keepdims=True)
            yn = yc * lax.rsqrt(jnp.mean(yc * yc, axis=-1, keepdims=True) + M_NORM_EPS)
            cols = slice(h * dv, (h + 1) * dv)
            hbuf[rows, cols] = yn * hg_ref[:, cols] * _sigmoid(o_ref[rows, cols])

    out_ref[...] = h_ref[...] + _dot(hbuf[...].astype(BF16), wout_ref[...])


def _mlstm_layer(h2d, bsz, seq, norm_g, w_in, conv_w, conv_b, b_i, b_f, head_g, w_out, *, tm=512, lt=256, lc=128):
    t, d = h2d.shape
    nh, dk, dv = M_HEADS, M_QK_DIM, M_V_DIM
    o0 = 2 * nh * dk
    o1 = o0 + nh * dv
    o3 = o1 + 2 * nh
    wqk = w_in[:, :o0].astype(BF16)
    wv = w_in[:, o0:o1].astype(BF16)
    wo = w_in[:, o3:].astype(BF16)
    wg = jnp.pad(w_in[:, o1:o3], ((0, 0), (0, LANES - 2 * nh))).astype(BF16)
    wgt = jnp.transpose(w_in[:, o1:o3]).astype(BF16)
    g2 = norm_g.reshape(1, d)

    qk, v, o, gc, gr = pl.pallas_call(
        _ml_proj_kernel,
        grid=(t // tm,),
        in_specs=[pl.BlockSpec((tm, d), lambda i: (i, 0)), _const_spec((1, d)),
                  _const_spec(wqk.shape), _const_spec(wv.shape), _const_spec(wo.shape),
                  _const_spec(wg.shape), _const_spec(wgt.shape)],
        out_specs=[pl.BlockSpec((tm, o0), lambda i: (i, 0)), pl.BlockSpec((tm, nh * dv), lambda i: (i, 0)),
                   pl.BlockSpec((tm, nh * dv), lambda i: (i, 0)), pl.BlockSpec((tm, LANES), lambda i: (i, 0)),
                   pl.BlockSpec((2 * nh, tm), lambda i: (0, i))],
        out_shape=[jax.ShapeDtypeStruct((t, o0), F32), jax.ShapeDtypeStruct((t, nh * dv), BF16),
                   jax.ShapeDtypeStruct((t, nh * dv), F32), jax.ShapeDtypeStruct((t, LANES), F32),
                   jax.ShapeDtypeStruct((2 * nh, t), F32)],
        compiler_params=_cparams("parallel"),
        name="ml_proj",
    )(h2d, g2, wqk, wv, wo, wg, wgt)

    bcol = jnp.pad(jnp.concatenate([b_i, b_f]), (0, LANES - 2 * nh)).reshape(1, LANES)
    brow = jnp.concatenate([b_i, b_f]).reshape(2 * nh, 1)
    nt = seq // lt
    rep = (jnp.arange(3 * LANES)[:, None] % LANES == jnp.arange(2 * nh * LANES)[None, :] // LANES).astype(BF16)
    row_spec = lambda w: pl.BlockSpec((lt, w), lambda b, i: (b * nt + i, 0))
    return pl.pallas_call(
        functools.partial(_ml_kernel, lt=lt, lc=lc),
        grid=(bsz, nt),
        in_specs=[row_spec(o0), row_spec(nh * dv), row_spec(nh * dv), row_spec(LANES),
                  pl.BlockSpec((2 * nh, lt), lambda b, i: (0, b * nt + i)), row_spec(d),
                  _const_spec((M_CONV, o0)), _const_spec((1, o0)), _const_spec((1, LANES)),
                  _const_spec((2 * nh, 1)), _const_spec((1, nh * dv)), _const_spec((nh * dv, d)),
                  _const_spec(rep.shape)],
        out_specs=row_spec(d),
        out_shape=jax.ShapeDtypeStruct((t, d), F32),
        scratch_shapes=[pltpu.VMEM((lt + 8, o0), F32), pltpu.VMEM((nh, LANES, 2 * LANES), F32),
                        pltpu.VMEM((nh, LANES), F32), pltpu.VMEM((lt, nh * dv), F32)],
        compiler_params=_cparams("parallel", "arbitrary"),
        name="ml_mix",
    )(qk, v, o, gc, gr, h2d, conv_w, conv_b.reshape(1, o0), bcol, brow,
      head_g.reshape(1, nh * dv), w_out.astype(BF16), rep)


S5_GROUP_CH = 16
S5_STATE = 64
S5_CHUNK_GROUPS = 4
S5_CHUNK = S5_CHUNK_GROUPS * S5_STATE


def _s5_disc_kernel(ar_ref, ai_ref, ldt_ref, arx_ref, aix_ref, br_ref, bi_ref,
                    abr_ref, abi_ref, bbr_ref, bbi_ref):
    dt = jnp.exp(ldt_ref[...])

    def abar(ar, ai):
        mag = jnp.exp(dt * ar)
        return mag * jnp.cos(dt * ai), mag * jnp.sin(dt * ai)

    abr_ref[...], abi_ref[...] = abar(ar_ref[...], ai_ref[...])
    ar, ai = arx_ref[...], aix_ref[...]
    xr, xi = abar(ar, ai)
    inv = 1.0 / (ar * ar + ai * ai)
    zr, zi = xr - 1.0, xi
    coef_re = (zr * ar + zi * ai) * inv
    coef_im = (zi * ar - zr * ai) * inv
    br, bi = br_ref[...], bi_ref[...]
    bbr_ref[...] = coef_re * br - coef_im * bi
    bbi_ref[...] = coef_re * bi + coef_im * br


def _s5_kernel(h_ref, g_ref, abr_ref, abi_ref, bre_ref, bim_ref, cre_ref, cim_ref, dsk_ref, wglu_ref, bglu_ref,
               out_ref, sr_scr, si_scr, *, tt, bsz):
    d = h_ref.shape[1]
    nchunk = sr_scr.shape[0]
    per_slab = nchunk // (d // S5_CHUNK)

    @pl.when(pl.program_id(0) == 0)
    def _():
        sr_scr[...] = jnp.zeros_like(sr_scr)
        si_scr[...] = jnp.zeros_like(si_scr)

    x = h_ref[...]
    u = _rms(x, g_ref[...])
    ub = u.astype(BF16)
    ys = [jnp.zeros((tt * bsz, S5_CHUNK), F32) for _ in range(d // S5_CHUNK)]
    for c in range(nchunk):
        slab = c // per_slab
        us = ub[:, slab * S5_CHUNK:(slab + 1) * S5_CHUNK]
        bu_re = _dot(us, bre_ref[c])
        bu_im = _dot(us, bim_ref[c])
        ar, ai = abr_ref[c], abi_ref[c]
        sr, si = sr_scr[c], si_scr[c]
        s_re, s_im = [], []
        for t in range(tt):
            rows = slice(t * bsz, (t + 1) * bsz)
            sr, si = ar * sr - ai * si + bu_re[rows], ar * si + ai * sr + bu_im[rows]
            s_re.append(sr)
            s_im.append(si)
        sr_scr[c] = sr
        si_scr[c] = si
        ys[slab] = (ys[slab] + _dot(jnp.concatenate(s_re, axis=0).astype(BF16), cre_ref[c])
                    - _dot(jnp.concatenate(s_im, axis=0).astype(BF16), cim_ref[c]))
    y = jnp.concatenate(ys, axis=1) + dsk_ref[...] * u
    gelu = 0.5 * y * (1.0 + jnp.tanh(math.sqrt(2.0 / math.pi) * (y + 0.044715 * (y * y * y))))
    z = _dot(gelu.astype(BF16), wglu_ref[...]) + bglu_ref[...]
    out_ref[...] = x + z[:, :d] * _sigmoid(z[:, d:])


def _s5_layer(h2d, bsz, seq, norm_g, a_re, a_im, log_dt, b_re, b_im, c_re, c_im, d_skip, w_glu, b_glu, *, tt=8):
    t, d = h2d.shape
    ng, ns, hg = a_re.shape[0], S5_STATE, S5_GROUP_CH
    cg = S5_CHUNK_GROUPS
    nchunk = ng // cg
    per_slab = S5_CHUNK // (cg * hg)

    rep = lambda a: jnp.repeat(a, hg, axis=1)
    abr, abi, bbr, bbi = pl.pallas_call(
        _s5_disc_kernel,
        out_shape=[jax.ShapeDtypeStruct((ng, ns), F32)] * 2 + [jax.ShapeDtypeStruct((ng, ns * hg), F32)] * 2,
        name="s5_disc",
    )(a_re, a_im, log_dt.reshape(ng, 1), rep(a_re), rep(a_im), b_re.reshape(ng, ns * hg), b_im.reshape(ng, ns * hg))

    eye = jnp.eye(cg, dtype=F32)
    place = jax.nn.one_hot(jnp.arange(nchunk) % per_slab, per_slab, dtype=F32)

    def b_blocks(bb):
        m = bb.reshape(nchunk, cg, ns, hg)
        blk = jnp.einsum('cgph,gq->cghqp', m, eye).reshape(nchunk, cg * hg, cg * ns)
        return jnp.einsum('crs,cj->cjrs', blk, place).reshape(nchunk, per_slab * cg * hg, cg * ns).astype(BF16)

    def c_blocks(cc):
        m = cc.reshape(nchunk, cg, hg, ns)
        blk = jnp.einsum('cghp,gq->cgpqh', m, eye).reshape(nchunk, cg * ns, cg * hg)
        return jnp.einsum('csr,cj->csjr', blk, place).reshape(nchunk, cg * ns, per_slab * cg * hg).astype(BF16)

    rows = tt * bsz
    ht = h2d.reshape(bsz, seq, d).transpose(1, 0, 2).reshape(t, d)
    out = pl.pallas_call(
        functools.partial(_s5_kernel, tt=tt, bsz=bsz),
        grid=(seq // tt,),
        in_specs=[pl.BlockSpec((rows, d), lambda i: (i, 0)), _const_spec((1, d)),
                  _const_spec((nchunk, 1, S5_CHUNK)), _const_spec((nchunk, 1, S5_CHUNK)),
                  _const_spec((nchunk, S5_CHUNK, S5_CHUNK)), _const_spec((nchunk, S5_CHUNK, S5_CHUNK)),
                  _const_spec((nchunk, S5_CHUNK, S5_CHUNK)), _const_spec((nchunk, S5_CHUNK, S5_CHUNK)),
                  _const_spec((1, d)), _const_spec((d, 2 * d)), _const_spec((1, 2 * d))],
        out_specs=pl.BlockSpec((rows, d), lambda i: (i, 0)),
        out_shape=jax.ShapeDtypeStruct((t, d), F32),
        scratch_shapes=[pltpu.VMEM((nchunk, bsz, S5_CHUNK), F32), pltpu.VMEM((nchunk, bsz, S5_CHUNK), F32)],
        compiler_params=_cparams("arbitrary"),
        name="s5_mix",
    )(ht, norm_g.reshape(1, d), abr.reshape(nchunk, 1, S5_CHUNK), abi.reshape(nchunk, 1, S5_CHUNK),
      b_blocks(bbr), b_blocks(bbi), c_blocks(c_re), c_blocks(c_im),
      d_skip.reshape(1, d), w_glu.astype(BF16), b_glu.reshape(1, 2 * d))
    return out.reshape(seq, bsz, d).transpose(1, 0, 2).reshape(t, d)


F_HEADS = 16
F_HEAD_DIM = 64
F_QROWS = 256


def _fox_proj_kernel(h_ref, g_ref, wq_ref, wk_ref, wv_ref, wo_ref, wgt_ref,
                     q_ref, k_ref, v_ref, o_ref, gr_ref):
    xn = _rms(h_ref[...], g_ref[...]).astype(BF16)
    q_ref[...] = (_dot(xn, wq_ref[...]) * (F_HEAD_DIM ** -0.5)).astype(BF16)
    k_ref[...] = _dot(xn, wk_ref[...]).astype(BF16)
    v_ref[...] = _dot(xn, wv_ref[...]).astype(BF16)
    o_ref[...] = _dot(xn, wo_ref[...])
    gr_ref[...] = _dot_nt(wgt_ref[...], xn)


def _fox_kernel(q_ref, k_ref, v_ref, o_ref, gr_ref, h_ref, brow_ref, wout_ref,
                out_ref, cr_scr, obuf, m_scr, l_scr, acc_scr, *, tq, tk, seq):
    nh, dh = F_HEADS, F_HEAD_DIM
    i = pl.program_id(1)
    lane = lax.broadcasted_iota(jnp.int32, (1, LANES), 1)
    hmask = [lane < dh, lane >= dh]
    th = F_QROWS

    @pl.when(i == 0)
    def _():
        r_i = lax.broadcasted_iota(jnp.int32, (LANES, LANES), 0)
        c_i = lax.broadcasted_iota(jnp.int32, (LANES, LANES), 1)
        tri_u = (r_i <= c_i).astype(BF16)
        carry_r = jnp.zeros((nh, 1), F32)
        per = tk // LANES
        for blk in range(seq // LANES):
            rows = slice(blk * LANES, (blk + 1) * LANES)
            cr = _dot_exact_r(_log_sigmoid(gr_ref[:, rows] + brow_ref[...]), tri_u) + carry_r
            cr_scr[blk // per, :, (blk % per) * LANES:(blk % per + 1) * LANES] = cr
            carry_r = cr[:, LANES - 1:LANES]

    r_i = lax.broadcasted_iota(jnp.int32, (th, tk), 0)
    c_i = lax.broadcasted_iota(jnp.int32, (th, tk), 1)

    m_scr[...] = jnp.full(m_scr.shape, NEG_BIG, F32)
    l_scr[...] = jnp.zeros_like(l_scr)
    acc_scr[...] = jnp.zeros_like(acc_scr)

    def step(j, masked):
        k0 = pl.multiple_of(j * tk, tk)
        crow = cr_scr[j]
        for p in range(nh // 2):
            cols = slice(p * LANES, (p + 1) * LANES)
            kp = k_ref[pl.ds(k0, tk), cols]
            vp = v_ref[pl.ds(k0, tk), cols]
            for half in range(tq // th):
                qrows = slice(half * th, (half + 1) * th)
                qp = q_ref[qrows, cols]
                acc = acc_scr[p, qrows, :]
                scaled = []
                for e in range(2):
                    h = 2 * p + e
                    q_m = jnp.where(hmask[e], qp, jnp.zeros_like(qp))
                    s = _dot_nt(q_m, kp) - crow[h:h + 1, :]
                    if masked:
                        s = jnp.where(c_i <= r_i + half * th, s, NEG_BIG)
                    m_prev = m_scr[h, qrows, :]
                    m_next = jnp.maximum(m_prev, jnp.max(s, axis=-1, keepdims=True))
                    alpha = jnp.exp(m_prev - m_next)
                    pm = jnp.exp(s - jnp.concatenate([m_next] * (tk // LANES), axis=1))
                    l_scr[h, qrows, :] = alpha * l_scr[h, qrows, :] + jnp.sum(pm, axis=-1, keepdims=True)
                    m_scr[h, qrows, :] = m_next
                    scaled.append(acc * alpha + _dot(pm.astype(BF16), vp))
                acc_scr[p, qrows, :] = jnp.where(hmask[0], scaled[0], scaled[1])

    def body(j, carry):
        step(j, False)
        return carry

    lax.fori_loop(0, i, body, 0)
    step(i, True)
    for p in range(nh // 2):
        cols = slice(p * LANES, (p + 1) * LANES)
        denom = jnp.where(hmask[0], l_scr[2 * p], l_scr[2 * p + 1])
        obuf[:, cols] = acc_scr[p] / denom * _sigmoid(o_ref[:, cols])

    out_ref[...] = h_ref[...] + _dot(obuf[...].astype(BF16), wout_ref[...])


def _fox_layer(h2d, bsz, seq, norm_g, w_in, b_f, w_out, *, tm=512, tq=256):
    t, d = h2d.shape
    nh, dh = F_HEADS, F_HEAD_DIM
    hd = nh * dh
    wq, wk, wv = (w_in[:, j * hd:(j + 1) * hd].astype(BF16) for j in range(3))
    wf = w_in[:, 3 * hd:3 * hd + nh]
    wo = w_in[:, 3 * hd + nh:].astype(BF16)
    wgt = jnp.transpose(wf).astype(BF16)
    tile = lambda w: pl.BlockSpec((tm, w), lambda i: (i, 0))
    q, k, v, o, gr = pl.pallas_call(
        _fox_proj_kernel,
        grid=(t // tm,),
        in_specs=[tile(d), _const_spec((1, d)), _const_spec(wq.shape), _const_spec(wk.shape),
                  _const_spec(wv.shape), _const_spec(wo.shape), _const_spec(wgt.shape)],
        out_specs=[tile(hd), tile(hd), tile(hd), tile(hd), pl.BlockSpec((nh, tm), lambda i: (0, i))],
        out_shape=[jax.ShapeDtypeStruct((t, hd), BF16)] * 3 + [jax.ShapeDtypeStruct((t, hd), F32),
                   jax.ShapeDtypeStruct((nh, t), F32)],
        compiler_params=_cparams("parallel"),
        name="fox_proj",
    )(h2d, norm_g.reshape(1, d), wq, wk, wv, wo, wgt)

    nt = seq // tq
    q_spec = lambda w: pl.BlockSpec((tq, w), lambda b, i: (b * nt + i, 0))
    seq_spec = lambda w: pl.BlockSpec((seq, w), lambda b, i: (b, 0))
    return pl.pallas_call(
        functools.partial(_fox_kernel, tq=tq, tk=tq, seq=seq),
        grid=(bsz, nt),
        in_specs=[q_spec(hd), seq_spec(hd), seq_spec(hd), q_spec(hd),
                  pl.BlockSpec((nh, seq), lambda b, i: (0, b)), q_spec(d),
                  _const_spec((nh, 1)), _const_spec((hd, d))],
        out_specs=q_spec(d),
        out_shape=jax.ShapeDtypeStruct((t, d), F32),
        scratch_shapes=[pltpu.VMEM((nt, nh, tq), F32), pltpu.VMEM((tq, hd), F32),
                        pltpu.VMEM((nh, tq, LANES), F32), pltpu.VMEM((nh, tq, LANES), F32),
                        pltpu.VMEM((nh // 2, tq, LANES), F32)],
        compiler_params=_cparams("parallel", "arbitrary"),
        name="fox_attn",
    )(q, k, v, o, gr, h2d, b_f.reshape(nh, 1), w_out.astype(BF16))


R_HEADS = 16
R_HEAD_DIM = 64
R_LN_EPS = 64e-5
R_SUB = 16
R_HALF = R_SUB // 2


def _softplus(z):
    return jnp.maximum(z, 0.0) + jnp.log(1.0 + jnp.exp(-jnp.abs(z)))


def _head_sum(x, lane):
    lo = lane < R_HEAD_DIM
    s0 = jnp.sum(jnp.where(lo, x, 0.0), axis=-1, keepdims=True)
    s1 = jnp.sum(jnp.where(lo, 0.0, x), axis=-1, keepdims=True)
    return jnp.where(lo, s0, s1)


def _rw_proj_kernel(h_ref, g_ref, mu_ref, wr_ref, wk_ref, wv_ref, wd_ref, wa_ref, wg_ref, wup_ref, aup_ref,
                    gup_ref, w0_ref, a0_ref, kk_ref, ka_ref,
                    r_ref, lw_ref, k_ref, v_ref, al_ref, be_ref, g_out_ref, ext_scr, *, tm, seq):
    d = h_ref.shape[1]
    i = pl.program_id(0)
    xn = _rms(h_ref[...], g_ref[...])
    @pl.when((i * tm) % seq == 0)
    def _():
        ext_scr[7:8, :] = jnp.zeros((1, d), F32)

    @pl.when((i * tm) % seq != 0)
    def _():
        ext_scr[7:8, :] = ext_scr[7 + tm:8 + tm, :]

    ext_scr[8:8 + tm, :] = xn
    xb = xn.astype(BF16)
    dxb = (ext_scr[7:7 + tm, :] - xn).astype(BF16)
    mub = mu_ref[...].astype(BF16)
    mix = lambda j: xb + dxb * mub[j:j + 1, :]
    r_ref[...] = _dot(mix(0), wr_ref[...])
    w_in = w0_ref[...] + _dot(jnp.tanh(_dot(mix(1), wd_ref[...])).astype(BF16), wup_ref[...])
    lw_ref[...] = -math.exp(-0.5) * _sigmoid(w_in)
    k = _dot(mix(2), wk_ref[...])
    v_ref[...] = _dot(mix(3), wv_ref[...])
    a = _sigmoid(a0_ref[...] + _dot(_dot(mix(4), wa_ref[...]).astype(BF16), aup_ref[...]))
    g_out_ref[...] = _dot(_sigmoid(_dot(mix(5), wg_ref[...])).astype(BF16), gup_ref[...])
    k_ref[...] = k * (1.0 + (a - 1.0) * ka_ref[...])
    lane = lax.broadcasted_iota(jnp.int32, (1, LANES), 1)
    for p in range(d // LANES):
        cols = slice(p * LANES, (p + 1) * LANES)
        kk = k[:, cols] * kk_ref[:, cols]
        kk = kk * jnp.minimum(lax.rsqrt(_head_sum(kk * kk, lane)), 1e12)
        al_ref[:, cols] = kk
        be_ref[:, cols] = kk * a[:, cols]


def _head_sum_mxu(x, ones_bd):
    hi = x.astype(BF16)
    lo = (x - hi.astype(F32)).astype(BF16)
    return _dot(jnp.concatenate([hi, lo], axis=1), ones_bd)


def _rw_kernel(r_ref, lw_ref, k_ref, v_ref, al_ref, be_ref, g_ref, h_ref, rk_ref, lng_ref, lnb_ref, wout_ref,
               sel_ref, out_ref, ht_scr, ybuf, aab_scr, aoff_scr, arkb_scr, x_scr, coef_scr, *, lt, lc):
    d = h_ref.shape[1]
    dh = R_HEAD_DIM
    npair = d // LANES
    nb = lc // R_SUB

    @pl.when(pl.program_id(1) == 0)
    def _():
        ht_scr[...] = jnp.zeros_like(ht_scr)

    lane = lax.broadcasted_iota(jnp.int32, (1, LANES), 1)
    hmask = [lane < dh, lane >= dh]
    lane2 = lax.broadcasted_iota(jnp.int32, (1, 2 * LANES), 1) % LANES
    hmask2 = [lane2 < dh, lane2 >= dh]
    r_i = lax.broadcasted_iota(jnp.int32, (lc, lc), 0)
    c_i = lax.broadcasted_iota(jnp.int32, (lc, lc), 1)
    tri_l = (c_i <= r_i).astype(BF16)
    r_p2 = lax.broadcasted_iota(jnp.int32, (lc, LANES), 0)
    c_p2 = lax.broadcasted_iota(jnp.int32, (lc, LANES), 1) % lc
    strict2 = c_p2 < r_p2
    incl2 = c_p2 <= r_p2
    below2 = (c_p2 // R_SUB) < (r_p2 // R_SUB)
    rr = lax.broadcasted_iota(jnp.int32, (LANES, LANES), 0)
    cc = lax.broadcasted_iota(jnp.int32, (LANES, LANES), 1)
    bdiag = (rr < dh) == (cc < dh)

    def chunk(c, carry):
        rows = pl.ds(pl.multiple_of(c * lc, lc), lc)
        lw = lw_ref[rows, :]
        cum = _dot_exact_l(tri_l, lw)
        cum_last = cum[lc - 1:lc, :]
        e_pos = jnp.exp(cum)
        e_neg = jnp.exp(-cum)
        e_rem = jnp.exp(cum_last - cum)
        r_t = r_ref[rows, :] * e_pos
        a_t = al_ref[rows, :] * jnp.exp(cum - lw)
        k_all = k_ref[rows, :]
        b_all = be_ref[rows, :]
        k_t, b_t = k_all * e_neg, b_all * e_neg
        k_h, b_h = k_all * e_rem, b_all * e_rem
        gam = jnp.exp(cum_last)

        nheads = 2 * npair
        vps = [v_ref[rows, p * LANES:(p + 1) * LANES].astype(BF16) for p in range(npair)]
        vp2s = [jnp.concatenate([vp, vp], axis=0) for vp in vps]

        scs = []
        for p in range(npair):
            cols = slice(p * LANES, (p + 1) * LANES)
            lhs = jnp.concatenate([a_t[:, cols], r_t[:, cols]], axis=0).astype(BF16)
            b_p, k_p = b_t[:, cols], k_t[:, cols]
            rhs = jnp.concatenate([jnp.where(hmask[0], b_p, 0.0), jnp.where(hmask[1], b_p, 0.0),
                                   jnp.where(hmask[0], k_p, 0.0), jnp.where(hmask[1], k_p, 0.0)], axis=0)
            scs.append(_dot_nt(lhs, rhs.astype(BF16)))
        aaks = []
        for p in range(npair):
            sc = scs[p]
            top_b, top_k = sc[:lc, :LANES], sc[:lc, LANES:]
            bot_b, bot_k = sc[lc:, :LANES], sc[lc:, LANES:]
            aab_scr[p] = jnp.where(strict2, top_b, 0.0)
            aoff_scr[p] = jnp.where(below2, top_b, 0.0).astype(BF16)
            arkb_scr[p] = jnp.concatenate([jnp.where(incl2, bot_k, 0.0), jnp.where(incl2, bot_b, 0.0)],
                                          axis=1).astype(BF16)
            a_ak = jnp.where(strict2, top_k, 0.0)
            aaks.append(jnp.concatenate([jnp.where(hmask[0], a_ak, 0.0), jnp.where(hmask[1], a_ak, 0.0)],
                                        axis=0).astype(BF16))
        for p in range(npair):
            cols = slice(p * LANES, (p + 1) * LANES)
            av = _dot(aaks[p], vp2s[p])
            for e in range(2):
                av_e = jnp.where(hmask[e], av[e * lc:(e + 1) * lc], 0.0)
                x_scr[2 * p + e] = jnp.where(hmask[e], a_t[:, cols], 0.0) + pltpu.roll(av_e, dh, axis=1)

        for bi in range(nb):
            brows = slice(bi * R_SUB, (bi + 1) * R_SUB)
            a_blk = aab_scr[:, brows, :].reshape(npair * R_SUB, LANES)
            a_hi = a_blk.astype(BF16)
            a_lo = (a_blk - a_hi.astype(F32)).astype(BF16)
            coef_scr[...] = _dot(jnp.concatenate([a_hi, a_lo], axis=1), sel_ref[bi])
            xs = [x_scr[h, brows, :] for h in range(nheads)]
            if bi > 0:
                for p in range(npair):
                    slab = aoff_scr[p, brows, :]
                    lhs = jnp.concatenate([jnp.where(hmask[0], slab, jnp.zeros_like(slab)),
                                           jnp.where(hmask[1], slab, jnp.zeros_like(slab))], axis=0)
                    xst = x_scr[2 * p:2 * p + 2].reshape(2 * lc, LANES).astype(BF16)
                    corr = _dot(lhs, xst)
                    for e in range(2):
                        xs[2 * p + e] = xs[2 * p + e] - corr[e * R_SUB:(e + 1) * R_SUB]
            x_lo = [x[:R_HALF] for x in xs]
            x_hi = [x[R_HALF:] for x in xs]
            for s in range(R_HALF - 1):
                for h in range(nheads):
                    p, e = h // 2, h % 2
                    tile = e * (R_HALF - 1) + s
                    coef = coef_scr[p * R_SUB:(p + 1) * R_SUB, tile * LANES:(tile + 1) * LANES]
                    row = x_lo[h][s:s + 1, :]
                    x_lo[h] = x_lo[h] - coef[:R_HALF] * row
                    x_hi[h] = x_hi[h] - coef[R_HALF:] * row
            for s in range(R_HALF - 1, R_SUB - 1):
                for h in range(nheads):
                    p, e = h // 2, h % 2
                    col = e * lc + bi * R_SUB + s
                    coef = aab_scr[p, bi * R_SUB + R_HALF:(bi + 1) * R_SUB, col:col + 1]
                    row = x_lo[h][s:s + 1, :] if s < R_HALF else x_hi[h][s - R_HALF:s - R_HALF + 1, :]
                    x_hi[h] = x_hi[h] - coef * row
            for h in range(nheads):
                x_scr[h, bi * R_SUB:bi * R_SUB + R_HALF, :] = x_lo[h]
                x_scr[h, bi * R_SUB + R_HALF:(bi + 1) * R_SUB, :] = x_hi[h]

        hts, ubs = [], []
        for p in range(npair):
            x0, x1 = x_scr[2 * p], x_scr[2 * p + 1]
            w_pair = jnp.where(hmask[0], x0, x1)
            uv_pair = pltpu.roll(jnp.where(hmask[0], x1, x0), dh, axis=1)
            ht = ht_scr[p]
            hts.append(ht)
            ubs.append((_dot_nt(w_pair.astype(BF16), ht.astype(BF16)) + uv_pair).astype(BF16))
        for p in range(npair):
            cols = slice(p * LANES, (p + 1) * LANES)
            arkb = arkb_scr[p]
            lhs = jnp.concatenate([jnp.where(hmask2[e], arkb, jnp.zeros_like(arkb)) for e in range(2)], axis=0)
            rhs = jnp.concatenate([vp2s[p], -ubs[p], -ubs[p]], axis=0)
            intra = _dot(lhs, rhs)
            y = _dot_nt(r_t[:, cols].astype(BF16), hts[p].astype(BF16))
            ybuf[rows, cols] = y + jnp.where(hmask[0], intra[:lc], intra[lc:])
        for p in range(npair):
            cols = slice(p * LANES, (p + 1) * LANES)
            vu = jnp.concatenate([vps[p], -ubs[p]], axis=0)
            kb = jnp.concatenate([k_h[:, cols], b_h[:, cols]], axis=0).astype(BF16)
            upd = lax.dot_general(vu, kb, (((0,), (0,)), ((), ())), preferred_element_type=F32)
            ht_scr[p] = hts[p] * gam[:, cols] + jnp.where(bdiag, upd, 0.0)
        return carry

    lax.fori_loop(0, lt // lc, chunk, 0)

    rr2 = lax.broadcasted_iota(jnp.int32, (2 * LANES, LANES), 0)
    cc2 = lax.broadcasted_iota(jnp.int32, (2 * LANES, LANES), 1)
    ones_bd = (((rr2 % LANES) < dh) == (cc2 < dh)).astype(BF16)
    for p in range(npair):
        cols = slice(p * LANES, (p + 1) * LANES)
        y = ybuf[:, cols]
        r, k, v = r_ref[:, cols], k_ref[:, cols], v_ref[:, cols]
        bonus = _head_sum(r * k * rk_ref[:, cols], lane) * v
        yc = y - _head_sum_mxu(y, ones_bd) * (1.0 / dh)
        yn = yc * lax.rsqrt(_head_sum_mxu(yc * yc, ones_bd) * (1.0 / dh) + R_LN_EPS)
        ybuf[:, cols] = (yn * lng_ref[:, cols] + lnb_ref[:, cols] + bonus) * g_ref[:, cols]
    out_ref[...] = h_ref[...] + _dot(ybuf[...].astype(BF16), wout_ref[...])


def _rwkv_layer(h2d, bsz, seq, norm_g, mu, w_in, w0, w_up, a0, a_up, g_up, k_k, k_a, r_k, ln_g, ln_b, w_out,
                *, tm=512, lt=256, lc=64):
    t, d = h2d.shape
    c0 = 3 * d
    c1 = c0 + w_up.shape[0]
    c2 = c1 + a_up.shape[0]
    pad_c = lambda w, n: jnp.pad(w, ((0, 0), (0, n - w.shape[1]))).astype(BF16)
    pad_r = lambda w, n: jnp.pad(w, ((0, n - w.shape[0]), (0, 0))).astype(BF16)
    up = lambda n: -(-n // LANES) * LANES
    nw, na, ng = up(c1 - c0), up(c2 - c1), up(w_in.shape[1] - c2)
    wr, wk, wv = (w_in[:, j * d:(j + 1) * d].astype(BF16) for j in range(3))
    wd, wa, wg = pad_c(w_in[:, c0:c1], nw), pad_c(w_in[:, c1:c2], na), pad_c(w_in[:, c2:], ng)
    row = lambda a: a.reshape(1, d)
    tile = pl.BlockSpec((tm, d), lambda i: (i, 0))
    outs = pl.pallas_call(
        functools.partial(_rw_proj_kernel, tm=tm, seq=seq),
        grid=(t // tm,),
        in_specs=[tile, _const_spec((1, d)), _const_spec((6, d)), _const_spec((d, d)), _const_spec((d, d)),
                  _const_spec((d, d)), _const_spec((d, nw)), _const_spec((d, na)), _const_spec((d, ng)),
                  _const_spec((nw, d)), _const_spec((na, d)), _const_spec((ng, d)),
                  _const_spec((1, d)), _const_spec((1, d)), _const_spec((1, d)), _const_spec((1, d))],
        out_specs=[tile] * 7,
        out_shape=[jax.ShapeDtypeStruct((t, d), F32)] * 7,
        scratch_shapes=[pltpu.VMEM((tm + 8, d), F32)],
        compiler_params=_cparams("arbitrary"),
        name="rw_proj",
    )(h2d, row(norm_g), mu, wr, wk, wv, wd, wa, wg, pad_r(w_up, nw), pad_r(a_up, na), pad_r(g_up, ng),
      row(w0), row(a0), row(k_k), row(k_a))
    r, lw, k, v, al, be, g = outs

    nt = seq // lt
    assert lc == R_HEAD_DIM, "pair tiles put the two heads' (lc, lc) matrices side by side on 128 lanes"
    nb, nmx = lc // R_SUB, R_HALF - 1
    nsel = 2 * nmx * LANES
    klane = jnp.arange(2 * LANES)[None, :, None] % LANES
    otile = jnp.arange(nsel)[None, None, :] // LANES
    sel = ((klane // lc == otile // nmx)
           & (klane % lc == jnp.arange(nb)[:, None, None] * R_SUB + otile % nmx)).astype(BF16)
    spec = pl.BlockSpec((lt, d), lambda b, i: (b * nt + i, 0))
    return pl.pallas_call(
        functools.partial(_rw_kernel, lt=lt, lc=lc),
        grid=(bsz, nt),
        in_specs=[spec] * 8 + [_const_spec((1, d))] * 3 + [_const_spec((d, d)), _const_spec(sel.shape)],
        out_specs=spec,
        out_shape=jax.ShapeDtypeStruct((t, d), F32),
        scratch_shapes=[pltpu.VMEM((d // LANES, LANES, LANES), F32), pltpu.VMEM((lt, d), F32),
                        pltpu.VMEM((d // LANES, lc, LANES), F32), pltpu.VMEM((d // LANES, lc, LANES), BF16),
                        pltpu.VMEM((d // LANES, lc, 2 * LANES), BF16),
                        pltpu.VMEM((R_HEADS, lc, LANES), F32), pltpu.VMEM((d // LANES * R_SUB, nsel), F32)],
        compiler_params=_cparams("parallel", "arbitrary"),
        name="rw_mix",
    )(r, lw, k, v, al, be, g, h2d, r_k.reshape(1, d), row(ln_g), row(ln_b), w_out.astype(BF16), sel)


def kernel(x, mlp_norm_g, mlp_w_up, mlp_w_down, final_norm_g, s5_norm_g, s5_a_re, s5_a_im, s5_log_dt, s5_b_re, s5_b_im, s5_c_re, s5_c_im, s5_d, s5_w_glu, s5_b_glu, ml_norm_g, ml_w_in, ml_conv_w, ml_conv_b, ml_b_i, ml_b_f, ml_head_g, ml_w_out, fox_norm_g, fox_w_in, fox_b_f, fox_w_out, rw_norm_g, rw_mu, rw_w_in, rw_w0, rw_w_up, rw_a0, rw_a_up, rw_g_up, rw_k_k, rw_k_a, rw_r_k, rw_ln_g, rw_ln_b, rw_w_out):
    bsz, seq, d = x.shape
    depth = mlp_w_up.shape[0]
    h = x.reshape(bsz * seq, d)
    for i in range(depth):
        kind, j = i % 4, i // 4
        if kind == 0:
            h = _s5_layer(h, bsz, seq, s5_norm_g[j], s5_a_re[j], s5_a_im[j], s5_log_dt[j], s5_b_re[j], s5_b_im[j],
                          s5_c_re[j], s5_c_im[j], s5_d[j], s5_w_glu[j], s5_b_glu[j])
        elif kind == 1:
            h = _mlstm_layer(h, bsz, seq, ml_norm_g[j], ml_w_in[j], ml_conv_w[j], ml_conv_b[j], ml_b_i[j],
                             ml_b_f[j], ml_head_g[j], ml_w_out[j])
        elif kind == 2:
            h = _fox_layer(h, bsz, seq, fox_norm_g[j], fox_w_in[j], fox_b_f[j], fox_w_out[j])
        else:
            h = _rwkv_layer(h, bsz, seq, rw_norm_g[j], rw_mu[j], rw_w_in[j], rw_w0[j], rw_w_up[j], rw_a0[j],
                            rw_a_up[j], rw_g_up[j], rw_k_k[j], rw_k_a[j], rw_r_k[j], rw_ln_g[j], rw_ln_b[j],
                            rw_w_out[j])
        h = _mlp_layer(h, mlp_norm_g[i], mlp_w_up[i], mlp_w_down[i],
                       final_g=final_norm_g if i == depth - 1 else None)
    return h.reshape(bsz, seq, d)
```

```python
import functools
import math

import jax
import jax.numpy as jnp
from jax import lax
from jax.experimental import pallas as pl
from jax.experimental.pallas import tpu as pltpu

F32 = jnp.float32
BF16 = jnp.bfloat16

NORM_EPS = 1e-6
NEG_BIG = -1e30
LANES = 128
VMEM_LIMIT = 56 * 1024 * 1024

M_HEADS = 8
M_QK_DIM = 64
M_V_DIM = 128
M_CONV = 4
M_NORM_EPS = 1e-6


def _cparams(*sem):
    return pltpu.CompilerParams(dimension_semantics=sem, vmem_limit_bytes=VMEM_LIMIT)


def _rms(x, g):
    return x * lax.rsqrt(jnp.mean(x * x, axis=-1, keepdims=True) + NORM_EPS) * g


def _dot(a, b):
    return jnp.dot(a, b, preferred_element_type=F32)


def _dot_nt(a, b):
    return lax.dot_general(a, b, (((1,), (1,)), ((), ())), preferred_element_type=F32)


def _split3(x):
    hi = x.astype(BF16)
    r = x - hi.astype(F32)
    mid = r.astype(BF16)
    lo = (r - mid.astype(F32)).astype(BF16)
    return hi, mid, lo


def _dot_exact_l(ones_mat, x):
    hi, mid, lo = _split3(x)
    return _dot(ones_mat, hi) + _dot(ones_mat, mid) + _dot(ones_mat, lo)


def _dot_exact_r(x, ones_mat):
    hi, mid, lo = _split3(x)
    return _dot(hi, ones_mat) + _dot(mid, ones_mat) + _dot(lo, ones_mat)


def _log_sigmoid(x):
    return jnp.minimum(x, 0.0) - jnp.log(1.0 + jnp.exp(-jnp.abs(x)))


def _sigmoid(x):
    return 1.0 / (1.0 + jnp.exp(-x))


def _const_spec(shape):
    nd = len(shape)
    return pl.BlockSpec(shape, lambda *_: (0,) * nd)


def _mlp_kernel(h_ref, g_ref, wup_ref, wdn_ref, fg_ref, out_ref, *, fc, final):
    x = h_ref[...]
    xn = _rms(x, g_ref[...]).astype(BF16)
    acc = x
    for c in range(wup_ref.shape[1] // fc):
        hid = jnp.maximum(_dot(xn, wup_ref[:, c * fc:(c + 1) * fc]), 0.0)
        acc = acc + _dot((hid * hid).astype(BF16), wdn_ref[c * fc:(c + 1) * fc, :])
    out_ref[...] = _rms(acc, fg_ref[...]) if final else acc


def _mlp_layer(h2d, norm_g, w_up, w_down, final_g=None, *, tm=512, fc=512):
    t, d = h2d.shape
    ff = w_up.shape[1]
    fg = (norm_g if final_g is None else final_g).reshape(1, d)
    return pl.pallas_call(
        functools.partial(_mlp_kernel, fc=fc, final=final_g is not None),
        grid=(t // tm,),
        in_specs=[pl.BlockSpec((tm, d), lambda i: (i, 0)), _const_spec((1, d)),
                  _const_spec((d, ff)), _const_spec((ff, d)), _const_spec((1, d))],
        out_specs=pl.BlockSpec((tm, d), lambda i: (i, 0)),
        out_shape=jax.ShapeDtypeStruct((t, d), F32),
        compiler_params=_cparams("parallel"),
        name="mlp",
    )(h2d, norm_g.reshape(1, d), w_up.astype(BF16), w_down.astype(BF16), fg)


def _ml_proj_kernel(h_ref, g_ref, wqk_ref, wv_ref, wo_ref, wg_ref, wgt_ref,
                    qk_ref, v_ref, o_ref, gc_ref, gr_ref):
    xn = _rms(h_ref[...], g_ref[...]).astype(BF16)
    qk_ref[...] = _dot(xn, wqk_ref[...])
    v_ref[...] = _dot(xn, wv_ref[...]).astype(v_ref.dtype)
    o_ref[...] = _dot(xn, wo_ref[...])
    gc_ref[...] = _dot(xn, wg_ref[...])
    gr_ref[...] = _dot_nt(wgt_ref[...], xn)


def _ml_kernel(qk_ref, v_ref, o_ref, gc_ref, gr_ref, h_ref, cw_ref, cb_ref, bcol_ref, brow_ref,
               hg_ref, wout_ref, rep_ref, out_ref, ext_scr, c_scr, m_scr, hbuf, *, lt, lc):
    nh, dk, dv = M_HEADS, M_QK_DIM, M_V_DIM
    hd = nh * dk

    @pl.when(pl.program_id(1) == 0)
    def _():
        c_scr[...] = jnp.zeros_like(c_scr)
        m_scr[...] = jnp.zeros_like(m_scr)
        ext_scr[0:8, :] = jnp.zeros((8, 2 * hd), F32)

    ext_scr[8:8 + lt, :] = qk_ref[...]
    conv = cb_ref[...] + jnp.zeros((lt, 2 * hd), F32)
    for kk in range(M_CONV):
        conv = conv + cw_ref[kk:kk + 1, :] * ext_scr[8 - (M_CONV - 1) + kk:8 - (M_CONV - 1) + kk + lt, :]
    ext_scr[0:8, :] = ext_scr[lt:lt + 8, :]
    act = conv * _sigmoid(conv)

    lane = lax.broadcasted_iota(jnp.int32, (1, LANES), 1)
    r_i = lax.broadcasted_iota(jnp.int32, (lc, lc), 0)
    c_i = lax.broadcasted_iota(jnp.int32, (lc, lc), 1)
    causal = c_i <= r_i
    tri_l = causal.astype(BF16)
    tri_u = (r_i <= c_i).astype(BF16)
    ones_blk = jnp.ones((lc, LANES), BF16)
    row16 = lax.broadcasted_iota(jnp.int32, (2 * nh, 1), 0)

    def both(x):
        return jnp.concatenate([x, x], axis=1)

    local = [[None] * nh for _ in range(lt // lc)]
    for c in range(lt // lc):
        rows = slice(c * lc, (c + 1) * lc)
        gcb = gc_ref[rows, :] + bcol_ref[...]
        gate_c = jnp.where(lane < nh, gcb, jnp.where(lane < 2 * nh, _log_sigmoid(gcb), 0.0))
        cum_c = _dot_exact_l(tri_l, jnp.where(lane < nh, 0.0, gate_c))
        hi, mid, lo = _split3(jnp.where(lane < nh, gate_c, cum_c))
        rep = _dot(jnp.concatenate([hi, mid, lo], axis=1), rep_ref[...])
        grb = gr_ref[:, rows] + brow_ref[...]
        gate_r = jnp.where(row16 < nh, grb, _log_sigmoid(grb))
        cum_r = _dot_exact_r(jnp.where(row16 < nh, 0.0, gate_r), tri_u)

        for h in range(nh):
            p, e = h // 2, h % 2
            head_mask = (lane >= e * dk) & (lane < (e + 1) * dk)
            q_m = (jnp.where(head_mask, act[rows, p * LANES:(p + 1) * LANES], 0.0) * (dk ** -0.5)).astype(BF16)
            k_m = jnp.where(head_mask, act[rows, hd + p * LANES:hd + (p + 1) * LANES], 0.0)
            v_aug = jnp.concatenate([v_ref[rows, h * dv:(h + 1) * dv].astype(BF16), ones_blk], axis=1)
            li_c = rep[:, h * LANES:(h + 1) * LANES]
            b_c = rep[:, (nh + h) * LANES:(nh + h + 1) * LANES]
            b_row = cum_r[nh + h:nh + h + 1, :]
            li_row = gate_r[h:h + 1, :]
            d_log = jnp.where(causal, b_c[:, :lc] - b_row + li_row, NEG_BIG)
            m_loc = jnp.max(d_log, axis=-1, keepdims=True) + jnp.zeros((lc, LANES), F32)
            s_loc = _dot_nt(q_m, k_m.astype(BF16)) * jnp.exp(d_log - m_loc[:, :lc])
            r_loc = _dot(s_loc.astype(BF16), v_aug)
            b_last = b_c[lc - 1:lc, :]
            m_loc_last = m_loc[lc - 1:lc, :]
            w_state = jnp.exp(b_last - b_c + li_c - m_loc_last)
            kv_loc = _dot(jnp.transpose(k_m * w_state).astype(BF16), v_aug)
            local[c][h] = (q_m, b_c, m_loc, r_loc, kv_loc, b_last, m_loc_last)

    for c in range(lt // lc):
        rows = slice(c * lc, (c + 1) * lc)
        for h in range(nh):
            q_m, b_c, m_loc, r_loc, kv_loc, b_last, m_loc_last = local[c][h]
            m_prev = m_scr[h:h + 1, :]
            g = b_c + m_prev
            m_t = jnp.maximum(g, m_loc)
            c_aug = c_scr[h]
            r = both(jnp.exp(g - m_t)) * _dot(q_m, c_aug.astype(BF16)) + both(jnp.exp(m_loc - m_t)) * r_loc
            hh = r[:, :dv] / jnp.maximum(jnp.abs(r[:, dv:]), jnp.exp(-m_t))

            m_new = m_t[lc - 1:lc, :]
            decay = jnp.exp(b_last + m_prev - m_new)
            c_scr[h] = both(decay) * c_aug + both(jnp.exp(m_loc_last - m_new)) * kv_loc
            m_scr[h:h + 1, :] = m_new

            yc = hh - jnp.mean(hh, axis=-1, keepdims=True)
            yn = yc * lax.rsqrt(jnp.mean(yc * yc, axis=-1, keepdims=True) + M_NORM_EPS)
            cols = slice(h * dv, (h + 1) * dv)
            hbuf[rows, cols] = yn * hg_ref[:, cols] * _sigmoid(o_ref[rows, cols])

    out_ref[...] = h_ref[...] + _dot(hbuf[...].astype(BF16), wout_ref[...])


def _mlstm_layer(h2d, bsz, seq, norm_g, w_in, conv_w, conv_b, b_i, b_f, head_g, w_out, *, tm=512, lt=256, lc=128):
    t, d = h2d.shape
    nh, dk, dv = M_HEADS, M_QK_DIM, M_V_DIM
    o0 = 2 * nh * dk
    o1 = o0 + nh * dv
    o3 = o1 + 2 * nh
    wqk = w_in[:, :o0].astype(BF16)
    wv = w_in[:, o0:o1].astype(BF16)
    wo = w_in[:, o3:].astype(BF16)
    wg = jnp.pad(w_in[:, o1:o3], ((0, 0), (0, LANES - 2 * nh))).astype(BF16)
    wgt = jnp.transpose(w_in[:, o1:o3]).astype(BF16)
    g2 = norm_g.reshape(1, d)

    qk, v, o, gc, gr = pl.pallas_call(
        _ml_proj_kernel,
        grid=(t // tm,),
        in_specs=[pl.BlockSpec((tm, d), lambda i: (i, 0)), _const_spec((1, d)),
                  _const_spec(wqk.shape), _const_spec(wv.shape), _const_spec(wo.shape),
                  _const_spec(wg.shape), _const_spec(wgt.shape)],
        out_specs=[pl.BlockSpec((tm, o0), lambda i: (i, 0)), pl.BlockSpec((tm, nh * dv), lambda i: (i, 0)),
                   pl.BlockSpec((tm, nh * dv), lambda i: (i, 0)), pl.BlockSpec((tm, LANES), lambda i: (i, 0)),
                   pl.BlockSpec((2 * nh, tm), lambda i: (0, i))],
        out_shape=[jax.ShapeDtypeStruct((t, o0), F32), jax.ShapeDtypeStruct((t, nh * dv), BF16),
                   jax.ShapeDtypeStruct((t, nh * dv), F32), jax.ShapeDtypeStruct((t, LANES), F32),
                   jax.ShapeDtypeStruct((2 * nh, t), F32)],
        compiler_params=_cparams("parallel"),
        name="ml_proj",
    )(h2d, g2, wqk, wv, wo, wg, wgt)

    bcol = jnp.pad(jnp.concatenate([b_i, b_f]), (0, LANES - 2 * nh)).reshape(1, LANES)
    brow = jnp.concatenate([b_i, b_f]).reshape(2 * nh, 1)
    nt = seq // lt
    rep = (jnp.arange(3 * LANES)[:, None] % LANES == jnp.arange(2 * nh * LANES)[None, :] // LANES).astype(BF16)
    row_spec = lambda w: pl.BlockSpec((lt, w), lambda b, i: (b * nt + i, 0))
    return pl.pallas_call(
        functools.partial(_ml_kernel, lt=lt, lc=lc),
        grid=(bsz, nt),
        in_specs=[row_spec(o0), row_spec(nh * dv), row_spec(nh * dv), row_spec(LANES),
                  pl.BlockSpec((2 * nh, lt), lambda b, i: (0, b * nt + i)), row_spec(d),
                  _const_spec((M_CONV, o0)), _const_spec((1, o0)), _const_spec((1, LANES)),
                  _const_spec((2 * nh, 1)), _const_spec((1, nh * dv)), _const_spec((nh * dv, d)),
                  _const_spec(rep.shape)],
        out_specs=row_spec(d),
        out_shape=jax.ShapeDtypeStruct((t, d), F32),
        scratch_shapes=[pltpu.VMEM((lt + 8, o0), F32), pltpu.VMEM((nh, LANES, 2 * LANES), F32),
                        pltpu.VMEM((nh, LANES), F32), pltpu.VMEM((lt, nh * dv), F32)],
        compiler_params=_cparams("parallel", "arbitrary"),
        name="ml_mix",
    )(qk, v, o, gc, gr, h2d, conv_w, conv_b.reshape(1, o0), bcol, brow,
      head_g.reshape(1, nh * dv), w_out.astype(BF16), rep)


S5_GROUP_CH = 16
S5_STATE = 64
S5_CHUNK_GROUPS = 4
S5_CHUNK = S5_CHUNK_GROUPS * S5_STATE


def _s5_disc_kernel(ar_ref, ai_ref, ldt_ref, arx_ref, aix_ref, br_ref, bi_ref,
                    abr_ref, abi_ref, bbr_ref, bbi_ref):
    dt = jnp.exp(ldt_ref[...])

    def abar(ar, ai):
        mag = jnp.exp(dt * ar)
        return mag * jnp.cos(dt * ai), mag * jnp.sin(dt * ai)

    abr_ref[...], abi_ref[...] = abar(ar_ref[...], ai_ref[...])
    ar, ai = arx_ref[...], aix_ref[...]
    xr, xi = abar(ar, ai)
    inv = 1.0 / (ar * ar + ai * ai)
    zr, zi = xr - 1.0, xi
    coef_re = (zr * ar + zi * ai) * inv
    coef_im = (zi * ar - zr * ai) * inv
    br, bi = br_ref[...], bi_ref[...]
    bbr_ref[...] = coef_re * br - coef_im * bi
    bbi_ref[...] = coef_re * bi + coef_im * br


def _s5_kernel(h_ref, g_ref, abr_ref, abi_ref, bre_ref, bim_ref, cre_ref, cim_ref, dsk_ref, wglu_ref, bglu_ref,
               out_ref, sr_scr, si_scr, *, tt, bsz):
    d = h_ref.shape[1]
    nchunk = sr_scr.shape[0]
    per_slab = nchunk // (d // S5_CHUNK)

    @pl.when(pl.program_id(0) == 0)
    def _():
        sr_scr[...] = jnp.zeros_like(sr_scr)
        si_scr[...] = jnp.zeros_like(si_scr)

    x = h_ref[...]
    u = _rms(x, g_ref[...])
    ub = u.astype(BF16)
    ys = [jnp.zeros((tt * bsz, S5_CHUNK), F32) for _ in range(d // S5_CHUNK)]
    for c in range(nchunk):
        slab = c // per_slab
        us = ub[:, slab * S5_CHUNK:(slab + 1) * S5_CHUNK]
        bu_re = _dot(us, bre_ref[c])
        bu_im = _dot(us, bim_ref[c])
        ar, ai = abr_ref[c], abi_ref[c]
        sr, si = sr_scr[c], si_scr[c]
        s_re, s_im = [], []
        for t in range(tt):
            rows = slice(t * bsz, (t + 1) * bsz)
            sr, si = ar * sr - ai * si + bu_re[rows], ar * si + ai * sr + bu_im[rows]
            s_re.append(sr)
            s_im.append(si)
        sr_scr[c] = sr
        si_scr[c] = si
        ys[slab] = (ys[slab] + _dot(jnp.concatenate(s_re, axis=0).astype(BF16), cre_ref[c])
                    - _dot(jnp.concatenate(s_im, axis=0).astype(BF16), cim_ref[c]))
    y = jnp.concatenate(ys, axis=1) + dsk_ref[...] * u
    gelu = 0.5 * y * (1.0 + jnp.tanh(math.sqrt(2.0 / math.pi) * (y + 0.044715 * (y * y * y))))
    z = _dot(gelu.astype(BF16), wglu_ref[...]) + bglu_ref[...]
    out_ref[...] = x + z[:, :d] * _sigmoid(z[:, d:])


def _s5_layer(h2d, bsz, seq, norm_g, a_re, a_im, log_dt, b_re, b_im, c_re, c_im, d_skip, w_glu, b_glu, *, tt=8):
    t, d = h2d.shape
    ng, ns, hg = a_re.shape[0], S5_STATE, S5_GROUP_CH
    cg = S5_CHUNK_GROUPS
    nchunk = ng // cg
    per_slab = S5_CHUNK // (cg * hg)

    rep = lambda a: jnp.repeat(a, hg, axis=1)
    abr, abi, bbr, bbi = pl.pallas_call(
        _s5_disc_kernel,
        out_shape=[jax.ShapeDtypeStruct((ng, ns), F32)] * 2 + [jax.ShapeDtypeStruct((ng, ns * hg), F32)] * 2,
        name="s5_disc",
    )(a_re, a_im, log_dt.reshape(ng, 1), rep(a_re), rep(a_im), b_re.reshape(ng, ns * hg), b_im.reshape(ng, ns * hg))

    eye = jnp.eye(cg, dtype=F32)
    place = jax.nn.one_hot(jnp.arange(nchunk) % per_slab, per_slab, dtype=F32)

    def b_blocks(bb):
        m = bb.reshape(nchunk, cg, ns, hg)
        blk = jnp.einsum('cgph,gq->cghqp', m, eye).reshape(nchunk, cg * hg, cg * ns)
        return jnp.einsum('crs,cj->cjrs', blk, place).reshape(nchunk, per_slab * cg * hg, cg * ns).astype(BF16)

    def c_blocks(cc):
        m = cc.reshape(nchunk, cg, hg, ns)
        blk = jnp.einsum('cghp,gq->cgpqh', m, eye).reshape(nchunk, cg * ns, cg * hg)
        return jnp.einsum('csr,cj->csjr', blk, place).reshape(nchunk, cg * ns, per_slab * cg * hg).astype(BF16)

    rows = tt * bsz
    ht = h2d.reshape(bsz, seq, d).transpose(1, 0, 2).reshape(t, d)
    out = pl.pallas_call(
        functools.partial(_s5_kernel, tt=tt, bsz=bsz),
        grid=(seq // tt,),
        in_specs=[pl.BlockSpec((rows, d), lambda i: (i, 0)), _const_spec((1, d)),
                  _const_spec((nchunk, 1, S5_CHUNK)), _const_spec((nchunk, 1, S5_CHUNK)),
                  _const_spec((nchunk, S5_CHUNK, S5_CHUNK)), _const_spec((nchunk, S5_CHUNK, S5_CHUNK)),
                  _const_spec((nchunk, S5_CHUNK, S5_CHUNK)), _const_spec((nchunk, S5_CHUNK, S5_CHUNK)),
                  _const_spec((1, d)), _const_spec((d, 2 * d)), _const_spec((1, 2 * d))],
        out_specs=pl.BlockSpec((rows, d), lambda i: (i, 0)),
        out_shape=jax.ShapeDtypeStruct((t, d), F32),
        scratch_shapes=[pltpu.VMEM((nchunk, bsz, S5_CHUNK), F32), pltpu.VMEM((nchunk, bsz, S5_CHUNK), F32)],
        compiler_params=_cparams("arbitrary"),
        name="s5_mix",
    )(ht, norm_g.reshape(1, d), abr.reshape(nchunk, 1, S5_CHUNK), abi.reshape(nchunk, 1, S5_CHUNK),
      b_blocks(bbr), b_blocks(bbi), c_blocks(c_re), c_blocks(c_im),
      d_skip.reshape(1, d), w_glu.astype(BF16), b_glu.reshape(1, 2 * d))
    return out.reshape(seq, bsz, d).transpose(1, 0, 2).reshape(t, d)


F_HEADS = 16
F_HEAD_DIM = 64
F_DIAG_ROWS = 128
LOG2E = math.log2(math.e)


def _fox_proj_kernel(h_ref, g_ref, wq_ref, wk_ref, wv_ref, wo_ref, wgt_ref,
                     q_ref, k_ref, v_ref, o_ref, gr_ref):
    xn = _rms(h_ref[...], g_ref[...]).astype(BF16)
    q_ref[...] = (_dot(xn, wq_ref[...]) * (F_HEAD_DIM ** -0.5 * LOG2E)).astype(BF16)
    k_ref[...] = _dot(xn, wk_ref[...]).astype(BF16)
    v_ref[...] = _dot(xn, wv_ref[...]).astype(BF16)
    o_ref[...] = _dot(xn, wo_ref[...])
    gr_ref[...] = _dot_nt(wgt_ref[...], xn)


def _fox_kernel(q_ref, k_ref, v_ref, o_ref, gr_ref, h_ref, brow_ref, wout_ref,
                out_ref, cr_scr, obuf, m_scr, l_scr, acc_scr, *, tq, tk, seq):
    nh, dh = F_HEADS, F_HEAD_DIM
    i = pl.program_id(1)
    lane = lax.broadcasted_iota(jnp.int32, (1, LANES), 1)
    hmask = [lane < dh, lane >= dh]

    @pl.when(i == 0)
    def _():
        r_i = lax.broadcasted_iota(jnp.int32, (LANES, LANES), 0)
        c_i = lax.broadcasted_iota(jnp.int32, (LANES, LANES), 1)
        tri_u = (r_i <= c_i).astype(BF16)
        carry_r = jnp.zeros((nh, 1), F32)
        per = tk // LANES
        for blk in range(seq // LANES):
            rows = slice(blk * LANES, (blk + 1) * LANES)
            cr = _dot_exact_r(_log_sigmoid(gr_ref[:, rows] + brow_ref[...]), tri_u) + carry_r
            cr_scr[blk // per, :, (blk % per) * LANES:(blk % per + 1) * LANES] = cr * LOG2E
            carry_r = cr[:, LANES - 1:LANES]

    m_scr[...] = jnp.full(m_scr.shape, NEG_BIG, F32)
    l_scr[...] = jnp.zeros_like(l_scr)
    acc_scr[...] = jnp.zeros_like(acc_scr)

    def step(j, qrow0, nrows, width, masked):
        k0 = pl.multiple_of(j * tk, tk)
        crow = cr_scr[j][:, :width]
        qrows = slice(qrow0, qrow0 + nrows)
        if masked:
            causal = (lax.broadcasted_iota(jnp.int32, (nrows, width), 1)
                      <= lax.broadcasted_iota(jnp.int32, (nrows, width), 0) + qrow0)
        for p in range(nh // 2):
            cols = slice(p * LANES, (p + 1) * LANES)
            kp = k_ref[pl.ds(k0, width), cols]
            vp = v_ref[pl.ds(k0, width), cols]
            qp = q_ref[qrows, cols]
            q2 = jnp.concatenate([jnp.where(hmask[e], qp, jnp.zeros_like(qp)) for e in range(2)], axis=0)
            s2 = _dot_nt(q2, kp)
            dm, ls, pms = [], [], []
            for e in range(2):
                h = 2 * p + e
                s = s2[e * nrows:(e + 1) * nrows] - crow[h:h + 1, :]
                if masked:
                    s = jnp.where(causal, s, NEG_BIG)
                m_prev = m_scr[h, qrows, :]
                m_next = jnp.maximum(m_prev, jnp.max(s, axis=-1, keepdims=True))
                m_scr[h, qrows, :] = m_next
                pm = jnp.exp2(s - jnp.concatenate([m_next] * (width // LANES), axis=1))
                dm.append(m_prev - m_next)
                ls.append(jnp.sum(pm, axis=-1, keepdims=True))
                pms.append(pm.astype(BF16))
            pv = _dot(jnp.concatenate(pms, axis=0), vp)
            alpha = jnp.exp2(jnp.where(hmask[0], dm[0], dm[1]))
            l_scr[p, qrows, :] = alpha * l_scr[p, qrows, :] + jnp.where(hmask[0], ls[0], ls[1])
            acc_scr[p, qrows, :] = alpha * acc_scr[p, qrows, :] + jnp.where(hmask[0], pv[:nrows], pv[nrows:])

    def body(j, carry):
        step(j, 0, tq, tk, False)
        return carry

    lax.fori_loop(0, i, body, 0)
    for r in range(tq // F_DIAG_ROWS):
        step(i, r * F_DIAG_ROWS, F_DIAG_ROWS, (r + 1) * F_DIAG_ROWS, True)
    for p in range(nh // 2):
        cols = slice(p * LANES, (p + 1) * LANES)
        obuf[:, cols] = acc_scr[p] / l_scr[p] * _sigmoid(o_ref[:, cols])

    out_ref[...] = h_ref[...] + _dot(obuf[...].astype(BF16), wout_ref[...])


def _fox_layer(h2d, bsz, seq, norm_g, w_in, b_f, w_out, *, tm=512, tq=256):
    t, d = h2d.shape
    nh, dh = F_HEADS, F_HEAD_DIM
    hd = nh * dh
    wq, wk, wv = (w_in[:, j * hd:(j + 1) * hd].astype(BF16) for j in range(3))
    wf = w_in[:, 3 * hd:3 * hd + nh]
    wo = w_in[:, 3 * hd + nh:].astype(BF16)
    wgt = jnp.transpose(wf).astype(BF16)
    tile = lambda w: pl.BlockSpec((tm, w), lambda i: (i, 0))
    q, k, v, o, gr = pl.pallas_call(
        _fox_proj_kernel,
        grid=(t // tm,),
        in_specs=[tile(d), _const_spec((1, d)), _const_spec(wq.shape), _const_spec(wk.shape),
                  _const_spec(wv.shape), _const_spec(wo.shape), _const_spec(wgt.shape)],
        out_specs=[tile(hd), tile(hd), tile(hd), tile(hd), pl.BlockSpec((nh, tm), lambda i: (0, i))],
        out_shape=[jax.ShapeDtypeStruct((t, hd), BF16)] * 3 + [jax.ShapeDtypeStruct((t, hd), F32),
                   jax.ShapeDtypeStruct((nh, t), F32)],
        compiler_params=_cparams("parallel"),
        name="fox_proj",
    )(h2d, norm_g.reshape(1, d), wq, wk, wv, wo, wgt)

    nt = seq // tq
    q_spec = lambda w: pl.BlockSpec((tq, w), lambda b, i: (b * nt + i, 0))
    seq_spec = lambda w: pl.BlockSpec((seq, w), lambda b, i: (b, 0))
    return pl.pallas_call(
        functools.partial(_fox_kernel, tq=tq, tk=tq, seq=seq),
        grid=(bsz, nt),
        in_specs=[q_spec(hd), seq_spec(hd), seq_spec(hd), q_spec(hd),
                  pl.BlockSpec((nh, seq), lambda b, i: (0, b)), q_spec(d),
                  _const_spec((nh, 1)), _const_spec((hd, d))],
        out_specs=q_spec(d),
        out_shape=jax.ShapeDtypeStruct((t, d), F32),
        scratch_shapes=[pltpu.VMEM((nt, nh, tq), F32), pltpu.VMEM((tq, hd), F32),
                        pltpu.VMEM((nh, tq, LANES), F32), pltpu.VMEM((nh // 2, tq, LANES), F32),
                        pltpu.VMEM((nh // 2, tq, LANES), F32)],
        compiler_params=_cparams("parallel", "arbitrary"),
        name="fox_attn",
    )(q, k, v, o, gr, h2d, b_f.reshape(nh, 1), w_out.astype(BF16))


R_HEADS = 16
R_HEAD_DIM = 64
R_LN_EPS = 64e-5
R_SUB = 16
R_HALF = R_SUB // 2


def _softplus(z):
    return jnp.maximum(z, 0.0) + jnp.log(1.0 + jnp.exp(-jnp.abs(z)))


def _head_sum(x, lane):
    lo = lane < R_HEAD_DIM
    s0 = jnp.sum(jnp.where(lo, x, 0.0), axis=-1, keepdims=True)
    s1 = jnp.sum(jnp.where(lo, 0.0, x), axis=-1, keepdims=True)
    return jnp.where(lo, s0, s1)


def _rw_proj_kernel(h_ref, g_ref, mu_ref, wr_ref, wk_ref, wv_ref, wd_ref, wa_ref, wg_ref, wup_ref, aup_ref,
                    gup_ref, w0_ref, a0_ref, kk_ref, ka_ref,
                    r_ref, lw_ref, k_ref, v_ref, al_ref, be_ref, g_out_ref, ext_scr, *, tm, seq):
    d = h_ref.shape[1]
    i = pl.program_id(0)
    xn = _rms(h_ref[...], g_ref[...])
    @pl.when((i * tm) % seq == 0)
    def _():
        ext_scr[7:8, :] = jnp.zeros((1, d), F32)

    @pl.when((i * tm) % seq != 0)
    def _():
        ext_scr[7:8, :] = ext_scr[7 + tm:8 + tm, :]

    ext_scr[8:8 + tm, :] = xn
    xb = xn.astype(BF16)
    dxb = (ext_scr[7:7 + tm, :] - xn).astype(BF16)
    mub = mu_ref[...].astype(BF16)
    mix = lambda j: xb + dxb * mub[j:j + 1, :]
    r_ref[...] = _dot(mix(0), wr_ref[...])
    w_in = w0_ref[...] + _dot(jnp.tanh(_dot(mix(1), wd_ref[...])).astype(BF16), wup_ref[...])
    lw_ref[...] = -math.exp(-0.5) * _sigmoid(w_in)
    k = _dot(mix(2), wk_ref[...])
    v_ref[...] = _dot(mix(3), wv_ref[...])
    a = _sigmoid(a0_ref[...] + _dot(_dot(mix(4), wa_ref[...]).astype(BF16), aup_ref[...]))
    g_out_ref[...] = _dot(_sigmoid(_dot(mix(5), wg_ref[...])).astype(BF16), gup_ref[...])
    k_ref[...] = k * (1.0 + (a - 1.0) * ka_ref[...])
    lane = lax.broadcasted_iota(jnp.int32, (1, LANES), 1)
    for p in range(d // LANES):
        cols = slice(p * LANES, (p + 1) * LANES)
        kk = k[:, cols] * kk_ref[:, cols]
        kk = kk * jnp.minimum(lax.rsqrt(_head_sum(kk * kk, lane)), 1e12)
        al_ref[:, cols] = kk
        be_ref[:, cols] = kk * a[:, cols]


def _head_sum_mxu(x, ones_bd):
    hi = x.astype(BF16)
    lo = (x - hi.astype(F32)).astype(BF16)
    return _dot(jnp.concatenate([hi, lo], axis=1), ones_bd)


def _rw_kernel(r_ref, lw_ref, k_ref, v_ref, al_ref, be_ref, g_ref, h_ref, rk_ref, lng_ref, lnb_ref, wout_ref,
               sel_ref, out_ref, ht_scr, ybuf, aab_scr, aoff_scr, arkb_scr, x_scr, coef_scr, *, lt, lc):
    d = h_ref.shape[1]
    dh = R_HEAD_DIM
    npair = d // LANES
    nb = lc // R_SUB

    @pl.when(pl.program_id(1) == 0)
    def _():
        ht_scr[...] = jnp.zeros_like(ht_scr)

    lane = lax.broadcasted_iota(jnp.int32, (1, LANES), 1)
    hmask = [lane < dh, lane >= dh]
    lane2 = lax.broadcasted_iota(jnp.int32, (1, 2 * LANES), 1) % LANES
    hmask2 = [lane2 < dh, lane2 >= dh]
    r_i = lax.broadcasted_iota(jnp.int32, (lc, lc), 0)
    c_i = lax.broadcasted_iota(jnp.int32, (lc, lc), 1)
    tri_l = (c_i <= r_i).astype(BF16)
    r_p2 = lax.broadcasted_iota(jnp.int32, (lc, LANES), 0)
    c_p2 = lax.broadcasted_iota(jnp.int32, (lc, LANES), 1) % lc
    strict2 = c_p2 < r_p2
    incl2 = c_p2 <= r_p2
    below2 = (c_p2 // R_SUB) < (r_p2 // R_SUB)
    rr = lax.broadcasted_iota(jnp.int32, (LANES, LANES), 0)
    cc = lax.broadcasted_iota(jnp.int32, (LANES, LANES), 1)
    bdiag = (rr < dh) == (cc < dh)

    def chunk(c, carry):
        rows = slice(c * lc, (c + 1) * lc)
        lw = lw_ref[rows, :]
        cum = _dot_exact_l(tri_l, lw)
        cum_last = cum[lc - 1:lc, :]
        e_pos = jnp.exp(cum)
        e_neg = jnp.exp(-cum)
        e_rem = jnp.exp(cum_last - cum)
        r_t = r_ref[rows, :] * e_pos
        a_t = al_ref[rows, :] * jnp.exp(cum - lw)
        k_all = k_ref[rows, :]
        b_all = be_ref[rows, :]
        k_t, b_t = k_all * e_neg, b_all * e_neg
        k_h, b_h = k_all * e_rem, b_all * e_rem
        gam = jnp.exp(cum_last)

        nheads = 2 * npair
        vps = [v_ref[rows, p * LANES:(p + 1) * LANES].astype(BF16) for p in range(npair)]
        vp2s = [jnp.concatenate([vp, vp], axis=0) for vp in vps]

        scs = []
        for p in range(npair):
            cols = slice(p * LANES, (p + 1) * LANES)
            lhs = jnp.concatenate([a_t[:, cols], r_t[:, cols]], axis=0).astype(BF16)
            b_p, k_p = b_t[:, cols], k_t[:, cols]
            rhs = jnp.concatenate([jnp.where(hmask[0], b_p, 0.0), jnp.where(hmask[1], b_p, 0.0),
                                   jnp.where(hmask[0], k_p, 0.0), jnp.where(hmask[1], k_p, 0.0)], axis=0)
            scs.append(_dot_nt(lhs, rhs.astype(BF16)))
        aaks = []
        for p in range(npair):
            sc = scs[p]
            top_b, top_k = sc[:lc, :LANES], sc[:lc, LANES:]
            bot_b, bot_k = sc[lc:, :LANES], sc[lc:, LANES:]
            aab_scr[p] = jnp.where(strict2, top_b, 0.0)
            aoff_scr[p] = jnp.where(below2, top_b, 0.0).astype(BF16)
            arkb_scr[p] = jnp.concatenate([jnp.where(incl2, bot_k, 0.0), jnp.where(incl2, bot_b, 0.0)],
                                          axis=1).astype(BF16)
            a_ak = jnp.where(strict2, top_k, 0.0)
            aaks.append(jnp.concatenate([jnp.where(hmask[0], a_ak, 0.0), jnp.where(hmask[1], a_ak, 0.0)],
                                        axis=0).astype(BF16))
        for p in range(npair):
            cols = slice(p * LANES, (p + 1) * LANES)
            av = _dot(aaks[p], vp2s[p])
            for e in range(2):
                av_e = jnp.where(hmask[e], av[e * lc:(e + 1) * lc], 0.0)
                x_scr[2 * p + e] = jnp.where(hmask[e], a_t[:, cols], 0.0) + pltpu.roll(av_e, dh, axis=1)

        for bi in range(nb):
            brows = slice(bi * R_SUB, (bi + 1) * R_SUB)
            a_blk = aab_scr[:, brows, :].reshape(npair * R_SUB, LANES)
            a_hi = a_blk.astype(BF16)
            a_lo = (a_blk - a_hi.astype(F32)).astype(BF16)
            coef_scr[...] = _dot(jnp.concatenate([a_hi, a_lo], axis=1), sel_ref[bi])
            xs = [x_scr[h, brows, :] for h in range(nheads)]
            if bi > 0:
                for p in range(npair):
                    slab = aoff_scr[p, brows, :]
                    lhs = jnp.concatenate([jnp.where(hmask[0], slab, jnp.zeros_like(slab)),
                                           jnp.where(hmask[1], slab, jnp.zeros_like(slab))], axis=0)
                    xst = x_scr[2 * p:2 * p + 2].reshape(2 * lc, LANES).astype(BF16)
                    corr = _dot(lhs, xst)
                    for e in range(2):
                        xs[2 * p + e] = xs[2 * p + e] - corr[e * R_SUB:(e + 1) * R_SUB]
            x_lo = [x[:R_HALF] for x in xs]
            x_hi = [x[R_HALF:] for x in xs]
            for s in range(R_HALF - 1):
                for h in range(nheads):
                    p, e = h // 2, h % 2
                    tile = e * (R_HALF - 1) + s
                    coef = coef_scr[p * R_SUB:(p + 1) * R_SUB, tile * LANES:(tile + 1) * LANES]
                    row = x_lo[h][s:s + 1, :]
                    x_lo[h] = x_lo[h] - coef[:R_HALF] * row
                    x_hi[h] = x_hi[h] - coef[R_HALF:] * row
            for s in range(R_HALF - 1, R_SUB - 1):
                for h in range(nheads):
                    p, e = h // 2, h % 2
                    col = e * lc + bi * R_SUB + s
                    coef = aab_scr[p, bi * R_SUB + R_HALF:(bi + 1) * R_SUB, col:col + 1]
                    row = x_lo[h][s:s + 1, :] if s < R_HALF else x_hi[h][s - R_HALF:s - R_HALF + 1, :]
                    x_hi[h] = x_hi[h] - coef * row
            for h in range(nheads):
                x_scr[h, bi * R_SUB:bi * R_SUB + R_HALF, :] = x_lo[h]
                x_scr[h, bi * R_SUB + R_HALF:(bi + 1) * R_SUB, :] = x_hi[h]

        hts, ubs, yhs = [], [], []
        for p in range(npair):
            cols = slice(p * LANES, (p + 1) * LANES)
            x0, x1 = x_scr[2 * p], x_scr[2 * p + 1]
            w_pair = jnp.where(hmask[0], x0, x1)
            uv_pair = pltpu.roll(jnp.where(hmask[0], x1, x0), dh, axis=1)
            ht = ht_scr[p]
            hts.append(ht)
            wr = jnp.concatenate([w_pair, r_t[:, cols]], axis=0).astype(BF16)
            wh = _dot_nt(wr, ht.astype(BF16))
            ubs.append((wh[:lc] + uv_pair).astype(BF16))
            yhs.append(wh[lc:])
        for p in range(npair):
            cols = slice(p * LANES, (p + 1) * LANES)
            arkb = arkb_scr[p]
            lhs = jnp.concatenate([jnp.where(hmask2[e], arkb, jnp.zeros_like(arkb)) for e in range(2)], axis=0)
            rhs = jnp.concatenate([vp2s[p], -ubs[p], -ubs[p]], axis=0)
            intra = _dot(lhs, rhs)
            ybuf[rows, cols] = yhs[p] + jnp.where(hmask[0], intra[:lc], intra[lc:])
        for p in range(npair):
            cols = slice(p * LANES, (p + 1) * LANES)
            vu = jnp.concatenate([vps[p], -ubs[p]], axis=0)
            kb = jnp.concatenate([k_h[:, cols], b_h[:, cols]], axis=0).astype(BF16)
            upd = lax.dot_general(vu, kb, (((0,), (0,)), ((), ())), preferred_element_type=F32)
            ht_scr[p] = hts[p] * gam[:, cols] + jnp.where(bdiag, upd, 0.0)
        return carry

    for c in range(lt // lc):
        chunk(c, 0)

    rr2 = lax.broadcasted_iota(jnp.int32, (2 * LANES, LANES), 0)
    cc2 = lax.broadcasted_iota(jnp.int32, (2 * LANES, LANES), 1)
    ones_bd = (((rr2 % LANES) < dh) == (cc2 < dh)).astype(BF16)
    for p in range(npair):
        cols = slice(p * LANES, (p + 1) * LANES)
        y = ybuf[:, cols]
        r, k, v = r_ref[:, cols], k_ref[:, cols], v_ref[:, cols]
        bonus = _head_sum(r * k * rk_ref[:, cols], lane) * v
        yc = y - _head_sum_mxu(y, ones_bd) * (1.0 / dh)
        yn = yc * lax.rsqrt(_head_sum_mxu(yc * yc, ones_bd) * (1.0 / dh) + R_LN_EPS)
        ybuf[:, cols] = (yn * lng_ref[:, cols] + lnb_ref[:, cols] + bonus) * g_ref[:, cols]
    out_ref[...] = h_ref[...] + _dot(ybuf[...].astype(BF16), wout_ref[...])


def _rwkv_layer(h2d, bsz, seq, norm_g, mu, w_in, w0, w_up, a0, a_up, g_up, k_k, k_a, r_k, ln_g, ln_b, w_out,
                *, tm=512, lt=256, lc=64):
    t, d = h2d.shape
    c0 = 3 * d
    c1 = c0 + w_up.shape[0]
    c2 = c1 + a_up.shape[0]
    pad_c = lambda w, n: jnp.pad(w, ((0, 0), (0, n - w.shape[1]))).astype(BF16)
    pad_r = lambda w, n: jnp.pad(w, ((0, n - w.shape[0]), (0, 0))).astype(BF16)
    up = lambda n: -(-n // LANES) * LANES
    nw, na, ng = up(c1 - c0), up(c2 - c1), up(w_in.shape[1] - c2)
    wr, wk, wv = (w_in[:, j * d:(j + 1) * d].astype(BF16) for j in range(3))
    wd, wa, wg = pad_c(w_in[:, c0:c1], nw), pad_c(w_in[:, c1:c2], na), pad_c(w_in[:, c2:], ng)
    row = lambda a: a.reshape(1, d)
    tile = pl.BlockSpec((tm, d), lambda i: (i, 0))
    outs = pl.pallas_call(
        functools.partial(_rw_proj_kernel, tm=tm, seq=seq),
        grid=(t // tm,),
        in_specs=[tile, _const_spec((1, d)), _const_spec((6, d)), _const_spec((d, d)), _const_spec((d, d)),
                  _const_spec((d, d)), _const_spec((d, nw)), _const_spec((d, na)), _const_spec((d, ng)),
                  _const_spec((nw, d)), _const_spec((na, d)), _const_spec((ng, d)),
                  _const_spec((1, d)), _const_spec((1, d)), _const_spec((1, d)), _const_spec((1, d))],
        out_specs=[tile] * 7,
        out_shape=[jax.ShapeDtypeStruct((t, d), F32)] * 7,
        scratch_shapes=[pltpu.VMEM((tm + 8, d), F32)],
        compiler_params=_cparams("arbitrary"),
        name="rw_proj",
    )(h2d, row(norm_g), mu, wr, wk, wv, wd, wa, wg, pad_r(w_up, nw), pad_r(a_up, na), pad_r(g_up, ng),
      row(w0), row(a0), row(k_k), row(k_a))
    r, lw, k, v, al, be, g = outs

    nt = seq // lt
    assert lc == R_HEAD_DIM, "pair tiles put the two heads' (lc, lc) matrices side by side on 128 lanes"
    nb, nmx = lc // R_SUB, R_HALF - 1
    nsel = 2 * nmx * LANES
    klane = jnp.arange(2 * LANES)[None, :, None] % LANES
    otile = jnp.arange(nsel)[None, None, :] // LANES
    sel = ((klane // lc == otile // nmx)
           & (klane % lc == jnp.arange(nb)[:, None, None] * R_SUB + otile % nmx)).astype(BF16)
    spec = pl.BlockSpec((lt, d), lambda b, i: (b * nt + i, 0))
    return pl.pallas_call(
        functools.partial(_rw_kernel, lt=lt, lc=lc),
        grid=(bsz, nt),
        in_specs=[spec] * 8 + [_const_spec((1, d))] * 3 + [_const_spec((d, d)), _const_spec(sel.shape)],
        out_specs=spec,
        out_shape=jax.ShapeDtypeStruct((t, d), F32),
        scratch_shapes=[pltpu.VMEM((d // LANES, LANES, LANES), F32), pltpu.VMEM((lt, d), F32),
                        pltpu.VMEM((d // LANES, lc, LANES), F32), pltpu.VMEM((d // LANES, lc, LANES), BF16),
                        pltpu.VMEM((d // LANES, lc, 2 * LANES), BF16),
                        pltpu.VMEM((R_HEADS, lc, LANES), F32), pltpu.VMEM((d // LANES * R_SUB, nsel), F32)],
        compiler_params=_cparams("parallel", "arbitrary"),
        name="rw_mix",
    )(r, lw, k, v, al, be, g, h2d, r_k.reshape(1, d), row(ln_g), row(ln_b), w_out.astype(BF16), sel)


def kernel(x, mlp_norm_g, mlp_w_up, mlp_w_down, final_norm_g, s5_norm_g, s5_a_re, s5_a_im, s5_log_dt, s5_b_re, s5_b_im, s5_c_re, s5_c_im, s5_d, s5_w_glu, s5_b_glu, ml_norm_g, ml_w_in, ml_conv_w, ml_conv_b, ml_b_i, ml_b_f, ml_head_g, ml_w_out, fox_norm_g, fox_w_in, fox_b_f, fox_w_out, rw_norm_g, rw_mu, rw_w_in, rw_w0, rw_w_up, rw_a0, rw_a_up, rw_g_up, rw_k_k, rw_k_a, rw_r_k, rw_ln_g, rw_ln_b, rw_w_out):
    bsz, seq, d = x.shape
    depth = mlp_w_up.shape[0]
    h = x.reshape(bsz * seq, d)
    for i in range(depth):
        kind, j = i % 4, i // 4
        if kind == 0:
            h = _s5_layer(h, bsz, seq, s5_norm_g[j], s5_a_re[j], s5_a_im[j], s5_log_dt[j], s5_b_re[j], s5_b_im[j],
                          s5_c_re[j], s5_c_im[j], s5_d[j], s5_w_glu[j], s5_b_glu[j])
        elif kind == 1:
            h = _mlstm_layer(h, bsz, seq, ml_norm_g[j], ml_w_in[j], ml_conv_w[j], ml_conv_b[j], ml_b_i[j],
                             ml_b_f[j], ml_head_g[j], ml_w_out[j])
        elif kind == 2:
            h = _fox_layer(h, bsz, seq, fox_norm_g[j], fox_w_in[j], fox_b_f[j], fox_w_out[j])
        else:
            h = _rwkv_layer(h, bsz, seq, rw_norm_g[j], rw_mu[j], rw_w_in[j], rw_w0[j], rw_w_up[j], rw_a0[j],
                            rw_a_up[j], rw_g_up[j], rw_k_k[j], rw_k_a[j], rw_r_k[j], rw_ln_g[j], rw_ln_b[j],
                            rw_w_out[j])
        h = _mlp_layer(h, mlp_norm_g[i], mlp_w_up[i], mlp_w_down[i],
                       final_g=final_norm_g if i == depth - 1 else None)
    return h.reshape(bsz, seq, d)
```

```python
import functools
import math

import jax
import jax.numpy as jnp
from jax import lax
from jax.experimental import pallas as pl
from jax.experimental.pallas import tpu as pltpu

F32 = jnp.float32
BF16 = jnp.bfloat16

NORM_EPS = 1e-6
NEG_BIG = -1e30
LANES = 128
VMEM_LIMIT = 56 * 1024 * 1024

M_HEADS = 8
M_QK_DIM = 64
M_V_DIM = 128
M_CONV = 4
M_NORM_EPS = 1e-6


def _cparams(*sem):
    return pltpu.CompilerParams(dimension_semantics=sem, vmem_limit_bytes=VMEM_LIMIT)


def _rms(x, g):
    return x * lax.rsqrt(jnp.mean(x * x, axis=-1, keepdims=True) + NORM_EPS) * g


def _dot(a, b):
    return jnp.dot(a, b, preferred_element_type=F32)


def _dot_nt(a, b):
    return lax.dot_general(a, b, (((1,), (1,)), ((), ())), preferred_element_type=F32)


def _split3(x):
    hi = x.astype(BF16)
    r = x - hi.astype(F32)
    mid = r.astype(BF16)
    lo = (r - mid.astype(F32)).astype(BF16)
    return hi, mid, lo


def _dot_exact_l(ones_mat, x):
    hi, mid, lo = _split3(x)
    return _dot(ones_mat, hi) + _dot(ones_mat, mid) + _dot(ones_mat, lo)


def _dot_exact_r(x, ones_mat):
    hi, mid, lo = _split3(x)
    return _dot(hi, ones_mat) + _dot(mid, ones_mat) + _dot(lo, ones_mat)


def _log_sigmoid(x):
    return jnp.minimum(x, 0.0) - jnp.log(1.0 + jnp.exp(-jnp.abs(x)))


def _sigmoid(x):
    return 1.0 / (1.0 + jnp.exp(-x))


def _const_spec(shape):
    nd = len(shape)
    return pl.BlockSpec(shape, lambda *_: (0,) * nd)


def _mlp_kernel(h_ref, g_ref, wup_ref, wdn_ref, fg_ref, out_ref, *, fc, final):
    x = h_ref[...]
    xn = _rms(x, g_ref[...]).astype(BF16)
    acc = x
    for c in range(wup_ref.shape[1] // fc):
        hid = jnp.maximum(_dot(xn, wup_ref[:, c * fc:(c + 1) * fc]), 0.0)
        acc = acc + _dot((hid * hid).astype(BF16), wdn_ref[c * fc:(c + 1) * fc, :])
    out_ref[...] = _rms(acc, fg_ref[...]) if final else acc


def _mlp_layer(h2d, norm_g, w_up, w_down, final_g=None, *, tm=512, fc=512):
    t, d = h2d.shape
    ff = w_up.shape[1]
    fg = (norm_g if final_g is None else final_g).reshape(1, d)
    return pl.pallas_call(
        functools.partial(_mlp_kernel, fc=fc, final=final_g is not None),
        grid=(t // tm,),
        in_specs=[pl.BlockSpec((tm, d), lambda i: (i, 0)), _const_spec((1, d)),
                  _const_spec((d, ff)), _const_spec((ff, d)), _const_spec((1, d))],
        out_specs=pl.BlockSpec((tm, d), lambda i: (i, 0)),
        out_shape=jax.ShapeDtypeStruct((t, d), F32),
        compiler_params=_cparams("parallel"),
        name="mlp",
    )(h2d, norm_g.reshape(1, d), w_up.astype(BF16), w_down.astype(BF16), fg)


def _ml_proj_kernel(h_ref, g_ref, wqk_ref, wv_ref, wo_ref, wg_ref, wgt_ref,
                    qk_ref, v_ref, o_ref, gc_ref, gr_ref):
    xn = _rms(h_ref[...], g_ref[...]).astype(BF16)
    qk_ref[...] = _dot(xn, wqk_ref[...])
    v_ref[...] = _dot(xn, wv_ref[...]).astype(v_ref.dtype)
    o_ref[...] = _dot(xn, wo_ref[...])
    gc_ref[...] = _dot(xn, wg_ref[...])
    gr_ref[...] = _dot_nt(wgt_ref[...], xn)


def _ml_kernel(qk_ref, v_ref, o_ref, gc_ref, gr_ref, h_ref, cw_ref, cb_ref, bcol_ref, brow_ref,
               hg_ref, wout_ref, rep_ref, out_ref, ext_scr, c_scr, m_scr, hbuf, *, lt, lc):
    nh, dk, dv = M_HEADS, M_QK_DIM, M_V_DIM
    hd = nh * dk

    @pl.when(pl.program_id(1) == 0)
    def _():
        c_scr[...] = jnp.zeros_like(c_scr)
        m_scr[...] = jnp.zeros_like(m_scr)
        ext_scr[0:8, :] = jnp.zeros((8, 2 * hd), F32)

    ext_scr[8:8 + lt, :] = qk_ref[...]
    conv = cb_ref[...] + jnp.zeros((lt, 2 * hd), F32)
    for kk in range(M_CONV):
        conv = conv + cw_ref[kk:kk + 1, :] * ext_scr[8 - (M_CONV - 1) + kk:8 - (M_CONV - 1) + kk + lt, :]
    ext_scr[0:8, :] = ext_scr[lt:lt + 8, :]
    act = conv * _sigmoid(conv)

    lane = lax.broadcasted_iota(jnp.int32, (1, LANES), 1)
    r_i = lax.broadcasted_iota(jnp.int32, (lc, lc), 0)
    c_i = lax.broadcasted_iota(jnp.int32, (lc, lc), 1)
    causal = c_i <= r_i
    tri_l = causal.astype(BF16)
    tri_u = (r_i <= c_i).astype(BF16)
    ones_blk = jnp.ones((lc, LANES), BF16)
    row16 = lax.broadcasted_iota(jnp.int32, (2 * nh, 1), 0)

    def both(x):
        return jnp.concatenate([x, x], axis=1)

    local = [[None] * nh for _ in range(lt // lc)]
    for c in range(lt // lc):
        rows = slice(c * lc, (c + 1) * lc)
        gcb = gc_ref[rows, :] + bcol_ref[...]
        gate_c = jnp.where(lane < nh, gcb, jnp.where(lane < 2 * nh, _log_sigmoid(gcb), 0.0))
        cum_c = _dot_exact_l(tri_l, jnp.where(lane < nh, 0.0, gate_c))
        hi, mid, lo = _split3(jnp.where(lane < nh, gate_c, cum_c))
        rep = _dot(jnp.concatenate([hi, mid, lo], axis=1), rep_ref[...])
        grb = gr_ref[:, rows] + brow_ref[...]
        gate_r = jnp.where(row16 < nh, grb, _log_sigmoid(grb))
        cum_r = _dot_exact_r(jnp.where(row16 < nh, 0.0, gate_r), tri_u)

        for h in range(nh):
            p, e = h // 2, h % 2
            head_mask = (lane >= e * dk) & (lane < (e + 1) * dk)
            q_m = (jnp.where(head_mask, act[rows, p * LANES:(p + 1) * LANES], 0.0) * (dk ** -0.5)).astype(BF16)
            k_m = jnp.where(head_mask, act[rows, hd + p * LANES:hd + (p + 1) * LANES], 0.0)
            v_aug = jnp.concatenate([v_ref[rows, h * dv:(h + 1) * dv].astype(BF16), ones_blk], axis=1)
            li_c = rep[:, h * LANES:(h + 1) * LANES]
            b_c = rep[:, (nh + h) * LANES:(nh + h + 1) * LANES]
            b_row = cum_r[nh + h:nh + h + 1, :]
            li_row = gate_r[h:h + 1, :]
            d_log = jnp.where(causal, b_c[:, :lc] - b_row + li_row, NEG_BIG)
            m_loc = jnp.max(d_log, axis=-1, keepdims=True) + jnp.zeros((lc, LANES), F32)
            s_loc = _dot_nt(q_m, k_m.astype(BF16)) * jnp.exp(d_log - m_loc[:, :lc])
            r_loc = _dot(s_loc.astype(BF16), v_aug)
            b_last = b_c[lc - 1:lc, :]
            m_loc_last = m_loc[lc - 1:lc, :]
            w_state = jnp.exp(b_last - b_c + li_c - m_loc_last)
            kv_loc = _dot(jnp.transpose(k_m * w_state).astype(BF16), v_aug)
            local[c][h] = (q_m, b_c, m_loc, r_loc, kv_loc, b_last, m_loc_last)

    for c in range(lt // lc):
        rows = slice(c * lc, (c + 1) * lc)
        for h in range(nh):
            q_m, b_c, m_loc, r_loc, kv_loc, b_last, m_loc_last = local[c][h]
            m_prev = m_scr[h:h + 1, :]
            g = b_c + m_prev
            m_t = jnp.maximum(g, m_loc)
            c_aug = c_scr[h]
            r = both(jnp.exp(g - m_t)) * _dot(q_m, c_aug.astype(BF16)) + both(jnp.exp(m_loc - m_t)) * r_loc
            hh = r[:, :dv] / jnp.maximum(jnp.abs(r[:, dv:]), jnp.exp(-m_t))

            m_new = m_t[lc - 1:lc, :]
            decay = jnp.exp(b_last + m_prev - m_new)
            c_scr[h] = both(decay) * c_aug + both(jnp.exp(m_loc_last - m_new)) * kv_loc
            m_scr[h:h + 1, :] = m_new

            yc = hh - jnp.mean(hh, axis=-1, keepdims=True)
            yn = yc * lax.rsqrt(jnp.mean(yc * yc, axis=-1, keepdims=True) + M_NORM_EPS)
            cols = slice(h * dv, (h + 1) * dv)
            hbuf[rows, cols] = yn * hg_ref[:, cols] * _sigmoid(o_ref[rows, cols])

    out_ref[...] = h_ref[...] + _dot(hbuf[...].astype(BF16), wout_ref[...])


def _mlstm_layer(h2d, bsz, seq, norm_g, w_in, conv_w, conv_b, b_i, b_f, head_g, w_out, *, tm=512, lt=256, lc=128):
    t, d = h2d.shape
    nh, dk, dv = M_HEADS, M_QK_DIM, M_V_DIM
    o0 = 2 * nh * dk
    o1 = o0 + nh * dv
    o3 = o1 + 2 * nh
    wqk = w_in[:, :o0].astype(BF16)
    wv = w_in[:, o0:o1].astype(BF16)
    wo = w_in[:, o3:].astype(BF16)
    wg = jnp.pad(w_in[:, o1:o3], ((0, 0), (0, LANES - 2 * nh))).astype(BF16)
    wgt = jnp.transpose(w_in[:, o1:o3]).astype(BF16)
    g2 = norm_g.reshape(1, d)

    qk, v, o, gc, gr = pl.pallas_call(
        _ml_proj_kernel,
        grid=(t // tm,),
        in_specs=[pl.BlockSpec((tm, d), lambda i: (i, 0)), _const_spec((1, d)),
                  _const_spec(wqk.shape), _const_spec(wv.shape), _const_spec(wo.shape),
                  _const_spec(wg.shape), _const_spec(wgt.shape)],
        out_specs=[pl.BlockSpec((tm, o0), lambda i: (i, 0)), pl.BlockSpec((tm, nh * dv), lambda i: (i, 0)),
                   pl.BlockSpec((tm, nh * dv), lambda i: (i, 0)), pl.BlockSpec((tm, LANES), lambda i: (i, 0)),
                   pl.BlockSpec((2 * nh, tm), lambda i: (0, i))],
        out_shape=[jax.ShapeDtypeStruct((t, o0), F32), jax.ShapeDtypeStruct((t, nh * dv), BF16),
                   jax.ShapeDtypeStruct((t, nh * dv), F32), jax.ShapeDtypeStruct((t, LANES), F32),
                   jax.ShapeDtypeStruct((2 * nh, t), F32)],
        compiler_params=_cparams("parallel"),
        name="ml_proj",
    )(h2d, g2, wqk, wv, wo, wg, wgt)

    bcol = jnp.pad(jnp.concatenate([b_i, b_f]), (0, LANES - 2 * nh)).reshape(1, LANES)
    brow = jnp.concatenate([b_i, b_f]).reshape(2 * nh, 1)
    nt = seq // lt
    rep = (jnp.arange(3 * LANES)[:, None] % LANES == jnp.arange(2 * nh * LANES)[None, :] // LANES).astype(BF16)
    row_spec = lambda w: pl.BlockSpec((lt, w), lambda b, i: (b * nt + i, 0))
    return pl.pallas_call(
        functools.partial(_ml_kernel, lt=lt, lc=lc),
        grid=(bsz, nt),
        in_specs=[row_spec(o0), row_spec(nh * dv), row_spec(nh * dv), row_spec(LANES),
                  pl.BlockSpec((2 * nh, lt), lambda b, i: (0, b * nt + i)), row_spec(d),
                  _const_spec((M_CONV, o0)), _const_spec((1, o0)), _const_spec((1, LANES)),
                  _const_spec((2 * nh, 1)), _const_spec((1, nh * dv)), _const_spec((nh * dv, d)),
                  _const_spec(rep.shape)],
        out_specs=row_spec(d),
        out_shape=jax.ShapeDtypeStruct((t, d), F32),
        scratch_shapes=[pltpu.VMEM((lt + 8, o0), F32), pltpu.VMEM((nh, LANES, 2 * LANES), F32),
                        pltpu.VMEM((nh, LANES), F32), pltpu.VMEM((lt, nh * dv), F32)],
        compiler_params=_cparams("parallel", "arbitrary"),
        name="ml_mix",
    )(qk, v, o, gc, gr, h2d, conv_w, conv_b.reshape(1, o0), bcol, brow,
      head_g.reshape(1, nh * dv), w_out.astype(BF16), rep)


S5_GROUP_CH = 16
S5_STATE = 64
S5_CHUNK_GROUPS = 4
S5_CHUNK = S5_CHUNK_GROUPS * S5_STATE


def _s5_disc_kernel(ar_ref, ai_ref, ldt_ref, arx_ref, aix_ref, br_ref, bi_ref,
                    abr_ref, abi_ref, bbr_ref, bbi_ref):
    dt = jnp.exp(ldt_ref[...])

    def abar(ar, ai):
        mag = jnp.exp(dt * ar)
        return mag * jnp.cos(dt * ai), mag * jnp.sin(dt * ai)

    abr_ref[...], abi_ref[...] = abar(ar_ref[...], ai_ref[...])
    ar, ai = arx_ref[...], aix_ref[...]
    xr, xi = abar(ar, ai)
    inv = 1.0 / (ar * ar + ai * ai)
    zr, zi = xr - 1.0, xi
    coef_re = (zr * ar + zi * ai) * inv
    coef_im = (zi * ar - zr * ai) * inv
    br, bi = br_ref[...], bi_ref[...]
    bbr_ref[...] = coef_re * br - coef_im * bi
    bbi_ref[...] = coef_re * bi + coef_im * br


def _s5_kernel(h_ref, g_ref, abr_ref, abi_ref, bsl_ref, csl_ref, dsk_ref, wglu_ref, bglu_ref,
               out_ref, sr_scr, si_scr, *, tt, bsz):
    d = h_ref.shape[1]
    nchunk = sr_scr.shape[0]
    per_slab = nchunk // (d // S5_CHUNK)

    @pl.when(pl.program_id(0) == 0)
    def _():
        sr_scr[...] = jnp.zeros_like(sr_scr)
        si_scr[...] = jnp.zeros_like(si_scr)

    x = h_ref[...]
    u = _rms(x, g_ref[...])
    ub = u.astype(BF16)
    ys = []
    for slab in range(d // S5_CHUNK):
        bu = _dot(ub[:, slab * S5_CHUNK:(slab + 1) * S5_CHUNK], bsl_ref[slab])
        states = []
        for j in range(per_slab):
            c = slab * per_slab + j
            bu_re = bu[:, 2 * j * S5_CHUNK:(2 * j + 1) * S5_CHUNK]
            bu_im = bu[:, (2 * j + 1) * S5_CHUNK:(2 * j + 2) * S5_CHUNK]
            ar, ai = abr_ref[c], abi_ref[c]
            sr, si = sr_scr[c], si_scr[c]
            s_re, s_im = [], []
            for t in range(tt):
                rows = slice(t * bsz, (t + 1) * bsz)
                sr, si = ar * sr - ai * si + bu_re[rows], ar * si + ai * sr + bu_im[rows]
                s_re.append(sr)
                s_im.append(si)
            sr_scr[c] = sr
            si_scr[c] = si
            states += [jnp.concatenate(s_re, axis=0).astype(BF16), jnp.concatenate(s_im, axis=0).astype(BF16)]
        ys.append(_dot(jnp.concatenate(states, axis=1), csl_ref[slab]))
    y = jnp.concatenate(ys, axis=1) + dsk_ref[...] * u
    gelu = 0.5 * y * (1.0 + jnp.tanh(math.sqrt(2.0 / math.pi) * (y + 0.044715 * (y * y * y))))
    z = _dot(gelu.astype(BF16), wglu_ref[...]) + bglu_ref[...]
    out_ref[...] = x + z[:, :d] * _sigmoid(z[:, d:])


def _s5_layer(h2d, bsz, seq, norm_g, a_re, a_im, log_dt, b_re, b_im, c_re, c_im, d_skip, w_glu, b_glu, *, tt=16):
    t, d = h2d.shape
    ng, ns, hg = a_re.shape[0], S5_STATE, S5_GROUP_CH
    cg = S5_CHUNK_GROUPS
    nchunk = ng // cg
    per_slab = S5_CHUNK // (cg * hg)

    rep = lambda a: jnp.repeat(a, hg, axis=1)
    abr, abi, bbr, bbi = pl.pallas_call(
        _s5_disc_kernel,
        out_shape=[jax.ShapeDtypeStruct((ng, ns), F32)] * 2 + [jax.ShapeDtypeStruct((ng, ns * hg), F32)] * 2,
        name="s5_disc",
    )(a_re, a_im, log_dt.reshape(ng, 1), rep(a_re), rep(a_im), b_re.reshape(ng, ns * hg), b_im.reshape(ng, ns * hg))

    eye = jnp.eye(cg, dtype=F32)
    place = jax.nn.one_hot(jnp.arange(nchunk) % per_slab, per_slab, dtype=F32)

    def b_blocks(bb):
        m = bb.reshape(nchunk, cg, ns, hg)
        blk = jnp.einsum('cgph,gq->cghqp', m, eye).reshape(nchunk, cg * hg, cg * ns)
        return jnp.einsum('crs,cj->cjrs', blk, place).reshape(nchunk, per_slab * cg * hg, cg * ns).astype(BF16)

    def c_blocks(cc):
        m = cc.reshape(nchunk, cg, hg, ns)
        blk = jnp.einsum('cghp,gq->cgpqh', m, eye).reshape(nchunk, cg * ns, cg * hg)
        return jnp.einsum('csr,cj->csjr', blk, place).reshape(nchunk, cg * ns, per_slab * cg * hg).astype(BF16)

    rows = tt * bsz
    nslab = nchunk // per_slab
    bsl = jnp.stack([b_blocks(bbr), b_blocks(bbi)], axis=2).reshape(nslab, per_slab, S5_CHUNK, 2 * S5_CHUNK)
    bsl = bsl.transpose(0, 2, 1, 3).reshape(nslab, S5_CHUNK, per_slab * 2 * S5_CHUNK)
    csl = jnp.stack([c_blocks(c_re), -c_blocks(c_im)], axis=1).reshape(nslab, per_slab * 2 * S5_CHUNK, S5_CHUNK)
    ht = h2d.reshape(bsz, seq, d).transpose(1, 0, 2).reshape(t, d)
    out = pl.pallas_call(
        functools.partial(_s5_kernel, tt=tt, bsz=bsz),
        grid=(seq // tt,),
        in_specs=[pl.BlockSpec((rows, d), lambda i: (i, 0)), _const_spec((1, d)),
                  _const_spec((nchunk, 1, S5_CHUNK)), _const_spec((nchunk, 1, S5_CHUNK)),
                  _const_spec(bsl.shape), _const_spec(csl.shape),
                  _const_spec((1, d)), _const_spec((d, 2 * d)), _const_spec((1, 2 * d))],
        out_specs=pl.BlockSpec((rows, d), lambda i: (i, 0)),
        out_shape=jax.ShapeDtypeStruct((t, d), F32),
        scratch_shapes=[pltpu.VMEM((nchunk, bsz, S5_CHUNK), F32), pltpu.VMEM((nchunk, bsz, S5_CHUNK), F32)],
        compiler_params=_cparams("arbitrary"),
        name="s5_mix",
    )(ht, norm_g.reshape(1, d), abr.reshape(nchunk, 1, S5_CHUNK), abi.reshape(nchunk, 1, S5_CHUNK),
      bsl, csl, d_skip.reshape(1, d), w_glu.astype(BF16), b_glu.reshape(1, 2 * d))
    return out.reshape(seq, bsz, d).transpose(1, 0, 2).reshape(t, d)


F_HEADS = 16
F_HEAD_DIM = 64
F_DIAG_ROWS = 128
LOG2E = math.log2(math.e)


def _fox_proj_kernel(h_ref, g_ref, wq_ref, wk_ref, wv_ref, wo_ref, wgt_ref,
                     q_ref, k_ref, v_ref, o_ref, gr_ref):
    xn = _rms(h_ref[...], g_ref[...]).astype(BF16)
    q_ref[...] = (_dot(xn, wq_ref[...]) * (F_HEAD_DIM ** -0.5 * LOG2E)).astype(BF16)
    k_ref[...] = _dot(xn, wk_ref[...]).astype(BF16)
    v_ref[...] = _dot(xn, wv_ref[...]).astype(BF16)
    o_ref[...] = _dot(xn, wo_ref[...])
    gr_ref[...] = _dot_nt(wgt_ref[...], xn)


def _fox_kernel(q_ref, k_ref, v_ref, o_ref, gr_ref, h_ref, brow_ref, wout_ref,
                out_ref, cr_scr, obuf, m_scr, l_scr, acc_scr, *, tq, tk, seq):
    nh, dh = F_HEADS, F_HEAD_DIM
    i = pl.program_id(1)
    lane = lax.broadcasted_iota(jnp.int32, (1, LANES), 1)
    hmask = [lane < dh, lane >= dh]

    @pl.when(i == 0)
    def _():
        r_i = lax.broadcasted_iota(jnp.int32, (LANES, LANES), 0)
        c_i = lax.broadcasted_iota(jnp.int32, (LANES, LANES), 1)
        tri_u = (r_i <= c_i).astype(BF16)
        carry_r = jnp.zeros((nh, 1), F32)
        per = tk // LANES
        for blk in range(seq // LANES):
            rows = slice(blk * LANES, (blk + 1) * LANES)
            cr = _dot_exact_r(_log_sigmoid(gr_ref[:, rows] + brow_ref[...]), tri_u) + carry_r
            cr_scr[blk // per, :, (blk % per) * LANES:(blk % per + 1) * LANES] = cr * LOG2E
            carry_r = cr[:, LANES - 1:LANES]

    m_scr[...] = jnp.full(m_scr.shape, NEG_BIG, F32)
    l_scr[...] = jnp.zeros_like(l_scr)
    acc_scr[...] = jnp.zeros_like(acc_scr)

    def step(j, qrow0, nrows, width, masked):
        k0 = pl.multiple_of(j * tk, tk)
        crow = cr_scr[j][:, :width]
        qrows = slice(qrow0, qrow0 + nrows)
        if masked:
            causal = (lax.broadcasted_iota(jnp.int32, (nrows, width), 1)
                      <= lax.broadcasted_iota(jnp.int32, (nrows, width), 0) + qrow0)
        for p in range(nh // 2):
            cols = slice(p * LANES, (p + 1) * LANES)
            kp = k_ref[pl.ds(k0, width), cols]
            vp = v_ref[pl.ds(k0, width), cols]
            qp = q_ref[qrows, cols]
            q2 = jnp.concatenate([jnp.where(hmask[e], qp, jnp.zeros_like(qp)) for e in range(2)], axis=0)
            s2 = _dot_nt(q2, kp)
            dm, ls, pms = [], [], []
            for e in range(2):
                h = 2 * p + e
                s = s2[e * nrows:(e + 1) * nrows] - crow[h:h + 1, :]
                if masked:
                    s = jnp.where(causal, s, NEG_BIG)
                m_prev = m_scr[h, qrows, :]
                m_next = jnp.maximum(m_prev, jnp.max(s, axis=-1, keepdims=True))
                m_scr[h, qrows, :] = m_next
                pm = jnp.exp2(s - jnp.concatenate([m_next] * (width // LANES), axis=1))
                dm.append(m_prev - m_next)
                ls.append(jnp.sum(pm, axis=-1, keepdims=True))
                pms.append(pm.astype(BF16))
            pv = _dot(jnp.concatenate(pms, axis=0), vp)
            alpha = jnp.exp2(jnp.where(hmask[0], dm[0], dm[1]))
            l_scr[p, qrows, :] = alpha * l_scr[p, qrows, :] + jnp.where(hmask[0], ls[0], ls[1])
            acc_scr[p, qrows, :] = alpha * acc_scr[p, qrows, :] + jnp.where(hmask[0], pv[:nrows], pv[nrows:])

    def body(j, carry):
        step(j, 0, tq, tk, False)
        return carry

    lax.fori_loop(0, i, body, 0)
    for r in range(tq // F_DIAG_ROWS):
        step(i, r * F_DIAG_ROWS, F_DIAG_ROWS, (r + 1) * F_DIAG_ROWS, True)
    for p in range(nh // 2):
        cols = slice(p * LANES, (p + 1) * LANES)
        obuf[:, cols] = acc_scr[p] / l_scr[p] * _sigmoid(o_ref[:, cols])

    out_ref[...] = h_ref[...] + _dot(obuf[...].astype(BF16), wout_ref[...])


def _fox_layer(h2d, bsz, seq, norm_g, w_in, b_f, w_out, *, tm=512, tq=256):
    t, d = h2d.shape
    nh, dh = F_HEADS, F_HEAD_DIM
    hd = nh * dh
    wq, wk, wv = (w_in[:, j * hd:(j + 1) * hd].astype(BF16) for j in range(3))
    wf = w_in[:, 3 * hd:3 * hd + nh]
    wo = w_in[:, 3 * hd + nh:].astype(BF16)
    wgt = jnp.transpose(wf).astype(BF16)
    tile = lambda w: pl.BlockSpec((tm, w), lambda i: (i, 0))
    q, k, v, o, gr = pl.pallas_call(
        _fox_proj_kernel,
        grid=(t // tm,),
        in_specs=[tile(d), _const_spec((1, d)), _const_spec(wq.shape), _const_spec(wk.shape),
                  _const_spec(wv.shape), _const_spec(wo.shape), _const_spec(wgt.shape)],
        out_specs=[tile(hd), tile(hd), tile(hd), tile(hd), pl.BlockSpec((nh, tm), lambda i: (0, i))],
        out_shape=[jax.ShapeDtypeStruct((t, hd), BF16)] * 3 + [jax.ShapeDtypeStruct((t, hd), F32),
                   jax.ShapeDtypeStruct((nh, t), F32)],
        compiler_params=_cparams("parallel"),
        name="fox_proj",
    )(h2d, norm_g.reshape(1, d), wq, wk, wv, wo, wgt)

    nt = seq // tq
    q_spec = lambda w: pl.BlockSpec((tq, w), lambda b, i: (b * nt + i, 0))
    seq_spec = lambda w: pl.BlockSpec((seq, w), lambda b, i: (b, 0))
    return pl.pallas_call(
        functools.partial(_fox_kernel, tq=tq, tk=tq, seq=seq),
        grid=(bsz, nt),
        in_specs=[q_spec(hd), seq_spec(hd), seq_spec(hd), q_spec(hd),
                  pl.BlockSpec((nh, seq), lambda b, i: (0, b)), q_spec(d),
                  _const_spec((nh, 1)), _const_spec((hd, d))],
        out_specs=q_spec(d),
        out_shape=jax.ShapeDtypeStruct((t, d), F32),
        scratch_shapes=[pltpu.VMEM((nt, nh, tq), F32), pltpu.VMEM((tq, hd), F32),
                        pltpu.VMEM((nh, tq, LANES), F32), pltpu.VMEM((nh // 2, tq, LANES), F32),
                        pltpu.VMEM((nh // 2, tq, LANES), F32)],
        compiler_params=_cparams("parallel", "arbitrary"),
        name="fox_attn",
    )(q, k, v, o, gr, h2d, b_f.reshape(nh, 1), w_out.astype(BF16))


R_HEADS = 16
R_HEAD_DIM = 64
R_LN_EPS = 64e-5
R_SUB = 16
R_HALF = R_SUB // 2


def _softplus(z):
    return jnp.maximum(z, 0.0) + jnp.log(1.0 + jnp.exp(-jnp.abs(z)))


def _head_sum(x, lane):
    lo = lane < R_HEAD_DIM
    s0 = jnp.sum(jnp.where(lo, x, 0.0), axis=-1, keepdims=True)
    s1 = jnp.sum(jnp.where(lo, 0.0, x), axis=-1, keepdims=True)
    return jnp.where(lo, s0, s1)


def _rw_proj_kernel(h_ref, g_ref, mu_ref, wr_ref, wk_ref, wv_ref, wd_ref, wa_ref, wg_ref, wup_ref, aup_ref,
                    gup_ref, w0_ref, a0_ref, kk_ref, ka_ref,
                    r_ref, lw_ref, k_ref, v_ref, al_ref, be_ref, g_out_ref, ext_scr, *, tm, seq):
    d = h_ref.shape[1]
    i = pl.program_id(0)
    xn = _rms(h_ref[...], g_ref[...])
    @pl.when((i * tm) % seq == 0)
    def _():
        ext_scr[7:8, :] = jnp.zeros((1, d), F32)

    @pl.when((i * tm) % seq != 0)
    def _():
        ext_scr[7:8, :] = ext_scr[7 + tm:8 + tm, :]

    ext_scr[8:8 + tm, :] = xn
    xb = xn.astype(BF16)
    dxb = (ext_scr[7:7 + tm, :] - xn).astype(BF16)
    mub = mu_ref[...].astype(BF16)
    mix = lambda j: xb + dxb * mub[j:j + 1, :]
    r_ref[...] = _dot(mix(0), wr_ref[...]).astype(r_ref.dtype)
    w_in = w0_ref[...] + _dot(jnp.tanh(_dot(mix(1), wd_ref[...])).astype(BF16), wup_ref[...])
    lw_ref[...] = -math.exp(-0.5) * _sigmoid(w_in)
    k = _dot(mix(2), wk_ref[...])
    v_ref[...] = _dot(mix(3), wv_ref[...]).astype(v_ref.dtype)
    a = _sigmoid(a0_ref[...] + _dot(_dot(mix(4), wa_ref[...]).astype(BF16), aup_ref[...]))
    g_out_ref[...] = _dot(_sigmoid(_dot(mix(5), wg_ref[...])).astype(BF16), gup_ref[...]).astype(g_out_ref.dtype)
    k_ref[...] = (k * (1.0 + (a - 1.0) * ka_ref[...])).astype(k_ref.dtype)
    lane = lax.broadcasted_iota(jnp.int32, (1, LANES), 1)
    for p in range(d // LANES):
        cols = slice(p * LANES, (p + 1) * LANES)
        kk = k[:, cols] * kk_ref[:, cols]
        kk = kk * jnp.minimum(lax.rsqrt(_head_sum(kk * kk, lane)), 1e12)
        al_ref[:, cols] = kk.astype(al_ref.dtype)
        be_ref[:, cols] = (kk * a[:, cols]).astype(be_ref.dtype)


def _head_sum_mxu(x, ones_bd):
    hi = x.astype(BF16)
    lo = (x - hi.astype(F32)).astype(BF16)
    return _dot(jnp.concatenate([hi, lo], axis=1), ones_bd)


def _rw_kernel(r_ref, lw_ref, k_ref, v_ref, al_ref, be_ref, g_ref, h_ref, rk_ref, lng_ref, lnb_ref, wout_ref,
               sel_ref, out_ref, ht_scr, ybuf, aab_scr, aoff_scr, arkb_scr, x_scr, coef_scr, *, lt, lc):
    d = h_ref.shape[1]
    dh = R_HEAD_DIM
    npair = d // LANES
    nb = lc // R_SUB

    @pl.when(pl.program_id(1) == 0)
    def _():
        ht_scr[...] = jnp.zeros_like(ht_scr)

    lane = lax.broadcasted_iota(jnp.int32, (1, LANES), 1)
    hmask = [lane < dh, lane >= dh]
    lane2 = lax.broadcasted_iota(jnp.int32, (1, 2 * LANES), 1) % LANES
    hmask2 = [lane2 < dh, lane2 >= dh]
    r_i = lax.broadcasted_iota(jnp.int32, (lc, lc), 0)
    c_i = lax.broadcasted_iota(jnp.int32, (lc, lc), 1)
    tri_l = (c_i <= r_i).astype(BF16)
    r_p2 = lax.broadcasted_iota(jnp.int32, (lc, LANES), 0)
    c_p2 = lax.broadcasted_iota(jnp.int32, (lc, LANES), 1) % lc
    strict2 = c_p2 < r_p2
    incl2 = c_p2 <= r_p2
    below2 = (c_p2 // R_SUB) < (r_p2 // R_SUB)
    rr = lax.broadcasted_iota(jnp.int32, (LANES, LANES), 0)
    cc = lax.broadcasted_iota(jnp.int32, (LANES, LANES), 1)
    bdiag = (rr < dh) == (cc < dh)

    def chunk(c, carry):
        rows = slice(c * lc, (c + 1) * lc)
        lw = lw_ref[rows, :]
        cum = _dot_exact_l(tri_l, lw)
        cum_last = cum[lc - 1:lc, :]
        e_pos = jnp.exp(cum)
        e_neg = jnp.exp(-cum)
        e_rem = jnp.exp(cum_last - cum)
        r_t = r_ref[rows, :] * e_pos
        a_t = al_ref[rows, :] * jnp.exp(cum - lw)
        k_all = k_ref[rows, :]
        b_all = be_ref[rows, :]
        k_t, b_t = k_all * e_neg, b_all * e_neg
        k_h, b_h = k_all * e_rem, b_all * e_rem
        gam = jnp.exp(cum_last)

        nheads = 2 * npair
        vps = [v_ref[rows, p * LANES:(p + 1) * LANES].astype(BF16) for p in range(npair)]
        vp2s = [jnp.concatenate([vp, vp], axis=0) for vp in vps]

        scs = []
        for p in range(npair):
            cols = slice(p * LANES, (p + 1) * LANES)
            lhs = jnp.concatenate([a_t[:, cols], r_t[:, cols]], axis=0).astype(BF16)
            b_p, k_p = b_t[:, cols], k_t[:, cols]
            rhs = jnp.concatenate([jnp.where(hmask[0], b_p, 0.0), jnp.where(hmask[1], b_p, 0.0),
                                   jnp.where(hmask[0], k_p, 0.0), jnp.where(hmask[1], k_p, 0.0)], axis=0)
            scs.append(_dot_nt(lhs, rhs.astype(BF16)))
        aaks = []
        for p in range(npair):
            sc = scs[p]
            top_b, top_k = sc[:lc, :LANES], sc[:lc, LANES:]
            bot_b, bot_k = sc[lc:, :LANES], sc[lc:, LANES:]
            aab_scr[p] = jnp.where(strict2, top_b, 0.0)
            aoff_scr[p] = jnp.where(below2, top_b, 0.0).astype(BF16)
            arkb_scr[p] = jnp.concatenate([jnp.where(incl2, bot_k, 0.0), jnp.where(incl2, bot_b, 0.0)],
                                          axis=1).astype(BF16)
            a_ak = jnp.where(strict2, top_k, 0.0)
            aaks.append(jnp.concatenate([jnp.where(hmask[0], a_ak, 0.0), jnp.where(hmask[1], a_ak, 0.0)],
                                        axis=0).astype(BF16))
        for p in range(npair):
            cols = slice(p * LANES, (p + 1) * LANES)
            av = _dot(aaks[p], vp2s[p])
            for e in range(2):
                av_e = jnp.where(hmask[e], av[e * lc:(e + 1) * lc], 0.0)
                x_scr[2 * p + e] = jnp.where(hmask[e], a_t[:, cols], 0.0) + pltpu.roll(av_e, dh, axis=1)

        for bi in range(nb):
            brows = slice(bi * R_SUB, (bi + 1) * R_SUB)
            a_blk = aab_scr[:, brows, :].reshape(npair * R_SUB, LANES)
            a_hi = a_blk.astype(BF16)
            a_lo = (a_blk - a_hi.astype(F32)).astype(BF16)
            coef_scr[...] = _dot(jnp.concatenate([a_hi, a_lo], axis=1), sel_ref[bi])
            xs = [x_scr[h, brows, :] for h in range(nheads)]
            if bi > 0:
                for p in range(npair):
                    slab = aoff_scr[p, brows, :]
                    lhs = jnp.concatenate([jnp.where(hmask[0], slab, jnp.zeros_like(slab)),
                                           jnp.where(hmask[1], slab, jnp.zeros_like(slab))], axis=0)
                    xst = x_scr[2 * p:2 * p + 2].reshape(2 * lc, LANES).astype(BF16)
                    corr = _dot(lhs, xst)
                    for e in range(2):
                        xs[2 * p + e] = xs[2 * p + e] - corr[e * R_SUB:(e + 1) * R_SUB]
            x_lo = [x[:R_HALF] for x in xs]
            x_hi = [x[R_HALF:] for x in xs]
            for s in range(R_HALF - 1):
                for h in range(nheads):
                    p, e = h // 2, h % 2
                    tile = e * (R_HALF - 1) + s
                    coef = coef_scr[p * R_SUB:(p + 1) * R_SUB, tile * LANES:(tile + 1) * LANES]
                    row = x_lo[h][s:s + 1, :]
                    x_lo[h] = x_lo[h] - coef[:R_HALF] * row
                    x_hi[h] = x_hi[h] - coef[R_HALF:] * row
            for s in range(R_HALF - 1, R_SUB - 1):
                for h in range(nheads):
                    p, e = h // 2, h % 2
                    col = e * lc + bi * R_SUB + s
                    coef = aab_scr[p, bi * R_SUB + R_HALF:(bi + 1) * R_SUB, col:col + 1]
                    row = x_lo[h][s:s + 1, :] if s < R_HALF else x_hi[h][s - R_HALF:s - R_HALF + 1, :]
                    x_hi[h] = x_hi[h] - coef * row
            for h in range(nheads):
                x_scr[h, bi * R_SUB:bi * R_SUB + R_HALF, :] = x_lo[h]
                x_scr[h, bi * R_SUB + R_HALF:(bi + 1) * R_SUB, :] = x_hi[h]

        hts, ubs, yhs = [], [], []
        for p in range(npair):
            cols = slice(p * LANES, (p + 1) * LANES)
            x0, x1 = x_scr[2 * p], x_scr[2 * p + 1]
            w_pair = jnp.where(hmask[0], x0, x1)
            uv_pair = pltpu.roll(jnp.where(hmask[0], x1, x0), dh, axis=1)
            ht = ht_scr[p]
            hts.append(ht)
            wr = jnp.concatenate([w_pair, r_t[:, cols]], axis=0).astype(BF16)
            wh = _dot_nt(wr, ht.astype(BF16))
            ubs.append((wh[:lc] + uv_pair).astype(BF16))
            yhs.append(wh[lc:])
        for p in range(npair):
            cols = slice(p * LANES, (p + 1) * LANES)
            arkb = arkb_scr[p]
            lhs = jnp.concatenate([jnp.where(hmask2[e], arkb, jnp.zeros_like(arkb)) for e in range(2)], axis=0)
            rhs = jnp.concatenate([vp2s[p], -ubs[p], -ubs[p]], axis=0)
            intra = _dot(lhs, rhs)
            ybuf[rows, cols] = yhs[p] + jnp.where(hmask[0], intra[:lc], intra[lc:])
        for p in range(npair):
            cols = slice(p * LANES, (p + 1) * LANES)
            vu = jnp.concatenate([vps[p], -ubs[p]], axis=0)
            kb = jnp.concatenate([k_h[:, cols], b_h[:, cols]], axis=0).astype(BF16)
            upd = lax.dot_general(vu, kb, (((0,), (0,)), ((), ())), preferred_element_type=F32)
            ht_scr[p] = hts[p] * gam[:, cols] + jnp.where(bdiag, upd, 0.0)
        return carry

    for c in range(lt // lc):
        chunk(c, 0)

    rr2 = lax.broadcasted_iota(jnp.int32, (2 * LANES, LANES), 0)
    cc2 = lax.broadcasted_iota(jnp.int32, (2 * LANES, LANES), 1)
    ones_bd = (((rr2 % LANES) < dh) == (cc2 < dh)).astype(BF16)
    for p in range(npair):
        cols = slice(p * LANES, (p + 1) * LANES)
        y = ybuf[:, cols]
        r, k, v = r_ref[:, cols], k_ref[:, cols], v_ref[:, cols]
        bonus = _head_sum(r * k * rk_ref[:, cols], lane) * v
        yc = y - _head_sum_mxu(y, ones_bd) * (1.0 / dh)
        yn = yc * lax.rsqrt(_head_sum_mxu(yc * yc, ones_bd) * (1.0 / dh) + R_LN_EPS)
        ybuf[:, cols] = (yn * lng_ref[:, cols] + lnb_ref[:, cols] + bonus) * g_ref[:, cols]
    out_ref[...] = h_ref[...] + _dot(ybuf[...].astype(BF16), wout_ref[...])


def _rwkv_layer(h2d, bsz, seq, norm_g, mu, w_in, w0, w_up, a0, a_up, g_up, k_k, k_a, r_k, ln_g, ln_b, w_out,
                *, tm=512, lt=256, lc=64, act_dtype=BF16):
    t, d = h2d.shape
    c0 = 3 * d
    c1 = c0 + w_up.shape[0]
    c2 = c1 + a_up.shape[0]
    pad_c = lambda w, n: jnp.pad(w, ((0, 0), (0, n - w.shape[1]))).astype(BF16)
    pad_r = lambda w, n: jnp.pad(w, ((0, n - w.shape[0]), (0, 0))).astype(BF16)
    up = lambda n: -(-n // LANES) * LANES
    nw, na, ng = up(c1 - c0), up(c2 - c1), up(w_in.shape[1] - c2)
    wr, wk, wv = (w_in[:, j * d:(j + 1) * d].astype(BF16) for j in range(3))
    wd, wa, wg = pad_c(w_in[:, c0:c1], nw), pad_c(w_in[:, c1:c2], na), pad_c(w_in[:, c2:], ng)
    row = lambda a: a.reshape(1, d)
    tile = pl.BlockSpec((tm, d), lambda i: (i, 0))
    outs = pl.pallas_call(
        functools.partial(_rw_proj_kernel, tm=tm, seq=seq),
        grid=(t // tm,),
        in_specs=[tile, _const_spec((1, d)), _const_spec((6, d)), _const_spec((d, d)), _const_spec((d, d)),
                  _const_spec((d, d)), _const_spec((d, nw)), _const_spec((d, na)), _const_spec((d, ng)),
                  _const_spec((nw, d)), _const_spec((na, d)), _const_spec((ng, d)),
                  _const_spec((1, d)), _const_spec((1, d)), _const_spec((1, d)), _const_spec((1, d))],
        out_specs=[tile] * 7,
        out_shape=[jax.ShapeDtypeStruct((t, d), F32 if name == "lw" else act_dtype)
                   for name in ("r", "lw", "k", "v", "al", "be", "g")],
        scratch_shapes=[pltpu.VMEM((tm + 8, d), F32)],
        compiler_params=_cparams("arbitrary"),
        name="rw_proj",
    )(h2d, row(norm_g), mu, wr, wk, wv, wd, wa, wg, pad_r(w_up, nw), pad_r(a_up, na), pad_r(g_up, ng),
      row(w0), row(a0), row(k_k), row(k_a))
    r, lw, k, v, al, be, g = outs

    nt = seq // lt
    assert lc == R_HEAD_DIM, "pair tiles put the two heads' (lc, lc) matrices side by side on 128 lanes"
    nb, nmx = lc // R_SUB, R_HALF - 1
    nsel = 2 * nmx * LANES
    klane = jnp.arange(2 * LANES)[None, :, None] % LANES
    otile = jnp.arange(nsel)[None, None, :] // LANES
    sel = ((klane // lc == otile // nmx)
           & (klane % lc == jnp.arange(nb)[:, None, None] * R_SUB + otile % nmx)).astype(BF16)
    spec = pl.BlockSpec((lt, d), lambda b, i: (b * nt + i, 0))
    return pl.pallas_call(
        functools.partial(_rw_kernel, lt=lt, lc=lc),
        grid=(bsz, nt),
        in_specs=[spec] * 8 + [_const_spec((1, d))] * 3 + [_const_spec((d, d)), _const_spec(sel.shape)],
        out_specs=spec,
        out_shape=jax.ShapeDtypeStruct((t, d), F32),
        scratch_shapes=[pltpu.VMEM((d // LANES, LANES, LANES), F32), pltpu.VMEM((lt, d), F32),
                        pltpu.VMEM((d // LANES, lc, LANES), F32), pltpu.VMEM((d // LANES, lc, LANES), BF16),
                        pltpu.VMEM((d // LANES, lc, 2 * LANES), BF16),
                        pltpu.VMEM((R_HEADS, lc, LANES), F32), pltpu.VMEM((d // LANES * R_SUB, nsel), F32)],
        compiler_params=_cparams("parallel", "arbitrary"),
        name="rw_mix",
    )(r, lw, k, v, al, be, g, h2d, r_k.reshape(1, d), row(ln_g), row(ln_b), w_out.astype(BF16), sel)


def kernel(x, mlp_norm_g, mlp_w_up, mlp_w_down, final_norm_g, s5_norm_g, s5_a_re, s5_a_im, s5_log_dt, s5_b_re, s5_b_im, s5_c_re, s5_c_im, s5_d, s5_w_glu, s5_b_glu, ml_norm_g, ml_w_in, ml_conv_w, ml_conv_b, ml_b_i, ml_b_f, ml_head_g, ml_w_out, fox_norm_g, fox_w_in, fox_b_f, fox_w_out, rw_norm_g, rw_mu, rw_w_in, rw_w0, rw_w_up, rw_a0, rw_a_up, rw_g_up, rw_k_k, rw_k_a, rw_r_k, rw_ln_g, rw_ln_b, rw_w_out):
    bsz, seq, d = x.shape
    depth = mlp_w_up.shape[0]
    h = x.reshape(bsz * seq, d)
    for i in range(depth):
        kind, j = i % 4, i // 4
        if kind == 0:
            h = _s5_layer(h, bsz, seq, s5_norm_g[j], s5_a_re[j], s5_a_im[j], s5_log_dt[j], s5_b_re[j], s5_b_im[j],
                          s5_c_re[j], s5_c_im[j], s5_d[j], s5_w_glu[j], s5_b_glu[j])
        elif kind == 1:
            h = _mlstm_layer(h, bsz, seq, ml_norm_g[j], ml_w_in[j], ml_conv_w[j], ml_conv_b[j], ml_b_i[j],
                             ml_b_f[j], ml_head_g[j], ml_w_out[j])
        elif kind == 2:
            h = _fox_layer(h, bsz, seq, fox_norm_g[j], fox_w_in[j], fox_b_f[j], fox_w_out[j])
        else:
            h = _rwkv_layer(h, bsz, seq, rw_norm_g[j], rw_mu[j], rw_w_in[j], rw_w0[j], rw_w_up[j], rw_a0[j],
                            rw_a_up[j], rw_g_up[j], rw_k_k[j], rw_k_a[j], rw_r_k[j], rw_ln_g[j], rw_ln_b[j],
                            rw_w_out[j])
        h = _mlp_layer(h, mlp_norm_g[i], mlp_w_up[i], mlp_w_down[i],
                       final_g=final_norm_g if i == depth - 1 else None)
    return h.reshape(bsz, seq, d)
```

```python
import functools
import math

import jax
import jax.numpy as jnp
from jax import lax
from jax.experimental import pallas as pl
from jax.experimental.pallas import tpu as pltpu

F32 = jnp.float32
BF16 = jnp.bfloat16

NORM_EPS = 1e-6
NEG_BIG = -1e30
LANES = 128
VMEM_LIMIT = 56 * 1024 * 1024

M_HEADS = 8
M_QK_DIM = 64
M_V_DIM = 128
M_CONV = 4
M_NORM_EPS = 1e-6


def _cparams(*sem):
    return pltpu.CompilerParams(dimension_semantics=sem, vmem_limit_bytes=VMEM_LIMIT)


def _rms(x, g):
    return x * lax.rsqrt(jnp.mean(x * x, axis=-1, keepdims=True) + NORM_EPS) * g


def _dot(a, b):
    return jnp.dot(a, b, preferred_element_type=F32)


def _dot_nt(a, b):
    return lax.dot_general(a, b, (((1,), (1,)), ((), ())), preferred_element_type=F32)


def _split3(x):
    hi = x.astype(BF16)
    r = x - hi.astype(F32)
    mid = r.astype(BF16)
    lo = (r - mid.astype(F32)).astype(BF16)
    return hi, mid, lo


def _dot_exact_l(ones_mat, x):
    hi, mid, lo = _split3(x)
    return _dot(ones_mat, hi) + _dot(ones_mat, mid) + _dot(ones_mat, lo)


def _dot_exact_r(x, ones_mat):
    hi, mid, lo = _split3(x)
    return _dot(hi, ones_mat) + _dot(mid, ones_mat) + _dot(lo, ones_mat)


def _log_sigmoid(x):
    return jnp.minimum(x, 0.0) - jnp.log(1.0 + jnp.exp(-jnp.abs(x)))


def _sigmoid(x):
    return 1.0 / (1.0 + jnp.exp(-x))


def _const_spec(shape):
    nd = len(shape)
    return pl.BlockSpec(shape, lambda *_: (0,) * nd)


def _mlp_kernel(h_ref, g_ref, wup_ref, wdn_ref, fg_ref, out_ref, *, fc, final):
    x = h_ref[...]
    xn = _rms(x, g_ref[...]).astype(BF16)
    acc = x
    for c in range(wup_ref.shape[1] // fc):
        hid = jnp.maximum(_dot(xn, wup_ref[:, c * fc:(c + 1) * fc]), 0.0)
        acc = acc + _dot((hid * hid).astype(BF16), wdn_ref[c * fc:(c + 1) * fc, :])
    out_ref[...] = _rms(acc, fg_ref[...]) if final else acc


def _mlp_layer(h2d, norm_g, w_up, w_down, final_g=None, *, tm=1024, fc=512):
    t, d = h2d.shape
    ff = w_up.shape[1]
    fg = (norm_g if final_g is None else final_g).reshape(1, d)
    return pl.pallas_call(
        functools.partial(_mlp_kernel, fc=fc, final=final_g is not None),
        grid=(t // tm,),
        in_specs=[pl.BlockSpec((tm, d), lambda i: (i, 0)), _const_spec((1, d)),
                  _const_spec((d, ff)), _const_spec((ff, d)), _const_spec((1, d))],
        out_specs=pl.BlockSpec((tm, d), lambda i: (i, 0)),
        out_shape=jax.ShapeDtypeStruct((t, d), F32),
        compiler_params=_cparams("parallel"),
        name="mlp",
    )(h2d, norm_g.reshape(1, d), w_up.astype(BF16), w_down.astype(BF16), fg)


def _ml_proj_kernel(h_ref, g_ref, wqk_ref, wv_ref, wo_ref, wg_ref, wgt_ref,
                    qk_ref, v_ref, o_ref, gc_ref, gr_ref):
    xn = _rms(h_ref[...], g_ref[...]).astype(BF16)
    qk_ref[...] = _dot(xn, wqk_ref[...])
    v_ref[...] = _dot(xn, wv_ref[...]).astype(v_ref.dtype)
    o_ref[...] = _dot(xn, wo_ref[...])
    gc_ref[...] = _dot(xn, wg_ref[...])
    gr_ref[...] = _dot_nt(wgt_ref[...], xn)


def _ml_kernel(qk_ref, v_ref, o_ref, gc_ref, gr_ref, h_ref, cw_ref, cb_ref, bcol_ref, brow_ref,
               hg_ref, wout_ref, rep_ref, out_ref, ext_scr, c_scr, m_scr, hbuf, *, lt, lc):
    nh, dk, dv = M_HEADS, M_QK_DIM, M_V_DIM
    hd = nh * dk

    @pl.when(pl.program_id(1) == 0)
    def _():
        c_scr[...] = jnp.zeros_like(c_scr)
        m_scr[...] = jnp.zeros_like(m_scr)
        ext_scr[0:8, :] = jnp.zeros((8, 2 * hd), F32)

    ext_scr[8:8 + lt, :] = qk_ref[...]
    conv = cb_ref[...] + jnp.zeros((lt, 2 * hd), F32)
    for kk in range(M_CONV):
        conv = conv + cw_ref[kk:kk + 1, :] * ext_scr[8 - (M_CONV - 1) + kk:8 - (M_CONV - 1) + kk + lt, :]
    ext_scr[0:8, :] = ext_scr[lt:lt + 8, :]
    act = conv * _sigmoid(conv)

    lane = lax.broadcasted_iota(jnp.int32, (1, LANES), 1)
    r_i = lax.broadcasted_iota(jnp.int32, (lc, lc), 0)
    c_i = lax.broadcasted_iota(jnp.int32, (lc, lc), 1)
    causal = c_i <= r_i
    tri_l = causal.astype(BF16)
    tri_u = (r_i <= c_i).astype(BF16)
    ones_blk = jnp.ones((lc, LANES), BF16)
    row16 = lax.broadcasted_iota(jnp.int32, (2 * nh, 1), 0)

    def both(x):
        return jnp.concatenate([x, x], axis=1)

    local = [[None] * nh for _ in range(lt // lc)]
    for c in range(lt // lc):
        rows = slice(c * lc, (c + 1) * lc)
        gcb = gc_ref[rows, :] + bcol_ref[...]
        gate_c = jnp.where(lane < nh, gcb, jnp.where(lane < 2 * nh, _log_sigmoid(gcb), 0.0))
        cum_c = _dot_exact_l(tri_l, jnp.where(lane < nh, 0.0, gate_c))
        hi, mid, lo = _split3(jnp.where(lane < nh, gate_c, cum_c))
        rep = _dot(jnp.concatenate([hi, mid, lo], axis=1), rep_ref[...])
        grb = gr_ref[:, rows] + brow_ref[...]
        gate_r = jnp.where(row16 < nh, grb, _log_sigmoid(grb))
        cum_r = _dot_exact_r(jnp.where(row16 < nh, 0.0, gate_r), tri_u)

        for h in range(nh):
            p, e = h // 2, h % 2
            head_mask = (lane >= e * dk) & (lane < (e + 1) * dk)
            q_m = (jnp.where(head_mask, act[rows, p * LANES:(p + 1) * LANES], 0.0) * (dk ** -0.5)).astype(BF16)
            k_m = jnp.where(head_mask, act[rows, hd + p * LANES:hd + (p + 1) * LANES], 0.0)
            v_aug = jnp.concatenate([v_ref[rows, h * dv:(h + 1) * dv].astype(BF16), ones_blk], axis=1)
            li_c = rep[:, h * LANES:(h + 1) * LANES]
            b_c = rep[:, (nh + h) * LANES:(nh + h + 1) * LANES]
            b_row = cum_r[nh + h:nh + h + 1, :]
            li_row = gate_r[h:h + 1, :]
            d_log = jnp.where(causal, b_c[:, :lc] - b_row + li_row, NEG_BIG)
            m_loc = jnp.max(d_log, axis=-1, keepdims=True) + jnp.zeros((lc, LANES), F32)
            s_loc = _dot_nt(q_m, k_m.astype(BF16)) * jnp.exp(d_log - m_loc[:, :lc])
            r_loc = _dot(s_loc.astype(BF16), v_aug)
            b_last = b_c[lc - 1:lc, :]
            m_loc_last = m_loc[lc - 1:lc, :]
            w_state = jnp.exp(b_last - b_c + li_c - m_loc_last)
            kv_loc = _dot(jnp.transpose(k_m * w_state).astype(BF16), v_aug)
            local[c][h] = (q_m, b_c, m_loc, r_loc, kv_loc, b_last, m_loc_last)

    for c in range(lt // lc):
        rows = slice(c * lc, (c + 1) * lc)
        for h in range(nh):
            q_m, b_c, m_loc, r_loc, kv_loc, b_last, m_loc_last = local[c][h]
            m_prev = m_scr[h:h + 1, :]
            g = b_c + m_prev
            m_t = jnp.maximum(g, m_loc)
            c_aug = c_scr[h]
            r = both(jnp.exp(g - m_t)) * _dot(q_m, c_aug.astype(BF16)) + both(jnp.exp(m_loc - m_t)) * r_loc
            hh = r[:, :dv] / jnp.maximum(jnp.abs(r[:, dv:]), jnp.exp(-m_t))

            m_new = m_t[lc - 1:lc, :]
            decay = jnp.exp(b_last + m_prev - m_new)
            c_scr[h] = both(decay) * c_aug + both(jnp.exp(m_loc_last - m_new)) * kv_loc
            m_scr[h:h + 1, :] = m_new

            yc = hh - jnp.mean(hh, axis=-1, keepdims=True)
            yn = yc * lax.rsqrt(jnp.mean(yc * yc, axis=-1, keepdims=True) + M_NORM_EPS)
            cols = slice(h * dv, (h + 1) * dv)
            hbuf[rows, cols] = yn * hg_ref[:, cols] * _sigmoid(o_ref[rows, cols])

    out_ref[...] = h_ref[...] + _dot(hbuf[...].astype(BF16), wout_ref[...])


def _mlstm_layer(h2d, bsz, seq, norm_g, w_in, conv_w, conv_b, b_i, b_f, head_g, w_out, *, tm=512, lt=256, lc=128):
    t, d = h2d.shape
    nh, dk, dv = M_HEADS, M_QK_DIM, M_V_DIM
    o0 = 2 * nh * dk
    o1 = o0 + nh * dv
    o3 = o1 + 2 * nh
    wqk = w_in[:, :o0].astype(BF16)
    wv = w_in[:, o0:o1].astype(BF16)
    wo = w_in[:, o3:].astype(BF16)
    wg = jnp.pad(w_in[:, o1:o3], ((0, 0), (0, LANES - 2 * nh))).astype(BF16)
    wgt = jnp.transpose(w_in[:, o1:o3]).astype(BF16)
    g2 = norm_g.reshape(1, d)

    qk, v, o, gc, gr = pl.pallas_call(
        _ml_proj_kernel,
        grid=(t // tm,),
        in_specs=[pl.BlockSpec((tm, d), lambda i: (i, 0)), _const_spec((1, d)),
                  _const_spec(wqk.shape), _const_spec(wv.shape), _const_spec(wo.shape),
                  _const_spec(wg.shape), _const_spec(wgt.shape)],
        out_specs=[pl.BlockSpec((tm, o0), lambda i: (i, 0)), pl.BlockSpec((tm, nh * dv), lambda i: (i, 0)),
                   pl.BlockSpec((tm, nh * dv), lambda i: (i, 0)), pl.BlockSpec((tm, LANES), lambda i: (i, 0)),
                   pl.BlockSpec((2 * nh, tm), lambda i: (0, i))],
        out_shape=[jax.ShapeDtypeStruct((t, o0), F32), jax.ShapeDtypeStruct((t, nh * dv), BF16),
                   jax.ShapeDtypeStruct((t, nh * dv), F32), jax.ShapeDtypeStruct((t, LANES), F32),
                   jax.ShapeDtypeStruct((2 * nh, t), F32)],
        compiler_params=_cparams("parallel"),
        name="ml_proj",
    )(h2d, g2, wqk, wv, wo, wg, wgt)

    bcol = jnp.pad(jnp.concatenate([b_i, b_f]), (0, LANES - 2 * nh)).reshape(1, LANES)
    brow = jnp.concatenate([b_i, b_f]).reshape(2 * nh, 1)
    nt = seq // lt
    rep = (jnp.arange(3 * LANES)[:, None] % LANES == jnp.arange(2 * nh * LANES)[None, :] // LANES).astype(BF16)
    row_spec = lambda w: pl.BlockSpec((lt, w), lambda b, i: (b * nt + i, 0))
    return pl.pallas_call(
        functools.partial(_ml_kernel, lt=lt, lc=lc),
        grid=(bsz, nt),
        in_specs=[row_spec(o0), row_spec(nh * dv), row_spec(nh * dv), row_spec(LANES),
                  pl.BlockSpec((2 * nh, lt), lambda b, i: (0, b * nt + i)), row_spec(d),
                  _const_spec((M_CONV, o0)), _const_spec((1, o0)), _const_spec((1, LANES)),
                  _const_spec((2 * nh, 1)), _const_spec((1, nh * dv)), _const_spec((nh * dv, d)),
                  _const_spec(rep.shape)],
        out_specs=row_spec(d),
        out_shape=jax.ShapeDtypeStruct((t, d), F32),
        scratch_shapes=[pltpu.VMEM((lt + 8, o0), F32), pltpu.VMEM((nh, LANES, 2 * LANES), F32),
                        pltpu.VMEM((nh, LANES), F32), pltpu.VMEM((lt, nh * dv), F32)],
        compiler_params=_cparams("parallel", "arbitrary"),
        name="ml_mix",
    )(qk, v, o, gc, gr, h2d, conv_w, conv_b.reshape(1, o0), bcol, brow,
      head_g.reshape(1, nh * dv), w_out.astype(BF16), rep)


S5_GROUP_CH = 16
S5_STATE = 64
S5_CHUNK_GROUPS = 4
S5_CHUNK = S5_CHUNK_GROUPS * S5_STATE


def _s5_disc_kernel(ar_ref, ai_ref, ldt_ref, arx_ref, aix_ref, br_ref, bi_ref,
                    abr_ref, abi_ref, bbr_ref, bbi_ref):
    dt = jnp.exp(ldt_ref[...])

    def abar(ar, ai):
        mag = jnp.exp(dt * ar)
        return mag * jnp.cos(dt * ai), mag * jnp.sin(dt * ai)

    abr_ref[...], abi_ref[...] = abar(ar_ref[...], ai_ref[...])
    ar, ai = arx_ref[...], aix_ref[...]
    xr, xi = abar(ar, ai)
    inv = 1.0 / (ar * ar + ai * ai)
    zr, zi = xr - 1.0, xi
    coef_re = (zr * ar + zi * ai) * inv
    coef_im = (zi * ar - zr * ai) * inv
    br, bi = br_ref[...], bi_ref[...]
    bbr_ref[...] = coef_re * br - coef_im * bi
    bbi_ref[...] = coef_re * bi + coef_im * br


def _s5_kernel(h_ref, g_ref, abr_ref, abi_ref, bsl_ref, csl_ref, dsk_ref, wglu_ref, bglu_ref,
               out_ref, sr_scr, si_scr, *, tt, bsz):
    d = h_ref.shape[1]
    nchunk = sr_scr.shape[0]
    per_slab = nchunk // (d // S5_CHUNK)

    @pl.when(pl.program_id(0) == 0)
    def _():
        sr_scr[...] = jnp.zeros_like(sr_scr)
        si_scr[...] = jnp.zeros_like(si_scr)

    x = h_ref[...]
    u = _rms(x, g_ref[...])
    ub = u.astype(BF16)
    ys = []
    for slab in range(d // S5_CHUNK):
        bu = _dot(ub[:, slab * S5_CHUNK:(slab + 1) * S5_CHUNK], bsl_ref[slab])
        states = []
        for j in range(per_slab):
            c = slab * per_slab + j
            bu_re = bu[:, 2 * j * S5_CHUNK:(2 * j + 1) * S5_CHUNK]
            bu_im = bu[:, (2 * j + 1) * S5_CHUNK:(2 * j + 2) * S5_CHUNK]
            ar, ai = abr_ref[c], abi_ref[c]
            sr, si = sr_scr[c], si_scr[c]
            s_re, s_im = [], []
            for t in range(tt):
                rows = slice(t * bsz, (t + 1) * bsz)
                sr, si = ar * sr - ai * si + bu_re[rows], ar * si + ai * sr + bu_im[rows]
                s_re.append(sr)
                s_im.append(si)
            sr_scr[c] = sr
            si_scr[c] = si
            states += [jnp.concatenate(s_re, axis=0).astype(BF16), jnp.concatenate(s_im, axis=0).astype(BF16)]
        ys.append(_dot(jnp.concatenate(states, axis=1), csl_ref[slab]))
    y = jnp.concatenate(ys, axis=1) + dsk_ref[...] * u
    gelu = 0.5 * y * (1.0 + jnp.tanh(math.sqrt(2.0 / math.pi) * (y + 0.044715 * (y * y * y))))
    z = _dot(gelu.astype(BF16), wglu_ref[...]) + bglu_ref[...]
    out_ref[...] = x + z[:, :d] * _sigmoid(z[:, d:])


def _s5_layer(h2d, bsz, seq, norm_g, a_re, a_im, log_dt, b_re, b_im, c_re, c_im, d_skip, w_glu, b_glu, *, tt=16):
    t, d = h2d.shape
    ng, ns, hg = a_re.shape[0], S5_STATE, S5_GROUP_CH
    cg = S5_CHUNK_GROUPS
    nchunk = ng // cg
    per_slab = S5_CHUNK // (cg * hg)

    rep = lambda a: jnp.repeat(a, hg, axis=1)
    abr, abi, bbr, bbi = pl.pallas_call(
        _s5_disc_kernel,
        out_shape=[jax.ShapeDtypeStruct((ng, ns), F32)] * 2 + [jax.ShapeDtypeStruct((ng, ns * hg), F32)] * 2,
        name="s5_disc",
    )(a_re, a_im, log_dt.reshape(ng, 1), rep(a_re), rep(a_im), b_re.reshape(ng, ns * hg), b_im.reshape(ng, ns * hg))

    eye = jnp.eye(cg, dtype=F32)
    place = jax.nn.one_hot(jnp.arange(nchunk) % per_slab, per_slab, dtype=F32)

    def b_blocks(bb):
        m = bb.reshape(nchunk, cg, ns, hg)
        blk = jnp.einsum('cgph,gq->cghqp', m, eye).reshape(nchunk, cg * hg, cg * ns)
        return jnp.einsum('crs,cj->cjrs', blk, place).reshape(nchunk, per_slab * cg * hg, cg * ns).astype(BF16)

    def c_blocks(cc):
        m = cc.reshape(nchunk, cg, hg, ns)
        blk = jnp.einsum('cghp,gq->cgpqh', m, eye).reshape(nchunk, cg * ns, cg * hg)
        return jnp.einsum('csr,cj->csjr', blk, place).reshape(nchunk, cg * ns, per_slab * cg * hg).astype(BF16)

    rows = tt * bsz
    nslab = nchunk // per_slab
    bsl = jnp.stack([b_blocks(bbr), b_blocks(bbi)], axis=2).reshape(nslab, per_slab, S5_CHUNK, 2 * S5_CHUNK)
    bsl = bsl.transpose(0, 2, 1, 3).reshape(nslab, S5_CHUNK, per_slab * 2 * S5_CHUNK)
    csl = jnp.stack([c_blocks(c_re), -c_blocks(c_im)], axis=1).reshape(nslab, per_slab * 2 * S5_CHUNK, S5_CHUNK)
    ht = h2d.reshape(bsz, seq, d).transpose(1, 0, 2).reshape(t, d)
    out = pl.pallas_call(
        functools.partial(_s5_kernel, tt=tt, bsz=bsz),
        grid=(seq // tt,),
        in_specs=[pl.BlockSpec((rows, d), lambda i: (i, 0)), _const_spec((1, d)),
                  _const_spec((nchunk, 1, S5_CHUNK)), _const_spec((nchunk, 1, S5_CHUNK)),
                  _const_spec(bsl.shape), _const_spec(csl.shape),
                  _const_spec((1, d)), _const_spec((d, 2 * d)), _const_spec((1, 2 * d))],
        out_specs=pl.BlockSpec((rows, d), lambda i: (i, 0)),
        out_shape=jax.ShapeDtypeStruct((t, d), F32),
        scratch_shapes=[pltpu.VMEM((nchunk, bsz, S5_CHUNK), F32), pltpu.VMEM((nchunk, bsz, S5_CHUNK), F32)],
        compiler_params=_cparams("arbitrary"),
        name="s5_mix",
    )(ht, norm_g.reshape(1, d), abr.reshape(nchunk, 1, S5_CHUNK), abi.reshape(nchunk, 1, S5_CHUNK),
      bsl, csl, d_skip.reshape(1, d), w_glu.astype(BF16), b_glu.reshape(1, 2 * d))
    return out.reshape(seq, bsz, d).transpose(1, 0, 2).reshape(t, d)


F_HEADS = 16
F_HEAD_DIM = 64
F_DIAG_ROWS = 128
LOG2E = math.log2(math.e)


def _fox_proj_kernel(h_ref, g_ref, wq_ref, wk_ref, wv_ref, wo_ref, wgt_ref,
                     q_ref, k_ref, v_ref, o_ref, gr_ref):
    xn = _rms(h_ref[...], g_ref[...]).astype(BF16)
    q_ref[...] = (_dot(xn, wq_ref[...]) * (F_HEAD_DIM ** -0.5 * LOG2E)).astype(BF16)
    k_ref[...] = _dot(xn, wk_ref[...]).astype(BF16)
    v_ref[...] = _dot(xn, wv_ref[...]).astype(BF16)
    o_ref[...] = _dot(xn, wo_ref[...])
    gr_ref[...] = _dot_nt(wgt_ref[...], xn)


def _fox_kernel(q_ref, k_ref, v_ref, o_ref, gr_ref, h_ref, brow_ref, wout_ref,
                out_ref, cr_scr, obuf, m_scr, l_scr, acc_scr, *, tq, tk, seq):
    nh, dh = F_HEADS, F_HEAD_DIM
    i = pl.program_id(1)
    lane = lax.broadcasted_iota(jnp.int32, (1, LANES), 1)
    hmask = [lane < dh, lane >= dh]

    @pl.when(i == 0)
    def _():
        r_i = lax.broadcasted_iota(jnp.int32, (LANES, LANES), 0)
        c_i = lax.broadcasted_iota(jnp.int32, (LANES, LANES), 1)
        tri_u = (r_i <= c_i).astype(BF16)
        carry_r = jnp.zeros((nh, 1), F32)
        per = tk // LANES
        for blk in range(seq // LANES):
            rows = slice(blk * LANES, (blk + 1) * LANES)
            cr = _dot_exact_r(_log_sigmoid(gr_ref[:, rows] + brow_ref[...]), tri_u) + carry_r
            cr_scr[blk // per, :, (blk % per) * LANES:(blk % per + 1) * LANES] = cr * LOG2E
            carry_r = cr[:, LANES - 1:LANES]

    m_scr[...] = jnp.full(m_scr.shape, NEG_BIG, F32)
    l_scr[...] = jnp.zeros_like(l_scr)
    acc_scr[...] = jnp.zeros_like(acc_scr)

    def step(j, qrow0, nrows, width, masked):
        k0 = pl.multiple_of(j * tk, tk)
        crow = cr_scr[j][:, :width]
        qrows = slice(qrow0, qrow0 + nrows)
        if masked:
            causal = (lax.broadcasted_iota(jnp.int32, (nrows, width), 1)
                      <= lax.broadcasted_iota(jnp.int32, (nrows, width), 0) + qrow0)
        for p in range(nh // 2):
            cols = slice(p * LANES, (p + 1) * LANES)
            kp = k_ref[pl.ds(k0, width), cols]
            vp = v_ref[pl.ds(k0, width), cols]
            qp = q_ref[qrows, cols]
            q2 = jnp.concatenate([jnp.where(hmask[e], qp, jnp.zeros_like(qp)) for e in range(2)], axis=0)
            s2 = _dot_nt(q2, kp)
            dm, ls, pms = [], [], []
            for e in range(2):
                h = 2 * p + e
                s = s2[e * nrows:(e + 1) * nrows] - crow[h:h + 1, :]
                if masked:
                    s = jnp.where(causal, s, NEG_BIG)
                m_prev = m_scr[h, qrows, :]
                m_next = jnp.maximum(m_prev, jnp.max(s, axis=-1, keepdims=True))
                m_scr[h, qrows, :] = m_next
                pm = jnp.exp2(s - jnp.concatenate([m_next] * (width // LANES), axis=1))
                dm.append(m_prev - m_next)
                ls.append(jnp.sum(pm, axis=-1, keepdims=True))
                pms.append(pm.astype(BF16))
            pv = _dot(jnp.concatenate(pms, axis=0), vp)
            alpha = jnp.exp2(jnp.where(hmask[0], dm[0], dm[1]))
            l_scr[p, qrows, :] = alpha * l_scr[p, qrows, :] + jnp.where(hmask[0], ls[0], ls[1])
            acc_scr[p, qrows, :] = alpha * acc_scr[p, qrows, :] + jnp.where(hmask[0], pv[:nrows], pv[nrows:])

    def body(j, carry):
        step(j, 0, tq, tk, False)
        return carry

    lax.fori_loop(0, i, body, 0)
    for r in range(tq // F_DIAG_ROWS):
        step(i, r * F_DIAG_ROWS, F_DIAG_ROWS, (r + 1) * F_DIAG_ROWS, True)
    for p in range(nh // 2):
        cols = slice(p * LANES, (p + 1) * LANES)
        obuf[:, cols] = acc_scr[p] / l_scr[p] * _sigmoid(o_ref[:, cols])

    out_ref[...] = h_ref[...] + _dot(obuf[...].astype(BF16), wout_ref[...])


def _fox_layer(h2d, bsz, seq, norm_g, w_in, b_f, w_out, *, tm=512, tq=256):
    t, d = h2d.shape
    nh, dh = F_HEADS, F_HEAD_DIM
    hd = nh * dh
    wq, wk, wv = (w_in[:, j * hd:(j + 1) * hd].astype(BF16) for j in range(3))
    wf = w_in[:, 3 * hd:3 * hd + nh]
    wo = w_in[:, 3 * hd + nh:].astype(BF16)
    wgt = jnp.transpose(wf).astype(BF16)
    tile = lambda w: pl.BlockSpec((tm, w), lambda i: (i, 0))
    q, k, v, o, gr = pl.pallas_call(
        _fox_proj_kernel,
        grid=(t // tm,),
        in_specs=[tile(d), _const_spec((1, d)), _const_spec(wq.shape), _const_spec(wk.shape),
                  _const_spec(wv.shape), _const_spec(wo.shape), _const_spec(wgt.shape)],
        out_specs=[tile(hd), tile(hd), tile(hd), tile(hd), pl.BlockSpec((nh, tm), lambda i: (0, i))],
        out_shape=[jax.ShapeDtypeStruct((t, hd), BF16)] * 3 + [jax.ShapeDtypeStruct((t, hd), F32),
                   jax.ShapeDtypeStruct((nh, t), F32)],
        compiler_params=_cparams("parallel"),
        name="fox_proj",
    )(h2d, norm_g.reshape(1, d), wq, wk, wv, wo, wgt)

    nt = seq // tq
    q_spec = lambda w: pl.BlockSpec((tq, w), lambda b, i: (b * nt + i, 0))
    seq_spec = lambda w: pl.BlockSpec((seq, w), lambda b, i: (b, 0))
    return pl.pallas_call(
        functools.partial(_fox_kernel, tq=tq, tk=tq, seq=seq),
        grid=(bsz, nt),
        in_specs=[q_spec(hd), seq_spec(hd), seq_spec(hd), q_spec(hd),
                  pl.BlockSpec((nh, seq), lambda b, i: (0, b)), q_spec(d),
                  _const_spec((nh, 1)), _const_spec((hd, d))],
        out_specs=q_spec(d),
        out_shape=jax.ShapeDtypeStruct((t, d), F32),
        scratch_shapes=[pltpu.VMEM((nt, nh, tq), F32), pltpu.VMEM((tq, hd), F32),
                        pltpu.VMEM((nh, tq, LANES), F32), pltpu.VMEM((nh // 2, tq, LANES), F32),
                        pltpu.VMEM((nh // 2, tq, LANES), F32)],
        compiler_params=_cparams("parallel", "arbitrary"),
        name="fox_attn",
    )(q, k, v, o, gr, h2d, b_f.reshape(nh, 1), w_out.astype(BF16))


R_HEADS = 16
R_HEAD_DIM = 64
R_LN_EPS = 64e-5
R_SUB = 16
R_HALF = R_SUB // 2


def _softplus(z):
    return jnp.maximum(z, 0.0) + jnp.log(1.0 + jnp.exp(-jnp.abs(z)))


def _head_sum(x, lane):
    lo = lane < R_HEAD_DIM
    s0 = jnp.sum(jnp.where(lo, x, 0.0), axis=-1, keepdims=True)
    s1 = jnp.sum(jnp.where(lo, 0.0, x), axis=-1, keepdims=True)
    return jnp.where(lo, s0, s1)


def _rw_proj_kernel(h_ref, g_ref, mu_ref, wr_ref, wk_ref, wv_ref, wd_ref, wa_ref, wg_ref, wup_ref, aup_ref,
                    gup_ref, w0_ref, a0_ref, kk_ref, ka_ref,
                    r_ref, lw_ref, k_ref, v_ref, al_ref, be_ref, g_out_ref, ext_scr, *, tm, seq):
    d = h_ref.shape[1]
    i = pl.program_id(0)
    xn = _rms(h_ref[...], g_ref[...])
    @pl.when((i * tm) % seq == 0)
    def _():
        ext_scr[7:8, :] = jnp.zeros((1, d), F32)

    @pl.when((i * tm) % seq != 0)
    def _():
        ext_scr[7:8, :] = ext_scr[7 + tm:8 + tm, :]

    ext_scr[8:8 + tm, :] = xn
    xb = xn.astype(BF16)
    dxb = (ext_scr[7:7 + tm, :] - xn).astype(BF16)
    mub = mu_ref[...].astype(BF16)
    mix = lambda j: xb + dxb * mub[j:j + 1, :]
    r_ref[...] = _dot(mix(0), wr_ref[...]).astype(r_ref.dtype)
    w_in = w0_ref[...] + _dot(jnp.tanh(_dot(mix(1), wd_ref[...])).astype(BF16), wup_ref[...])
    lw_ref[...] = -math.exp(-0.5) * _sigmoid(w_in)
    k = _dot(mix(2), wk_ref[...])
    v_ref[...] = _dot(mix(3), wv_ref[...]).astype(v_ref.dtype)
    a = _sigmoid(a0_ref[...] + _dot(_dot(mix(4), wa_ref[...]).astype(BF16), aup_ref[...]))
    g_out_ref[...] = _dot(_sigmoid(_dot(mix(5), wg_ref[...])).astype(BF16), gup_ref[...]).astype(g_out_ref.dtype)
    k_ref[...] = (k * (1.0 + (a - 1.0) * ka_ref[...])).astype(k_ref.dtype)
    lane = lax.broadcasted_iota(jnp.int32, (1, LANES), 1)
    for p in range(d // LANES):
        cols = slice(p * LANES, (p + 1) * LANES)
        kk = k[:, cols] * kk_ref[:, cols]
        kk = kk * jnp.minimum(lax.rsqrt(_head_sum(kk * kk, lane)), 1e12)
        al_ref[:, cols] = kk.astype(al_ref.dtype)
        be_ref[:, cols] = (kk * a[:, cols]).astype(be_ref.dtype)


def _head_sum_mxu(x, ones_bd):
    hi = x.astype(BF16)
    lo = (x - hi.astype(F32)).astype(BF16)
    return _dot(jnp.concatenate([hi, lo], axis=1), ones_bd)


def _rw_kernel(r_ref, lw_ref, k_ref, v_ref, al_ref, be_ref, g_ref, h_ref, rk_ref, lng_ref, lnb_ref, wout_ref,
               sel_ref, out_ref, ht_scr, ybuf, aab_scr, aoff_scr, arkb_scr, x_scr, coef_scr, *, lt, lc):
    d = h_ref.shape[1]
    dh = R_HEAD_DIM
    npair = d // LANES
    nb = lc // R_SUB

    @pl.when(pl.program_id(1) == 0)
    def _():
        ht_scr[...] = jnp.zeros_like(ht_scr)

    lane = lax.broadcasted_iota(jnp.int32, (1, LANES), 1)
    hmask = [lane < dh, lane >= dh]
    lane2 = lax.broadcasted_iota(jnp.int32, (1, 2 * LANES), 1) % LANES
    hmask2 = [lane2 < dh, lane2 >= dh]
    r_i = lax.broadcasted_iota(jnp.int32, (lc, lc), 0)
    c_i = lax.broadcasted_iota(jnp.int32, (lc, lc), 1)
    tri_l = (c_i <= r_i).astype(BF16)
    r_p2 = lax.broadcasted_iota(jnp.int32, (lc, LANES), 0)
    c_p2 = lax.broadcasted_iota(jnp.int32, (lc, LANES), 1) % lc
    strict2 = c_p2 < r_p2
    incl2 = c_p2 <= r_p2
    below2 = (c_p2 // R_SUB) < (r_p2 // R_SUB)
    rr = lax.broadcasted_iota(jnp.int32, (LANES, LANES), 0)
    cc = lax.broadcasted_iota(jnp.int32, (LANES, LANES), 1)
    bdiag = (rr < dh) == (cc < dh)

    def chunk(c, carry):
        rows = slice(c * lc, (c + 1) * lc)
        lw = lw_ref[rows, :]
        cum = _dot_exact_l(tri_l, lw)
        cum_last = cum[lc - 1:lc, :]
        e_pos = jnp.exp(cum)
        e_neg = jnp.exp(-cum)
        e_rem = jnp.exp(cum_last - cum)
        r_t = r_ref[rows, :] * e_pos
        a_t = al_ref[rows, :] * jnp.exp(cum - lw)
        k_all = k_ref[rows, :]
        b_all = be_ref[rows, :]
        k_t, b_t = k_all * e_neg, b_all * e_neg
        k_h, b_h = k_all * e_rem, b_all * e_rem
        gam = jnp.exp(cum_last)

        nheads = 2 * npair
        vps = [v_ref[rows, p * LANES:(p + 1) * LANES].astype(BF16) for p in range(npair)]
        vp2s = [jnp.concatenate([vp, vp], axis=0) for vp in vps]

        scs = []
        for p in range(npair):
            cols = slice(p * LANES, (p + 1) * LANES)
            lhs = jnp.concatenate([a_t[:, cols], r_t[:, cols]], axis=0).astype(BF16)
            b_p, k_p = b_t[:, cols], k_t[:, cols]
            rhs = jnp.concatenate([jnp.where(hmask[0], b_p, 0.0), jnp.where(hmask[1], b_p, 0.0),
                                   jnp.where(hmask[0], k_p, 0.0), jnp.where(hmask[1], k_p, 0.0)], axis=0)
            scs.append(_dot_nt(lhs, rhs.astype(BF16)))
        aaks = []
        for p in range(npair):
            sc = scs[p]
            top_b, top_k = sc[:lc, :LANES], sc[:lc, LANES:]
            bot_b, bot_k = sc[lc:, :LANES], sc[lc:, LANES:]
            aab_scr[p] = jnp.where(strict2, top_b, 0.0)
            aoff_scr[p] = jnp.where(below2, top_b, 0.0).astype(BF16)
            arkb_scr[p] = jnp.concatenate([jnp.where(incl2, bot_k, 0.0), jnp.where(incl2, bot_b, 0.0)],
                                          axis=1).astype(BF16)
            a_ak = jnp.where(strict2, top_k, 0.0)
            aaks.append(jnp.concatenate([jnp.where(hmask[0], a_ak, 0.0), jnp.where(hmask[1], a_ak, 0.0)],
                                        axis=0).astype(BF16))
        for p in range(npair):
            cols = slice(p * LANES, (p + 1) * LANES)
            av = _dot(aaks[p], vp2s[p])
            for e in range(2):
                av_e = jnp.where(hmask[e], av[e * lc:(e + 1) * lc], 0.0)
                x_scr[2 * p + e] = jnp.where(hmask[e], a_t[:, cols], 0.0) + pltpu.roll(av_e, dh, axis=1)

        for bi in range(nb):
            brows = slice(bi * R_SUB, (bi + 1) * R_SUB)
            a_blk = aab_scr[:, brows, :].reshape(npair * R_SUB, LANES)
            a_hi = a_blk.astype(BF16)
            a_lo = (a_blk - a_hi.astype(F32)).astype(BF16)
            coef_scr[...] = _dot(jnp.concatenate([a_hi, a_lo], axis=1), sel_ref[bi])
            xs = [x_scr[h, brows, :] for h in range(nheads)]
            if bi > 0:
                for p in range(npair):
                    slab = aoff_scr[p, brows, :]
                    lhs = jnp.concatenate([jnp.where(hmask[0], slab, jnp.zeros_like(slab)),
                                           jnp.where(hmask[1], slab, jnp.zeros_like(slab))], axis=0)
                    xst = x_scr[2 * p:2 * p + 2].reshape(2 * lc, LANES).astype(BF16)
                    corr = _dot(lhs, xst)
                    for e in range(2):
                        xs[2 * p + e] = xs[2 * p + e] - corr[e * R_SUB:(e + 1) * R_SUB]
            x_lo = [x[:R_HALF] for x in xs]
            x_hi = [x[R_HALF:] for x in xs]
            for s in range(R_HALF - 1):
                for h in range(nheads):
                    p, e = h // 2, h % 2
                    tile = e * (R_HALF - 1) + s
                    coef = coef_scr[p * R_SUB:(p + 1) * R_SUB, tile * LANES:(tile + 1) * LANES]
                    row = x_lo[h][s:s + 1, :]
                    x_lo[h] = x_lo[h] - coef[:R_HALF] * row
                    x_hi[h] = x_hi[h] - coef[R_HALF:] * row
            for s in range(R_HALF - 1, R_SUB - 1):
                for h in range(nheads):
                    p, e = h // 2, h % 2
                    col = e * lc + bi * R_SUB + s
                    coef = aab_scr[p, bi * R_SUB + R_HALF:(bi + 1) * R_SUB, col:col + 1]
                    row = x_lo[h][s:s + 1, :] if s < R_HALF else x_hi[h][s - R_HALF:s - R_HALF + 1, :]
                    x_hi[h] = x_hi[h] - coef * row
            for h in range(nheads):
                x_scr[h, bi * R_SUB:bi * R_SUB + R_HALF, :] = x_lo[h]
                x_scr[h, bi * R_SUB + R_HALF:(bi + 1) * R_SUB, :] = x_hi[h]

        hts, ubs, yhs = [], [], []
        for p in range(npair):
            cols = slice(p * LANES, (p + 1) * LANES)
            x0, x1 = x_scr[2 * p], x_scr[2 * p + 1]
            w_pair = jnp.where(hmask[0], x0, x1)
            uv_pair = pltpu.roll(jnp.where(hmask[0], x1, x0), dh, axis=1)
            ht = ht_scr[p]
            hts.append(ht)
            wr = jnp.concatenate([w_pair, r_t[:, cols]], axis=0).astype(BF16)
            wh = _dot_nt(wr, ht.astype(BF16))
            ubs.append((wh[:lc] + uv_pair).astype(BF16))
            yhs.append(wh[lc:])
        for p in range(npair):
            cols = slice(p * LANES, (p + 1) * LANES)
            arkb = arkb_scr[p]
            lhs = jnp.concatenate([jnp.where(hmask2[e], arkb, jnp.zeros_like(arkb)) for e in range(2)], axis=0)
            rhs = jnp.concatenate([vp2s[p], -ubs[p], -ubs[p]], axis=0)
            intra = _dot(lhs, rhs)
            ybuf[rows, cols] = yhs[p] + jnp.where(hmask[0], intra[:lc], intra[lc:])
        for p in range(npair):
            cols = slice(p * LANES, (p + 1) * LANES)
            vu = jnp.concatenate([vps[p], -ubs[p]], axis=0)
            kb = jnp.concatenate([k_h[:, cols], b_h[:, cols]], axis=0).astype(BF16)
            upd = lax.dot_general(vu, kb, (((0,), (0,)), ((), ())), preferred_element_type=F32)
            ht_scr[p] = hts[p] * gam[:, cols] + jnp.where(bdiag, upd, 0.0)
        return carry

    for c in range(lt // lc):
        chunk(c, 0)

    rr2 = lax.broadcasted_iota(jnp.int32, (2 * LANES, LANES), 0)
    cc2 = lax.broadcasted_iota(jnp.int32, (2 * LANES, LANES), 1)
    ones_bd = (((rr2 % LANES) < dh) == (cc2 < dh)).astype(BF16)
    for p in range(npair):
        cols = slice(p * LANES, (p + 1) * LANES)
        y = ybuf[:, cols]
        r, k, v = r_ref[:, cols], k_ref[:, cols], v_ref[:, cols]
        bonus = _head_sum(r * k * rk_ref[:, cols], lane) * v
        yc = y - _head_sum_mxu(y, ones_bd) * (1.0 / dh)
        yn = yc * lax.rsqrt(_head_sum_mxu(yc * yc, ones_bd) * (1.0 / dh) + R_LN_EPS)
        ybuf[:, cols] = (yn * lng_ref[:, cols] + lnb_ref[:, cols] + bonus) * g_ref[:, cols]
    out_ref[...] = h_ref[...] + _dot(ybuf[...].astype(BF16), wout_ref[...])


def _rwkv_layer(h2d, bsz, seq, norm_g, mu, w_in, w0, w_up, a0, a_up, g_up, k_k, k_a, r_k, ln_g, ln_b, w_out,
                *, tm=512, lt=512, lc=64, act_dtype=F32):
    t, d = h2d.shape
    c0 = 3 * d
    c1 = c0 + w_up.shape[0]
    c2 = c1 + a_up.shape[0]
    pad_c = lambda w, n: jnp.pad(w, ((0, 0), (0, n - w.shape[1]))).astype(BF16)
    pad_r = lambda w, n: jnp.pad(w, ((0, n - w.shape[0]), (0, 0))).astype(BF16)
    up = lambda n: -(-n // LANES) * LANES
    nw, na, ng = up(c1 - c0), up(c2 - c1), up(w_in.shape[1] - c2)
    wr, wk, wv = (w_in[:, j * d:(j + 1) * d].astype(BF16) for j in range(3))
    wd, wa, wg = pad_c(w_in[:, c0:c1], nw), pad_c(w_in[:, c1:c2], na), pad_c(w_in[:, c2:], ng)
    row = lambda a: a.reshape(1, d)
    tile = pl.BlockSpec((tm, d), lambda i: (i, 0))
    outs = pl.pallas_call(
        functools.partial(_rw_proj_kernel, tm=tm, seq=seq),
        grid=(t // tm,),
        in_specs=[tile, _const_spec((1, d)), _const_spec((6, d)), _const_spec((d, d)), _const_spec((d, d)),
                  _const_spec((d, d)), _const_spec((d, nw)), _const_spec((d, na)), _const_spec((d, ng)),
                  _const_spec((nw, d)), _const_spec((na, d)), _const_spec((ng, d)),
                  _const_spec((1, d)), _const_spec((1, d)), _const_spec((1, d)), _const_spec((1, d))],
        out_specs=[tile] * 7,
        out_shape=[jax.ShapeDtypeStruct((t, d), F32 if name == "lw" else act_dtype)
                   for name in ("r", "lw", "k", "v", "al", "be", "g")],
        scratch_shapes=[pltpu.VMEM((tm + 8, d), F32)],
        compiler_params=_cparams("arbitrary"),
        name="rw_proj",
    )(h2d, row(norm_g), mu, wr, wk, wv, wd, wa, wg, pad_r(w_up, nw), pad_r(a_up, na), pad_r(g_up, ng),
      row(w0), row(a0), row(k_k), row(k_a))
    r, lw, k, v, al, be, g = outs

    nt = seq // lt
    assert lc == R_HEAD_DIM, "pair tiles put the two heads' (lc, lc) matrices side by side on 128 lanes"
    nb, nmx = lc // R_SUB, R_HALF - 1
    nsel = 2 * nmx * LANES
    klane = jnp.arange(2 * LANES)[None, :, None] % LANES
    otile = jnp.arange(nsel)[None, None, :] // LANES
    sel = ((klane // lc == otile // nmx)
           & (klane % lc == jnp.arange(nb)[:, None, None] * R_SUB + otile % nmx)).astype(BF16)
    spec = pl.BlockSpec((lt, d), lambda b, i: (b * nt + i, 0))
    return pl.pallas_call(
        functools.partial(_rw_kernel, lt=lt, lc=lc),
        grid=(bsz, nt),
        in_specs=[spec] * 8 + [_const_spec((1, d))] * 3 + [_const_spec((d, d)), _const_spec(sel.shape)],
        out_specs=spec,
        out_shape=jax.ShapeDtypeStruct((t, d), F32),
        scratch_shapes=[pltpu.VMEM((d // LANES, LANES, LANES), F32), pltpu.VMEM((lt, d), F32),
                        pltpu.VMEM((d // LANES, lc, LANES), F32), pltpu.VMEM((d // LANES, lc, LANES), BF16),
                        pltpu.VMEM((d // LANES, lc, 2 * LANES), BF16),
                        pltpu.VMEM((R_HEADS, lc, LANES), F32), pltpu.VMEM((d // LANES * R_SUB, nsel), F32)],
        compiler_params=_cparams("parallel", "arbitrary"),
        name="rw_mix",
    )(r, lw, k, v, al, be, g, h2d, r_k.reshape(1, d), row(ln_g), row(ln_b), w_out.astype(BF16), sel)


def kernel(x, mlp_norm_g, mlp_w_up, mlp_w_down, final_norm_g, s5_norm_g, s5_a_re, s5_a_im, s5_log_dt, s5_b_re, s5_b_im, s5_c_re, s5_c_im, s5_d, s5_w_glu, s5_b_glu, ml_norm_g, ml_w_in, ml_conv_w, ml_conv_b, ml_b_i, ml_b_f, ml_head_g, ml_w_out, fox_norm_g, fox_w_in, fox_b_f, fox_w_out, rw_norm_g, rw_mu, rw_w_in, rw_w0, rw_w_up, rw_a0, rw_a_up, rw_g_up, rw_k_k, rw_k_a, rw_r_k, rw_ln_g, rw_ln_b, rw_w_out):
    bsz, seq, d = x.shape
    depth = mlp_w_up.shape[0]
    h = x.reshape(bsz * seq, d)
    for i in range(depth):
        kind, j = i % 4, i // 4
        if kind == 0:
            h = _s5_layer(h, bsz, seq, s5_norm_g[j], s5_a_re[j], s5_a_im[j], s5_log_dt[j], s5_b_re[j], s5_b_im[j],
                          s5_c_re[j], s5_c_im[j], s5_d[j], s5_w_glu[j], s5_b_glu[j])
        elif kind == 1:
            h = _mlstm_layer(h, bsz, seq, ml_norm_g[j], ml_w_in[j], ml_conv_w[j], ml_conv_b[j], ml_b_i[j],
                             ml_b_f[j], ml_head_g[j], ml_w_out[j])
        elif kind == 2:
            h = _fox_layer(h, bsz, seq, fox_norm_g[j], fox_w_in[j], fox_b_f[j], fox_w_out[j])
        else:
            h = _rwkv_layer(h, bsz, seq, rw_norm_g[j], rw_mu[j], rw_w_in[j], rw_w0[j], rw_w_up[j], rw_a0[j],
                            rw_a_up[j], rw_g_up[j], rw_k_k[j], rw_k_a[j], rw_r_k[j], rw_ln_g[j], rw_ln_b[j],
                            rw_w_out[j])
        h = _mlp_layer(h, mlp_norm_g[i], mlp_w_up[i], mlp_w_down[i],
                       final_g=final_norm_g if i == depth - 1 else None)
    return h.reshape(bsz, seq, d)
```

```python
import functools
import math

import jax
import jax.numpy as jnp
from jax import lax
from jax.experimental import pallas as pl
from jax.experimental.pallas import tpu as pltpu

F32 = jnp.float32
BF16 = jnp.bfloat16

NORM_EPS = 1e-6
NEG_BIG = -1e30
LANES = 128
VMEM_LIMIT = 56 * 1024 * 1024

M_HEADS = 8
M_QK_DIM = 64
M_V_DIM = 128
M_CONV = 4
M_NORM_EPS = 1e-6


def _cparams(*sem):
    return pltpu.CompilerParams(dimension_semantics=sem, vmem_limit_bytes=VMEM_LIMIT)


def _rms(x, g):
    return x * lax.rsqrt(jnp.mean(x * x, axis=-1, keepdims=True) + NORM_EPS) * g


def _dot(a, b):
    return jnp.dot(a, b, preferred_element_type=F32)


def _dot_nt(a, b):
    return lax.dot_general(a, b, (((1,), (1,)), ((), ())), preferred_element_type=F32)


def _split3(x):
    hi = x.astype(BF16)
    r = x - hi.astype(F32)
    mid = r.astype(BF16)
    lo = (r - mid.astype(F32)).astype(BF16)
    return hi, mid, lo


def _dot_exact_l(ones_mat, x):
    hi, mid, lo = _split3(x)
    return _dot(ones_mat, hi) + _dot(ones_mat, mid) + _dot(ones_mat, lo)


def _dot_exact_r(x, ones_mat):
    hi, mid, lo = _split3(x)
    return _dot(hi, ones_mat) + _dot(mid, ones_mat) + _dot(lo, ones_mat)


def _log_sigmoid(x):
    return jnp.minimum(x, 0.0) - jnp.log(1.0 + jnp.exp(-jnp.abs(x)))


def _sigmoid(x):
    return 1.0 / (1.0 + jnp.exp(-x))


def _const_spec(shape):
    nd = len(shape)
    return pl.BlockSpec(shape, lambda *_: (0,) * nd)


def _mlp_kernel(h_ref, g_ref, wup_ref, wdn_ref, fg_ref, out_ref, *, fc, final):
    x = h_ref[...]
    xn = _rms(x, g_ref[...]).astype(BF16)
    acc = x
    for c in range(wup_ref.shape[1] // fc):
        hid = jnp.maximum(_dot(xn, wup_ref[:, c * fc:(c + 1) * fc]), 0.0)
        acc = acc + _dot((hid * hid).astype(BF16), wdn_ref[c * fc:(c + 1) * fc, :])
    out_ref[...] = _rms(acc, fg_ref[...]) if final else acc


def _mlp_layer(h2d, norm_g, w_up, w_down, final_g=None, *, tm=1024, fc=512):
    t, d = h2d.shape
    ff = w_up.shape[1]
    fg = (norm_g if final_g is None else final_g).reshape(1, d)
    return pl.pallas_call(
        functools.partial(_mlp_kernel, fc=fc, final=final_g is not None),
        grid=(t // tm,),
        in_specs=[pl.BlockSpec((tm, d), lambda i: (i, 0)), _const_spec((1, d)),
                  _const_spec((d, ff)), _const_spec((ff, d)), _const_spec((1, d))],
        out_specs=pl.BlockSpec((tm, d), lambda i: (i, 0)),
        out_shape=jax.ShapeDtypeStruct((t, d), F32),
        compiler_params=_cparams("parallel"),
        name="mlp",
    )(h2d, norm_g.reshape(1, d), w_up.astype(BF16), w_down.astype(BF16), fg)


def _ml_proj_kernel(h_ref, g_ref, wqk_ref, wv_ref, wo_ref, wg_ref, wgt_ref,
                    qk_ref, v_ref, o_ref, gc_ref, gr_ref):
    xn = _rms(h_ref[...], g_ref[...]).astype(BF16)
    qk_ref[...] = _dot(xn, wqk_ref[...])
    v_ref[...] = _dot(xn, wv_ref[...]).astype(v_ref.dtype)
    o_ref[...] = _dot(xn, wo_ref[...])
    gc_ref[...] = _dot(xn, wg_ref[...])
    gr_ref[...] = _dot_nt(wgt_ref[...], xn)


def _ml_kernel(qk_ref, v_ref, o_ref, gc_ref, gr_ref, h_ref, cw_ref, cb_ref, bcol_ref, brow_ref,
               hg_ref, wout_ref, rep_ref, out_ref, ext_scr, c_scr, m_scr, hbuf, *, lt, lc):
    nh, dk, dv = M_HEADS, M_QK_DIM, M_V_DIM
    hd = nh * dk

    @pl.when(pl.program_id(1) == 0)
    def _():
        c_scr[...] = jnp.zeros_like(c_scr)
        m_scr[...] = jnp.zeros_like(m_scr)
        ext_scr[0:8, :] = jnp.zeros((8, 2 * hd), F32)

    ext_scr[8:8 + lt, :] = qk_ref[...]
    conv = cb_ref[...] + jnp.zeros((lt, 2 * hd), F32)
    for kk in range(M_CONV):
        conv = conv + cw_ref[kk:kk + 1, :] * ext_scr[8 - (M_CONV - 1) + kk:8 - (M_CONV - 1) + kk + lt, :]
    ext_scr[0:8, :] = ext_scr[lt:lt + 8, :]
    act = conv * _sigmoid(conv)

    lane = lax.broadcasted_iota(jnp.int32, (1, LANES), 1)
    r_i = lax.broadcasted_iota(jnp.int32, (lc, lc), 0)
    c_i = lax.broadcasted_iota(jnp.int32, (lc, lc), 1)
    causal = c_i <= r_i
    tri_l = causal.astype(BF16)
    tri_u = (r_i <= c_i).astype(BF16)
    ones_blk = jnp.ones((lc, LANES), BF16)
    row16 = lax.broadcasted_iota(jnp.int32, (2 * nh, 1), 0)

    def both(x):
        return jnp.concatenate([x, x], axis=1)

    local = [[None] * nh for _ in range(lt // lc)]
    for c in range(lt // lc):
        rows = slice(c * lc, (c + 1) * lc)
        gcb = gc_ref[rows, :] + bcol_ref[...]
        gate_c = jnp.where(lane < nh, gcb, jnp.where(lane < 2 * nh, _log_sigmoid(gcb), 0.0))
        cum_c = _dot_exact_l(tri_l, jnp.where(lane < nh, 0.0, gate_c))
        hi, mid, lo = _split3(jnp.where(lane < nh, gate_c, cum_c))
        rep = _dot(jnp.concatenate([hi, mid, lo], axis=1), rep_ref[...])
        grb = gr_ref[:, rows] + brow_ref[...]
        gate_r = jnp.where(row16 < nh, grb, _log_sigmoid(grb))
        cum_r = _dot_exact_r(jnp.where(row16 < nh, 0.0, gate_r), tri_u)

        for h in range(nh):
            p, e = h // 2, h % 2
            head_mask = (lane >= e * dk) & (lane < (e + 1) * dk)
            q_m = (jnp.where(head_mask, act[rows, p * LANES:(p + 1) * LANES], 0.0) * (dk ** -0.5)).astype(BF16)
            k_m = jnp.where(head_mask, act[rows, hd + p * LANES:hd + (p + 1) * LANES], 0.0)
            v_aug = jnp.concatenate([v_ref[rows, h * dv:(h + 1) * dv].astype(BF16), ones_blk], axis=1)
            li_c = rep[:, h * LANES:(h + 1) * LANES]
            b_c = rep[:, (nh + h) * LANES:(nh + h + 1) * LANES]
            b_row = cum_r[nh + h:nh + h + 1, :]
            li_row = gate_r[h:h + 1, :]
            d_log = jnp.where(causal, b_c[:, :lc] - b_row + li_row, NEG_BIG)
            m_loc = jnp.max(d_log, axis=-1, keepdims=True) + jnp.zeros((lc, LANES), F32)
            s_loc = _dot_nt(q_m, k_m.astype(BF16)) * jnp.exp(d_log - m_loc[:, :lc])
            r_loc = _dot(s_loc.astype(BF16), v_aug)
            b_last = b_c[lc - 1:lc, :]
            m_loc_last = m_loc[lc - 1:lc, :]
            w_state = jnp.exp(b_last - b_c + li_c - m_loc_last)
            kv_loc = _dot(jnp.transpose(k_m * w_state).astype(BF16), v_aug)
            local[c][h] = (q_m, b_c, m_loc, r_loc, kv_loc, b_last, m_loc_last)

    for c in range(lt // lc):
        rows = slice(c * lc, (c + 1) * lc)
        for h in range(nh):
            q_m, b_c, m_loc, r_loc, kv_loc, b_last, m_loc_last = local[c][h]
            m_prev = m_scr[h:h + 1, :]
            g = b_c + m_prev
            m_t = jnp.maximum(g, m_loc)
            c_aug = c_scr[h]
            r = both(jnp.exp(g - m_t)) * _dot(q_m, c_aug.astype(BF16)) + both(jnp.exp(m_loc - m_t)) * r_loc
            hh = r[:, :dv] / jnp.maximum(jnp.abs(r[:, dv:]), jnp.exp(-m_t))

            m_new = m_t[lc - 1:lc, :]
            decay = jnp.exp(b_last + m_prev - m_new)
            c_scr[h] = both(decay) * c_aug + both(jnp.exp(m_loc_last - m_new)) * kv_loc
            m_scr[h:h + 1, :] = m_new

            yc = hh - jnp.mean(hh, axis=-1, keepdims=True)
            yn = yc * lax.rsqrt(jnp.mean(yc * yc, axis=-1, keepdims=True) + M_NORM_EPS)
            cols = slice(h * dv, (h + 1) * dv)
            hbuf[rows, cols] = yn * hg_ref[:, cols] * _sigmoid(o_ref[rows, cols])

    out_ref[...] = h_ref[...] + _dot(hbuf[...].astype(BF16), wout_ref[...])


def _mlstm_layer(h2d, bsz, seq, norm_g, w_in, conv_w, conv_b, b_i, b_f, head_g, w_out, *, tm=1024, lt=512, lc=128):
    t, d = h2d.shape
    nh, dk, dv = M_HEADS, M_QK_DIM, M_V_DIM
    o0 = 2 * nh * dk
    o1 = o0 + nh * dv
    o3 = o1 + 2 * nh
    wqk = w_in[:, :o0].astype(BF16)
    wv = w_in[:, o0:o1].astype(BF16)
    wo = w_in[:, o3:].astype(BF16)
    wg = jnp.pad(w_in[:, o1:o3], ((0, 0), (0, LANES - 2 * nh))).astype(BF16)
    wgt = jnp.transpose(w_in[:, o1:o3]).astype(BF16)
    g2 = norm_g.reshape(1, d)

    qk, v, o, gc, gr = pl.pallas_call(
        _ml_proj_kernel,
        grid=(t // tm,),
        in_specs=[pl.BlockSpec((tm, d), lambda i: (i, 0)), _const_spec((1, d)),
                  _const_spec(wqk.shape), _const_spec(wv.shape), _const_spec(wo.shape),
                  _const_spec(wg.shape), _const_spec(wgt.shape)],
        out_specs=[pl.BlockSpec((tm, o0), lambda i: (i, 0)), pl.BlockSpec((tm, nh * dv), lambda i: (i, 0)),
                   pl.BlockSpec((tm, nh * dv), lambda i: (i, 0)), pl.BlockSpec((tm, LANES), lambda i: (i, 0)),
                   pl.BlockSpec((2 * nh, tm), lambda i: (0, i))],
        out_shape=[jax.ShapeDtypeStruct((t, o0), F32), jax.ShapeDtypeStruct((t, nh * dv), BF16),
                   jax.ShapeDtypeStruct((t, nh * dv), F32), jax.ShapeDtypeStruct((t, LANES), F32),
                   jax.ShapeDtypeStruct((2 * nh, t), F32)],
        compiler_params=_cparams("parallel"),
        name="ml_proj",
    )(h2d, g2, wqk, wv, wo, wg, wgt)

    bcol = jnp.pad(jnp.concatenate([b_i, b_f]), (0, LANES - 2 * nh)).reshape(1, LANES)
    brow = jnp.concatenate([b_i, b_f]).reshape(2 * nh, 1)
    nt = seq // lt
    rep = (jnp.arange(3 * LANES)[:, None] % LANES == jnp.arange(2 * nh * LANES)[None, :] // LANES).astype(BF16)
    row_spec = lambda w: pl.BlockSpec((lt, w), lambda b, i: (b * nt + i, 0))
    return pl.pallas_call(
        functools.partial(_ml_kernel, lt=lt, lc=lc),
        grid=(bsz, nt),
        in_specs=[row_spec(o0), row_spec(nh * dv), row_spec(nh * dv), row_spec(LANES),
                  pl.BlockSpec((2 * nh, lt), lambda b, i: (0, b * nt + i)), row_spec(d),
                  _const_spec((M_CONV, o0)), _const_spec((1, o0)), _const_spec((1, LANES)),
                  _const_spec((2 * nh, 1)), _const_spec((1, nh * dv)), _const_spec((nh * dv, d)),
                  _const_spec(rep.shape)],
        out_specs=row_spec(d),
        out_shape=jax.ShapeDtypeStruct((t, d), F32),
        scratch_shapes=[pltpu.VMEM((lt + 8, o0), F32), pltpu.VMEM((nh, LANES, 2 * LANES), F32),
                        pltpu.VMEM((nh, LANES), F32), pltpu.VMEM((lt, nh * dv), F32)],
        compiler_params=_cparams("parallel", "arbitrary"),
        name="ml_mix",
    )(qk, v, o, gc, gr, h2d, conv_w, conv_b.reshape(1, o0), bcol, brow,
      head_g.reshape(1, nh * dv), w_out.astype(BF16), rep)


S5_GROUP_CH = 16
S5_STATE = 64
S5_CHUNK_GROUPS = 4
S5_CHUNK = S5_CHUNK_GROUPS * S5_STATE


def _s5_disc_kernel(ar_ref, ai_ref, ldt_ref, arx_ref, aix_ref, br_ref, bi_ref,
                    abr_ref, abi_ref, bbr_ref, bbi_ref):
    dt = jnp.exp(ldt_ref[...])

    def abar(ar, ai):
        mag = jnp.exp(dt * ar)
        return mag * jnp.cos(dt * ai), mag * jnp.sin(dt * ai)

    abr_ref[...], abi_ref[...] = abar(ar_ref[...], ai_ref[...])
    ar, ai = arx_ref[...], aix_ref[...]
    xr, xi = abar(ar, ai)
    inv = 1.0 / (ar * ar + ai * ai)
    zr, zi = xr - 1.0, xi
    coef_re = (zr * ar + zi * ai) * inv
    coef_im = (zi * ar - zr * ai) * inv
    br, bi = br_ref[...], bi_ref[...]
    bbr_ref[...] = coef_re * br - coef_im * bi
    bbi_ref[...] = coef_re * bi + coef_im * br


def _s5_kernel(h_ref, g_ref, abr_ref, abi_ref, bsl_ref, csl_ref, dsk_ref, wglu_ref, bglu_ref,
               out_ref, sr_scr, si_scr, *, tt, bsz):
    d = h_ref.shape[1]
    nchunk = sr_scr.shape[0]
    per_slab = nchunk // (d // S5_CHUNK)

    @pl.when(pl.program_id(0) == 0)
    def _():
        sr_scr[...] = jnp.zeros_like(sr_scr)
        si_scr[...] = jnp.zeros_like(si_scr)

    x = h_ref[...]
    u = _rms(x, g_ref[...])
    ub = u.astype(BF16)
    ys = []
    for slab in range(d // S5_CHUNK):
        bu = _dot(ub[:, slab * S5_CHUNK:(slab + 1) * S5_CHUNK], bsl_ref[slab])
        states = []
        for j in range(per_slab):
            c = slab * per_slab + j
            bu_re = bu[:, 2 * j * S5_CHUNK:(2 * j + 1) * S5_CHUNK]
            bu_im = bu[:, (2 * j + 1) * S5_CHUNK:(2 * j + 2) * S5_CHUNK]
            ar, ai = abr_ref[c], abi_ref[c]
            sr, si = sr_scr[c], si_scr[c]
            s_re, s_im = [], []
            for t in range(tt):
                rows = slice(t * bsz, (t + 1) * bsz)
                sr, si = ar * sr - ai * si + bu_re[rows], ar * si + ai * sr + bu_im[rows]
                s_re.append(sr)
                s_im.append(si)
            sr_scr[c] = sr
            si_scr[c] = si
            states += [jnp.concatenate(s_re, axis=0).astype(BF16), jnp.concatenate(s_im, axis=0).astype(BF16)]
        ys.append(_dot(jnp.concatenate(states, axis=1), csl_ref[slab]))
    y = jnp.concatenate(ys, axis=1) + dsk_ref[...] * u
    gelu = 0.5 * y * (1.0 + jnp.tanh(math.sqrt(2.0 / math.pi) * (y + 0.044715 * (y * y * y))))
    z = _dot(gelu.astype(BF16), wglu_ref[...]) + bglu_ref[...]
    out_ref[...] = x + z[:, :d] * _sigmoid(z[:, d:])


def _s5_layer(h2d, bsz, seq, norm_g, a_re, a_im, log_dt, b_re, b_im, c_re, c_im, d_skip, w_glu, b_glu, *, tt=32):
    t, d = h2d.shape
    ng, ns, hg = a_re.shape[0], S5_STATE, S5_GROUP_CH
    cg = S5_CHUNK_GROUPS
    nchunk = ng // cg
    per_slab = S5_CHUNK // (cg * hg)

    rep = lambda a: jnp.repeat(a, hg, axis=1)
    abr, abi, bbr, bbi = pl.pallas_call(
        _s5_disc_kernel,
        out_shape=[jax.ShapeDtypeStruct((ng, ns), F32)] * 2 + [jax.ShapeDtypeStruct((ng, ns * hg), F32)] * 2,
        name="s5_disc",
    )(a_re, a_im, log_dt.reshape(ng, 1), rep(a_re), rep(a_im), b_re.reshape(ng, ns * hg), b_im.reshape(ng, ns * hg))

    eye = jnp.eye(cg, dtype=F32)
    place = jax.nn.one_hot(jnp.arange(nchunk) % per_slab, per_slab, dtype=F32)

    def b_blocks(bb):
        m = bb.reshape(nchunk, cg, ns, hg)
        blk = jnp.einsum('cgph,gq->cghqp', m, eye).reshape(nchunk, cg * hg, cg * ns)
        return jnp.einsum('crs,cj->cjrs', blk, place).reshape(nchunk, per_slab * cg * hg, cg * ns).astype(BF16)

    def c_blocks(cc):
        m = cc.reshape(nchunk, cg, hg, ns)
        blk = jnp.einsum('cghp,gq->cgpqh', m, eye).reshape(nchunk, cg * ns, cg * hg)
        return jnp.einsum('csr,cj->csjr', blk, place).reshape(nchunk, cg * ns, per_slab * cg * hg).astype(BF16)

    rows = tt * bsz
    nslab = nchunk // per_slab
    bsl = jnp.stack([b_blocks(bbr), b_blocks(bbi)], axis=2).reshape(nslab, per_slab, S5_CHUNK, 2 * S5_CHUNK)
    bsl = bsl.transpose(0, 2, 1, 3).reshape(nslab, S5_CHUNK, per_slab * 2 * S5_CHUNK)
    csl = jnp.stack([c_blocks(c_re), -c_blocks(c_im)], axis=1).reshape(nslab, per_slab * 2 * S5_CHUNK, S5_CHUNK)
    ht = h2d.reshape(bsz, seq, d).transpose(1, 0, 2).reshape(t, d)
    out = pl.pallas_call(
        functools.partial(_s5_kernel, tt=tt, bsz=bsz),
        grid=(seq // tt,),
        in_specs=[pl.BlockSpec((rows, d), lambda i: (i, 0)), _const_spec((1, d)),
                  _const_spec((nchunk, 1, S5_CHUNK)), _const_spec((nchunk, 1, S5_CHUNK)),
                  _const_spec(bsl.shape), _const_spec(csl.shape),
                  _const_spec((1, d)), _const_spec((d, 2 * d)), _const_spec((1, 2 * d))],
        out_specs=pl.BlockSpec((rows, d), lambda i: (i, 0)),
        out_shape=jax.ShapeDtypeStruct((t, d), F32),
        scratch_shapes=[pltpu.VMEM((nchunk, bsz, S5_CHUNK), F32), pltpu.VMEM((nchunk, bsz, S5_CHUNK), F32)],
        compiler_params=_cparams("arbitrary"),
        name="s5_mix",
    )(ht, norm_g.reshape(1, d), abr.reshape(nchunk, 1, S5_CHUNK), abi.reshape(nchunk, 1, S5_CHUNK),
      bsl, csl, d_skip.reshape(1, d), w_glu.astype(BF16), b_glu.reshape(1, 2 * d))
    return out.reshape(seq, bsz, d).transpose(1, 0, 2).reshape(t, d)


F_HEADS = 16
F_HEAD_DIM = 64
F_DIAG_ROWS = 128
LOG2E = math.log2(math.e)


def _fox_proj_kernel(h_ref, g_ref, wq_ref, wk_ref, wv_ref, wo_ref, wgt_ref,
                     q_ref, k_ref, v_ref, o_ref, gr_ref):
    xn = _rms(h_ref[...], g_ref[...]).astype(BF16)
    q_ref[...] = (_dot(xn, wq_ref[...]) * (F_HEAD_DIM ** -0.5 * LOG2E)).astype(BF16)
    k_ref[...] = _dot(xn, wk_ref[...]).astype(BF16)
    v_ref[...] = _dot(xn, wv_ref[...]).astype(BF16)
    o_ref[...] = _dot(xn, wo_ref[...])
    gr_ref[...] = _dot_nt(wgt_ref[...], xn)


def _fox_kernel(q_ref, k_ref, v_ref, o_ref, gr_ref, h_ref, brow_ref, wout_ref,
                out_ref, cr_scr, obuf, m_scr, l_scr, acc_scr, *, tq, tk, seq):
    nh, dh = F_HEADS, F_HEAD_DIM
    i = pl.program_id(1)
    lane = lax.broadcasted_iota(jnp.int32, (1, LANES), 1)
    hmask = [lane < dh, lane >= dh]

    @pl.when(i == 0)
    def _():
        r_i = lax.broadcasted_iota(jnp.int32, (LANES, LANES), 0)
        c_i = lax.broadcasted_iota(jnp.int32, (LANES, LANES), 1)
        tri_u = (r_i <= c_i).astype(BF16)
        carry_r = jnp.zeros((nh, 1), F32)
        per = tk // LANES
        for blk in range(seq // LANES):
            rows = slice(blk * LANES, (blk + 1) * LANES)
            cr = _dot_exact_r(_log_sigmoid(gr_ref[:, rows] + brow_ref[...]), tri_u) + carry_r
            cr_scr[blk // per, :, (blk % per) * LANES:(blk % per + 1) * LANES] = cr * LOG2E
            carry_r = cr[:, LANES - 1:LANES]

    m_scr[...] = jnp.full(m_scr.shape, NEG_BIG, F32)
    l_scr[...] = jnp.zeros_like(l_scr)
    acc_scr[...] = jnp.zeros_like(acc_scr)

    def step(j, qrow0, nrows, width, masked):
        k0 = pl.multiple_of(j * tk, tk)
        crow = cr_scr[j][:, :width]
        qrows = slice(qrow0, qrow0 + nrows)
        if masked:
            causal = (lax.broadcasted_iota(jnp.int32, (nrows, width), 1)
                      <= lax.broadcasted_iota(jnp.int32, (nrows, width), 0) + qrow0)
        for p in range(nh // 2):
            cols = slice(p * LANES, (p + 1) * LANES)
            kp = k_ref[pl.ds(k0, width), cols]
            vp = v_ref[pl.ds(k0, width), cols]
            qp = q_ref[qrows, cols]
            q2 = jnp.concatenate([jnp.where(hmask[e], qp, jnp.zeros_like(qp)) for e in range(2)], axis=0)
            s2 = _dot_nt(q2, kp)
            dm, ls, pms = [], [], []
            for e in range(2):
                h = 2 * p + e
                s = s2[e * nrows:(e + 1) * nrows] - crow[h:h + 1, :]
                if masked:
                    s = jnp.where(causal, s, NEG_BIG)
                m_prev = m_scr[h, qrows, :]
                m_next = jnp.maximum(m_prev, jnp.max(s, axis=-1, keepdims=True))
                m_scr[h, qrows, :] = m_next
                pm = jnp.exp2(s - jnp.concatenate([m_next] * (width // LANES), axis=1))
                dm.append(m_prev - m_next)
                ls.append(jnp.sum(pm, axis=-1, keepdims=True))
                pms.append(pm.astype(BF16))
            pv = _dot(jnp.concatenate(pms, axis=0), vp)
            alpha = jnp.exp2(jnp.where(hmask[0], dm[0], dm[1]))
            l_scr[p, qrows, :] = alpha * l_scr[p, qrows, :] + jnp.where(hmask[0], ls[0], ls[1])
            acc_scr[p, qrows, :] = alpha * acc_scr[p, qrows, :] + jnp.where(hmask[0], pv[:nrows], pv[nrows:])

    def body(j, carry):
        step(j, 0, tq, tk, False)
        return carry

    lax.fori_loop(0, i, body, 0)
    for r in range(tq // F_DIAG_ROWS):
        step(i, r * F_DIAG_ROWS, F_DIAG_ROWS, (r + 1) * F_DIAG_ROWS, True)
    for p in range(nh // 2):
        cols = slice(p * LANES, (p + 1) * LANES)
        obuf[:, cols] = acc_scr[p] / l_scr[p] * _sigmoid(o_ref[:, cols])

    out_ref[...] = h_ref[...] + _dot(obuf[...].astype(BF16), wout_ref[...])


def _fox_layer(h2d, bsz, seq, norm_g, w_in, b_f, w_out, *, tm=1024, tq=256):
    t, d = h2d.shape
    nh, dh = F_HEADS, F_HEAD_DIM
    hd = nh * dh
    wq, wk, wv = (w_in[:, j * hd:(j + 1) * hd].astype(BF16) for j in range(3))
    wf = w_in[:, 3 * hd:3 * hd + nh]
    wo = w_in[:, 3 * hd + nh:].astype(BF16)
    wgt = jnp.transpose(wf).astype(BF16)
    tile = lambda w: pl.BlockSpec((tm, w), lambda i: (i, 0))
    q, k, v, o, gr = pl.pallas_call(
        _fox_proj_kernel,
        grid=(t // tm,),
        in_specs=[tile(d), _const_spec((1, d)), _const_spec(wq.shape), _const_spec(wk.shape),
                  _const_spec(wv.shape), _const_spec(wo.shape), _const_spec(wgt.shape)],
        out_specs=[tile(hd), tile(hd), tile(hd), tile(hd), pl.BlockSpec((nh, tm), lambda i: (0, i))],
        out_shape=[jax.ShapeDtypeStruct((t, hd), BF16)] * 3 + [jax.ShapeDtypeStruct((t, hd), F32),
                   jax.ShapeDtypeStruct((nh, t), F32)],
        compiler_params=_cparams("parallel"),
        name="fox_proj",
    )(h2d, norm_g.reshape(1, d), wq, wk, wv, wo, wgt)

    nt = seq // tq
    q_spec = lambda w: pl.BlockSpec((tq, w), lambda b, i: (b * nt + i, 0))
    seq_spec = lambda w: pl.BlockSpec((seq, w), lambda b, i: (b, 0))
    return pl.pallas_call(
        functools.partial(_fox_kernel, tq=tq, tk=tq, seq=seq),
        grid=(bsz, nt),
        in_specs=[q_spec(hd), seq_spec(hd), seq_spec(hd), q_spec(hd),
                  pl.BlockSpec((nh, seq), lambda b, i: (0, b)), q_spec(d),
                  _const_spec((nh, 1)), _const_spec((hd, d))],
        out_specs=q_spec(d),
        out_shape=jax.ShapeDtypeStruct((t, d), F32),
        scratch_shapes=[pltpu.VMEM((nt, nh, tq), F32), pltpu.VMEM((tq, hd), F32),
                        pltpu.VMEM((nh, tq, LANES), F32), pltpu.VMEM((nh // 2, tq, LANES), F32),
                        pltpu.VMEM((nh // 2, tq, LANES), F32)],
        compiler_params=_cparams("parallel", "arbitrary"),
        name="fox_attn",
    )(q, k, v, o, gr, h2d, b_f.reshape(nh, 1), w_out.astype(BF16))


R_HEADS = 16
R_HEAD_DIM = 64
R_LN_EPS = 64e-5
R_SUB = 16
R_HALF = R_SUB // 2


def _softplus(z):
    return jnp.maximum(z, 0.0) + jnp.log(1.0 + jnp.exp(-jnp.abs(z)))


def _head_sum(x, lane):
    lo = lane < R_HEAD_DIM
    s0 = jnp.sum(jnp.where(lo, x, 0.0), axis=-1, keepdims=True)
    s1 = jnp.sum(jnp.where(lo, 0.0, x), axis=-1, keepdims=True)
    return jnp.where(lo, s0, s1)


def _rw_proj_kernel(h_ref, g_ref, mu_ref, wr_ref, wk_ref, wv_ref, wd_ref, wa_ref, wg_ref, wup_ref, aup_ref,
                    gup_ref, w0_ref, a0_ref, kk_ref, ka_ref,
                    r_ref, lw_ref, k_ref, v_ref, al_ref, be_ref, g_out_ref, ext_scr, *, tm, seq):
    d = h_ref.shape[1]
    i = pl.program_id(0)
    xn = _rms(h_ref[...], g_ref[...])
    @pl.when((i * tm) % seq == 0)
    def _():
        ext_scr[7:8, :] = jnp.zeros((1, d), F32)

    @pl.when((i * tm) % seq != 0)
    def _():
        ext_scr[7:8, :] = ext_scr[7 + tm:8 + tm, :]

    ext_scr[8:8 + tm, :] = xn
    xb = xn.astype(BF16)
    dxb = (ext_scr[7:7 + tm, :] - xn).astype(BF16)
    mub = mu_ref[...].astype(BF16)
    mix = lambda j: xb + dxb * mub[j:j + 1, :]
    r_ref[...] = _dot(mix(0), wr_ref[...]).astype(r_ref.dtype)
    w_in = w0_ref[...] + _dot(jnp.tanh(_dot(mix(1), wd_ref[...])).astype(BF16), wup_ref[...])
    lw_ref[...] = -math.exp(-0.5) * _sigmoid(w_in)
    k = _dot(mix(2), wk_ref[...])
    v_ref[...] = _dot(mix(3), wv_ref[...]).astype(v_ref.dtype)
    a = _sigmoid(a0_ref[...] + _dot(_dot(mix(4), wa_ref[...]).astype(BF16), aup_ref[...]))
    g_out_ref[...] = _dot(_sigmoid(_dot(mix(5), wg_ref[...])).astype(BF16), gup_ref[...]).astype(g_out_ref.dtype)
    k_ref[...] = (k * (1.0 + (a - 1.0) * ka_ref[...])).astype(k_ref.dtype)
    lane = lax.broadcasted_iota(jnp.int32, (1, LANES), 1)
    for p in range(d // LANES):
        cols = slice(p * LANES, (p + 1) * LANES)
        kk = k[:, cols] * kk_ref[:, cols]
        kk = kk * jnp.minimum(lax.rsqrt(_head_sum(kk * kk, lane)), 1e12)
        al_ref[:, cols] = kk.astype(al_ref.dtype)
        be_ref[:, cols] = (kk * a[:, cols]).astype(be_ref.dtype)


def _head_sum_mxu(x, ones_bd):
    hi = x.astype(BF16)
    lo = (x - hi.astype(F32)).astype(BF16)
    return _dot(jnp.concatenate([hi, lo], axis=1), ones_bd)


def _rw_kernel(r_ref, lw_ref, k_ref, v_ref, al_ref, be_ref, g_ref, h_ref, rk_ref, lng_ref, lnb_ref, wout_ref,
               sel_ref, out_ref, ht_scr, ybuf, aab_scr, aoff_scr, arkb_scr, x_scr, coef_scr, *, lt, lc):
    d = h_ref.shape[1]
    dh = R_HEAD_DIM
    npair = d // LANES
    nb = lc // R_SUB

    @pl.when(pl.program_id(1) == 0)
    def _():
        ht_scr[...] = jnp.zeros_like(ht_scr)

    lane = lax.broadcasted_iota(jnp.int32, (1, LANES), 1)
    hmask = [lane < dh, lane >= dh]
    lane2 = lax.broadcasted_iota(jnp.int32, (1, 2 * LANES), 1) % LANES
    hmask2 = [lane2 < dh, lane2 >= dh]
    r_i = lax.broadcasted_iota(jnp.int32, (lc, lc), 0)
    c_i = lax.broadcasted_iota(jnp.int32, (lc, lc), 1)
    tri_l = (c_i <= r_i).astype(BF16)
    r_p2 = lax.broadcasted_iota(jnp.int32, (lc, LANES), 0)
    c_p2 = lax.broadcasted_iota(jnp.int32, (lc, LANES), 1) % lc
    strict2 = c_p2 < r_p2
    incl2 = c_p2 <= r_p2
    below2 = (c_p2 // R_SUB) < (r_p2 // R_SUB)
    rr = lax.broadcasted_iota(jnp.int32, (LANES, LANES), 0)
    cc = lax.broadcasted_iota(jnp.int32, (LANES, LANES), 1)
    bdiag = (rr < dh) == (cc < dh)

    def chunk(c, carry):
        rows = slice(c * lc, (c + 1) * lc)
        lw = lw_ref[rows, :]
        cum = _dot_exact_l(tri_l, lw)
        cum_last = cum[lc - 1:lc, :]
        e_pos = jnp.exp(cum)
        e_neg = jnp.exp(-cum)
        e_rem = jnp.exp(cum_last - cum)
        r_t = r_ref[rows, :] * e_pos
        a_t = al_ref[rows, :] * jnp.exp(cum - lw)
        k_all = k_ref[rows, :]
        b_all = be_ref[rows, :]
        k_t, b_t = k_all * e_neg, b_all * e_neg
        k_h, b_h = k_all * e_rem, b_all * e_rem
        gam = jnp.exp(cum_last)

        nheads = 2 * npair
        vps = [v_ref[rows, p * LANES:(p + 1) * LANES].astype(BF16) for p in range(npair)]
        vp2s = [jnp.concatenate([vp, vp], axis=0) for vp in vps]

        scs = []
        for p in range(npair):
            cols = slice(p * LANES, (p + 1) * LANES)
            lhs = jnp.concatenate([a_t[:, cols], r_t[:, cols]], axis=0).astype(BF16)
            b_p, k_p = b_t[:, cols], k_t[:, cols]
            rhs = jnp.concatenate([jnp.where(hmask[0], b_p, 0.0), jnp.where(hmask[1], b_p, 0.0),
                                   jnp.where(hmask[0], k_p, 0.0), jnp.where(hmask[1], k_p, 0.0)], axis=0)
            scs.append(_dot_nt(lhs, rhs.astype(BF16)))
        aaks = []
        for p in range(npair):
            sc = scs[p]
            top_b, top_k = sc[:lc, :LANES], sc[:lc, LANES:]
            bot_b, bot_k = sc[lc:, :LANES], sc[lc:, LANES:]
            aab_scr[p] = jnp.where(strict2, top_b, 0.0)
            aoff_scr[p] = jnp.where(below2, top_b, 0.0).astype(BF16)
            arkb_scr[p] = jnp.concatenate([jnp.where(incl2, bot_k, 0.0), jnp.where(incl2, bot_b, 0.0)],
                                          axis=1).astype(BF16)
            a_ak = jnp.where(strict2, top_k, 0.0)
            aaks.append(jnp.concatenate([jnp.where(hmask[0], a_ak, 0.0), jnp.where(hmask[1], a_ak, 0.0)],
                                        axis=0).astype(BF16))
        for p in range(npair):
            cols = slice(p * LANES, (p + 1) * LANES)
            av = _dot(aaks[p], vp2s[p])
            for e in range(2):
                av_e = jnp.where(hmask[e], av[e * lc:(e + 1) * lc], 0.0)
                x_scr[2 * p + e] = jnp.where(hmask[e], a_t[:, cols], 0.0) + pltpu.roll(av_e, dh, axis=1)

        for bi in range(nb):
            brows = slice(bi * R_SUB, (bi + 1) * R_SUB)
            a_blk = aab_scr[:, brows, :].reshape(npair * R_SUB, LANES)
            a_hi = a_blk.astype(BF16)
            a_lo = (a_blk - a_hi.astype(F32)).astype(BF16)
            coef_scr[...] = _dot(jnp.concatenate([a_hi, a_lo], axis=1), sel_ref[bi])
            xs = [x_scr[h, brows, :] for h in range(nheads)]
            if bi > 0:
                for p in range(npair):
                    slab = aoff_scr[p, brows, :]
                    lhs = jnp.concatenate([jnp.where(hmask[0], slab, jnp.zeros_like(slab)),
                                           jnp.where(hmask[1], slab, jnp.zeros_like(slab))], axis=0)
                    xst = x_scr[2 * p:2 * p + 2].reshape(2 * lc, LANES).astype(BF16)
                    corr = _dot(lhs, xst)
                    for e in range(2):
                        xs[2 * p + e] = xs[2 * p + e] - corr[e * R_SUB:(e + 1) * R_SUB]
            x_lo = [x[:R_HALF] for x in xs]
            x_hi = [x[R_HALF:] for x in xs]
            for s in range(R_HALF - 1):
                for h in range(nheads):
                    p, e = h // 2, h % 2
                    tile = e * (R_HALF - 1) + s
                    coef = coef_scr[p * R_SUB:(p + 1) * R_SUB, tile * LANES:(tile + 1) * LANES]
                    row = x_lo[h][s:s + 1, :]
                    x_lo[h] = x_lo[h] - coef[:R_HALF] * row
                    x_hi[h] = x_hi[h] - coef[R_HALF:] * row
            for s in range(R_HALF - 1, R_SUB - 1):
                for h in range(nheads):
                    p, e = h // 2, h % 2
                    col = e * lc + bi * R_SUB + s
                    coef = aab_scr[p, bi * R_SUB + R_HALF:(bi + 1) * R_SUB, col:col + 1]
                    row = x_lo[h][s:s + 1, :] if s < R_HALF else x_hi[h][s - R_HALF:s - R_HALF + 1, :]
                    x_hi[h] = x_hi[h] - coef * row
            for h in range(nheads):
                x_scr[h, bi * R_SUB:bi * R_SUB + R_HALF, :] = x_lo[h]
                x_scr[h, bi * R_SUB + R_HALF:(bi + 1) * R_SUB, :] = x_hi[h]

        hts, ubs, yhs = [], [], []
        for p in range(npair):
            cols = slice(p * LANES, (p + 1) * LANES)
            x0, x1 = x_scr[2 * p], x_scr[2 * p + 1]
            w_pair = jnp.where(hmask[0], x0, x1)
            uv_pair = pltpu.roll(jnp.where(hmask[0], x1, x0), dh, axis=1)
            ht = ht_scr[p]
            hts.append(ht)
            wr = jnp.concatenate([w_pair, r_t[:, cols]], axis=0).astype(BF16)
            wh = _dot_nt(wr, ht.astype(BF16))
            ubs.append((wh[:lc] + uv_pair).astype(BF16))
            yhs.append(wh[lc:])
        for p in range(npair):
            cols = slice(p * LANES, (p + 1) * LANES)
            arkb = arkb_scr[p]
            lhs = jnp.concatenate([jnp.where(hmask2[e], arkb, jnp.zeros_like(arkb)) for e in range(2)], axis=0)
            rhs = jnp.concatenate([vp2s[p], -ubs[p], -ubs[p]], axis=0)
            intra = _dot(lhs, rhs)
            ybuf[rows, cols] = yhs[p] + jnp.where(hmask[0], intra[:lc], intra[lc:])
        for p in range(npair):
            cols = slice(p * LANES, (p + 1) * LANES)
            vu = jnp.concatenate([vps[p], -ubs[p]], axis=0)
            kb = jnp.concatenate([k_h[:, cols], b_h[:, cols]], axis=0).astype(BF16)
            upd = lax.dot_general(vu, kb, (((0,), (0,)), ((), ())), preferred_element_type=F32)
            ht_scr[p] = hts[p] * gam[:, cols] + jnp.where(bdiag, upd, 0.0)
        return carry

    for c in range(lt // lc):
        chunk(c, 0)

    rr2 = lax.broadcasted_iota(jnp.int32, (2 * LANES, LANES), 0)
    cc2 = lax.broadcasted_iota(jnp.int32, (2 * LANES, LANES), 1)
    ones_bd = (((rr2 % LANES) < dh) == (cc2 < dh)).astype(BF16)
    for p in range(npair):
        cols = slice(p * LANES, (p + 1) * LANES)
        y = ybuf[:, cols]
        r, k, v = r_ref[:, cols], k_ref[:, cols], v_ref[:, cols]
        bonus = _head_sum(r * k * rk_ref[:, cols], lane) * v
        yc = y - _head_sum_mxu(y, ones_bd) * (1.0 / dh)
        yn = yc * lax.rsqrt(_head_sum_mxu(yc * yc, ones_bd) * (1.0 / dh) + R_LN_EPS)
        ybuf[:, cols] = (yn * lng_ref[:, cols] + lnb_ref[:, cols] + bonus) * g_ref[:, cols]
    out_ref[...] = h_ref[...] + _dot(ybuf[...].astype(BF16), wout_ref[...])


def _rwkv_layer(h2d, bsz, seq, norm_g, mu, w_in, w0, w_up, a0, a_up, g_up, k_k, k_a, r_k, ln_g, ln_b, w_out,
                *, tm=512, lt=512, lc=64, act_dtype=F32):
    t, d = h2d.shape
    c0 = 3 * d
    c1 = c0 + w_up.shape[0]
    c2 = c1 + a_up.shape[0]
    pad_c = lambda w, n: jnp.pad(w, ((0, 0), (0, n - w.shape[1]))).astype(BF16)
    pad_r = lambda w, n: jnp.pad(w, ((0, n - w.shape[0]), (0, 0))).astype(BF16)
    up = lambda n: -(-n // LANES) * LANES
    nw, na, ng = up(c1 - c0), up(c2 - c1), up(w_in.shape[1] - c2)
    wr, wk, wv = (w_in[:, j * d:(j + 1) * d].astype(BF16) for j in range(3))
    wd, wa, wg = pad_c(w_in[:, c0:c1], nw), pad_c(w_in[:, c1:c2], na), pad_c(w_in[:, c2:], ng)
    row = lambda a: a.reshape(1, d)
    tile = pl.BlockSpec((tm, d), lambda i: (i, 0))
    outs = pl.pallas_call(
        functools.partial(_rw_proj_kernel, tm=tm, seq=seq),
        grid=(t // tm,),
        in_specs=[tile, _const_spec((1, d)), _const_spec((6, d)), _const_spec((d, d)), _const_spec((d, d)),
                  _const_spec((d, d)), _const_spec((d, nw)), _const_spec((d, na)), _const_spec((d, ng)),
                  _const_spec((nw, d)), _const_spec((na, d)), _const_spec((ng, d)),
                  _const_spec((1, d)), _const_spec((1, d)), _const_spec((1, d)), _const_spec((1, d))],
        out_specs=[tile] * 7,
        out_shape=[jax.ShapeDtypeStruct((t, d), F32 if name == "lw" else act_dtype)
                   for name in ("r", "lw", "k", "v", "al", "be", "g")],
        scratch_shapes=[pltpu.VMEM((tm + 8, d), F32)],
        compiler_params=_cparams("arbitrary"),
        name="rw_proj",
    )(h2d, row(norm_g), mu, wr, wk, wv, wd, wa, wg, pad_r(w_up, nw), pad_r(a_up, na), pad_r(g_up, ng),
      row(w0), row(a0), row(k_k), row(k_a))
    r, lw, k, v, al, be, g = outs

    nt = seq // lt
    assert lc == R_HEAD_DIM, "pair tiles put the two heads' (lc, lc) matrices side by side on 128 lanes"
    nb, nmx = lc // R_SUB, R_HALF - 1
    nsel = 2 * nmx * LANES
    klane = jnp.arange(2 * LANES)[None, :, None] % LANES
    otile = jnp.arange(nsel)[None, None, :] // LANES
    sel = ((klane // lc == otile // nmx)
           & (klane % lc == jnp.arange(nb)[:, None, None] * R_SUB + otile % nmx)).astype(BF16)
    spec = pl.BlockSpec((lt, d), lambda b, i: (b * nt + i, 0))
    return pl.pallas_call(
        functools.partial(_rw_kernel, lt=lt, lc=lc),
        grid=(bsz, nt),
        in_specs=[spec] * 8 + [_const_spec((1, d))] * 3 + [_const_spec((d, d)), _const_spec(sel.shape)],
        out_specs=spec,
        out_shape=jax.ShapeDtypeStruct((t, d), F32),
        scratch_shapes=[pltpu.VMEM((d // LANES, LANES, LANES), F32), pltpu.VMEM((lt, d), F32),
                        pltpu.VMEM((d // LANES, lc, LANES), F32), pltpu.VMEM((d // LANES, lc, LANES), BF16),
                        pltpu.VMEM((d // LANES, lc, 2 * LANES), BF16),
                        pltpu.VMEM((R_HEADS, lc, LANES), F32), pltpu.VMEM((d // LANES * R_SUB, nsel), F32)],
        compiler_params=_cparams("parallel", "arbitrary"),
        name="rw_mix",
    )(r, lw, k, v, al, be, g, h2d, r_k.reshape(1, d), row(ln_g), row(ln_b), w_out.astype(BF16), sel)


def kernel(x, mlp_norm_g, mlp_w_up, mlp_w_down, final_norm_g, s5_norm_g, s5_a_re, s5_a_im, s5_log_dt, s5_b_re, s5_b_im, s5_c_re, s5_c_im, s5_d, s5_w_glu, s5_b_glu, ml_norm_g, ml_w_in, ml_conv_w, ml_conv_b, ml_b_i, ml_b_f, ml_head_g, ml_w_out, fox_norm_g, fox_w_in, fox_b_f, fox_w_out, rw_norm_g, rw_mu, rw_w_in, rw_w0, rw_w_up, rw_a0, rw_a_up, rw_g_up, rw_k_k, rw_k_a, rw_r_k, rw_ln_g, rw_ln_b, rw_w_out):
    bsz, seq, d = x.shape
    depth = mlp_w_up.shape[0]
    h = x.reshape(bsz * seq, d)
    for i in range(depth):
        kind, j = i % 4, i // 4
        if kind == 0:
            h = _s5_layer(h, bsz, seq, s5_norm_g[j], s5_a_re[j], s5_a_im[j], s5_log_dt[j], s5_b_re[j], s5_b_im[j],
                          s5_c_re[j], s5_c_im[j], s5_d[j], s5_w_glu[j], s5_b_glu[j])
        elif kind == 1:
            h = _mlstm_layer(h, bsz, seq, ml_norm_g[j], ml_w_in[j], ml_conv_w[j], ml_conv_b[j], ml_b_i[j],
                             ml_b_f[j], ml_head_g[j], ml_w_out[j])
        elif kind == 2:
            h = _fox_layer(h, bsz, seq, fox_norm_g[j], fox_w_in[j], fox_b_f[j], fox_w_out[j])
        else:
            h = _rwkv_layer(h, bsz, seq, rw_norm_g[j], rw_mu[j], rw_w_in[j], rw_w0[j], rw_w_up[j], rw_a0[j],
                            rw_a_up[j], rw_g_up[j], rw_k_k[j], rw_k_a[j], rw_r_k[j], rw_ln_g[j], rw_ln_b[j],
                            rw_w_out[j])
        h = _mlp_layer(h, mlp_norm_g[i], mlp_w_up[i], mlp_w_down[i],
                       final_g=final_norm_g if i == depth - 1 else None)
    return h.reshape(bsz, seq, d)
```

```python
import functools
import math

import jax
import jax.numpy as jnp
from jax import lax
from jax.experimental import pallas as pl
from jax.experimental.pallas import tpu as pltpu

F32 = jnp.float32
BF16 = jnp.bfloat16

NORM_EPS = 1e-6
NEG_BIG = -1e30
LANES = 128
VMEM_LIMIT = 56 * 1024 * 1024

M_HEADS = 8
M_QK_DIM = 64
M_V_DIM = 128
M_CONV = 4
M_NORM_EPS = 1e-6


def _cparams(*sem):
    return pltpu.CompilerParams(dimension_semantics=sem, vmem_limit_bytes=VMEM_LIMIT)


def _rms(x, g):
    return x * lax.rsqrt(jnp.mean(x * x, axis=-1, keepdims=True) + NORM_EPS) * g


def _dot(a, b):
    return jnp.dot(a, b, preferred_element_type=F32)


def _dot_nt(a, b):
    return lax.dot_general(a, b, (((1,), (1,)), ((), ())), preferred_element_type=F32)


def _split3(x):
    hi = x.astype(BF16)
    r = x - hi.astype(F32)
    mid = r.astype(BF16)
    lo = (r - mid.astype(F32)).astype(BF16)
    return hi, mid, lo


def _dot_exact_l(ones_mat, x):
    hi, mid, lo = _split3(x)
    return _dot(ones_mat, hi) + _dot(ones_mat, mid) + _dot(ones_mat, lo)


def _dot_exact_r(x, ones_mat):
    hi, mid, lo = _split3(x)
    return _dot(hi, ones_mat) + _dot(mid, ones_mat) + _dot(lo, ones_mat)


def _log_sigmoid(x):
    return jnp.minimum(x, 0.0) - jnp.log(1.0 + jnp.exp(-jnp.abs(x)))


def _sigmoid(x):
    return 1.0 / (1.0 + jnp.exp(-x))


def _const_spec(shape):
    nd = len(shape)
    return pl.BlockSpec(shape, lambda *_: (0,) * nd)


def _mlp_kernel(h_ref, g_ref, wup_ref, wdn_ref, fg_ref, out_ref, *, fc, final):
    x = h_ref[...]
    xn = _rms(x, g_ref[...]).astype(BF16)
    acc = x
    for c in range(wup_ref.shape[1] // fc):
        hid = jnp.maximum(_dot(xn, wup_ref[:, c * fc:(c + 1) * fc]), 0.0)
        acc = acc + _dot((hid * hid).astype(BF16), wdn_ref[c * fc:(c + 1) * fc, :])
    out_ref[...] = _rms(acc, fg_ref[...]) if final else acc


def _mlp_layer(h2d, norm_g, w_up, w_down, final_g=None, *, tm=1024, fc=512):
    t, d = h2d.shape
    ff = w_up.shape[1]
    fg = (norm_g if final_g is None else final_g).reshape(1, d)
    return pl.pallas_call(
        functools.partial(_mlp_kernel, fc=fc, final=final_g is not None),
        grid=(t // tm,),
        in_specs=[pl.BlockSpec((tm, d), lambda i: (i, 0)), _const_spec((1, d)),
                  _const_spec((d, ff)), _const_spec((ff, d)), _const_spec((1, d))],
        out_specs=pl.BlockSpec((tm, d), lambda i: (i, 0)),
        out_shape=jax.ShapeDtypeStruct((t, d), F32),
        compiler_params=_cparams("parallel"),
        name="mlp",
    )(h2d, norm_g.reshape(1, d), w_up.astype(BF16), w_down.astype(BF16), fg)


def _ml_proj_kernel(h_ref, g_ref, wqk_ref, wv_ref, wo_ref, wg_ref, wgt_ref,
                    qk_ref, v_ref, o_ref, gc_ref, gr_ref):
    xn = _rms(h_ref[...], g_ref[...]).astype(BF16)
    qk_ref[...] = _dot(xn, wqk_ref[...])
    v_ref[...] = _dot(xn, wv_ref[...]).astype(v_ref.dtype)
    o_ref[...] = _dot(xn, wo_ref[...])
    gc_ref[...] = _dot(xn, wg_ref[...])
    gr_ref[...] = _dot_nt(wgt_ref[...], xn)


def _ml_kernel(qk_ref, v_ref, o_ref, gc_ref, gr_ref, h_ref, cw_ref, cb_ref, bcol_ref, brow_ref,
               hg_ref, wout_ref, rep_ref, out_ref, ext_scr, c_scr, m_scr, hbuf, *, lt, lc):
    nh, dk, dv = M_HEADS, M_QK_DIM, M_V_DIM
    hd = nh * dk

    @pl.when(pl.program_id(1) == 0)
    def _():
        c_scr[...] = jnp.zeros_like(c_scr)
        m_scr[...] = jnp.zeros_like(m_scr)
        ext_scr[0:8, :] = jnp.zeros((8, 2 * hd), F32)

    ext_scr[8:8 + lt, :] = qk_ref[...]
    conv = cb_ref[...] + jnp.zeros((lt, 2 * hd), F32)
    for kk in range(M_CONV):
        conv = conv + cw_ref[kk:kk + 1, :] * ext_scr[8 - (M_CONV - 1) + kk:8 - (M_CONV - 1) + kk + lt, :]
    ext_scr[0:8, :] = ext_scr[lt:lt + 8, :]
    act = conv * _sigmoid(conv)

    lane = lax.broadcasted_iota(jnp.int32, (1, LANES), 1)
    r_i = lax.broadcasted_iota(jnp.int32, (lc, lc), 0)
    c_i = lax.broadcasted_iota(jnp.int32, (lc, lc), 1)
    causal = c_i <= r_i
    tri_l = causal.astype(BF16)
    tri_u = (r_i <= c_i).astype(BF16)
    ones_blk = jnp.ones((lc, LANES), BF16)
    row16 = lax.broadcasted_iota(jnp.int32, (2 * nh, 1), 0)

    def both(x):
        return jnp.concatenate([x, x], axis=1)

    local = [[None] * nh for _ in range(lt // lc)]
    for c in range(lt // lc):
        rows = slice(c * lc, (c + 1) * lc)
        gcb = gc_ref[rows, :] + bcol_ref[...]
        gate_c = jnp.where(lane < nh, gcb, jnp.where(lane < 2 * nh, _log_sigmoid(gcb), 0.0))
        cum_c = _dot_exact_l(tri_l, jnp.where(lane < nh, 0.0, gate_c))
        hi, mid, lo = _split3(jnp.where(lane < nh, gate_c, cum_c))
        rep = _dot(jnp.concatenate([hi, mid, lo], axis=1), rep_ref[...])
        grb = gr_ref[:, rows] + brow_ref[...]
        gate_r = jnp.where(row16 < nh, grb, _log_sigmoid(grb))
        cum_r = _dot_exact_r(jnp.where(row16 < nh, 0.0, gate_r), tri_u)

        for h in range(nh):
            p, e = h // 2, h % 2
            head_mask = (lane >= e * dk) & (lane < (e + 1) * dk)
            q_m = (jnp.where(head_mask, act[rows, p * LANES:(p + 1) * LANES], 0.0) * (dk ** -0.5)).astype(BF16)
            k_m = jnp.where(head_mask, act[rows, hd + p * LANES:hd + (p + 1) * LANES], 0.0)
            v_aug = jnp.concatenate([v_ref[rows, h * dv:(h + 1) * dv].astype(BF16), ones_blk], axis=1)
            li_c = rep[:, h * LANES:(h + 1) * LANES]
            b_c = rep[:, (nh + h) * LANES:(nh + h + 1) * LANES]
            b_row = cum_r[nh + h:nh + h + 1, :]
            li_row = gate_r[h:h + 1, :]
            d_log = jnp.where(causal, b_c[:, :lc] - b_row + li_row, NEG_BIG)
            m_loc = jnp.max(d_log, axis=-1, keepdims=True) + jnp.zeros((lc, LANES), F32)
            s_loc = _dot_nt(q_m, k_m.astype(BF16)) * jnp.exp(d_log - m_loc[:, :lc])
            r_loc = _dot(s_loc.astype(BF16), v_aug)
            b_last = b_c[lc - 1:lc, :]
            m_loc_last = m_loc[lc - 1:lc, :]
            w_state = jnp.exp(b_last - b_c + li_c - m_loc_last)
            kv_loc = _dot(jnp.transpose(k_m * w_state).astype(BF16), v_aug)
            local[c][h] = (q_m, b_c, m_loc, r_loc, kv_loc, b_last, m_loc_last)

    for c in range(lt // lc):
        rows = slice(c * lc, (c + 1) * lc)
        for h in range(nh):
            q_m, b_c, m_loc, r_loc, kv_loc, b_last, m_loc_last = local[c][h]
            m_prev = m_scr[h:h + 1, :]
            g = b_c + m_prev
            m_t = jnp.maximum(g, m_loc)
            c_aug = c_scr[h]
            r = both(jnp.exp(g - m_t)) * _dot(q_m, c_aug.astype(BF16)) + both(jnp.exp(m_loc - m_t)) * r_loc
            hh = r[:, :dv] / jnp.maximum(jnp.abs(r[:, dv:]), jnp.exp(-m_t))

            m_new = m_t[lc - 1:lc, :]
            decay = jnp.exp(b_last + m_prev - m_new)
            c_scr[h] = both(decay) * c_aug + both(jnp.exp(m_loc_last - m_new)) * kv_loc
            m_scr[h:h + 1, :] = m_new

            yc = hh - jnp.mean(hh, axis=-1, keepdims=True)
            yn = yc * lax.rsqrt(jnp.mean(yc * yc, axis=-1, keepdims=True) + M_NORM_EPS)
            cols = slice(h * dv, (h + 1) * dv)
            hbuf[rows, cols] = yn * hg_ref[:, cols] * _sigmoid(o_ref[rows, cols])

    out_ref[...] = h_ref[...] + _dot(hbuf[...].astype(BF16), wout_ref[...])


def _mlstm_layer(h2d, bsz, seq, norm_g, w_in, conv_w, conv_b, b_i, b_f, head_g, w_out, *, tm=1024, lt=512, lc=128):
    t, d = h2d.shape
    nh, dk, dv = M_HEADS, M_QK_DIM, M_V_DIM
    o0 = 2 * nh * dk
    o1 = o0 + nh * dv
    o3 = o1 + 2 * nh
    wqk = w_in[:, :o0].astype(BF16)
    wv = w_in[:, o0:o1].astype(BF16)
    wo = w_in[:, o3:].astype(BF16)
    wg = jnp.pad(w_in[:, o1:o3], ((0, 0), (0, LANES - 2 * nh))).astype(BF16)
    wgt = jnp.transpose(w_in[:, o1:o3]).astype(BF16)
    g2 = norm_g.reshape(1, d)

    qk, v, o, gc, gr = pl.pallas_call(
        _ml_proj_kernel,
        grid=(t // tm,),
        in_specs=[pl.BlockSpec((tm, d), lambda i: (i, 0)), _const_spec((1, d)),
                  _const_spec(wqk.shape), _const_spec(wv.shape), _const_spec(wo.shape),
                  _const_spec(wg.shape), _const_spec(wgt.shape)],
        out_specs=[pl.BlockSpec((tm, o0), lambda i: (i, 0)), pl.BlockSpec((tm, nh * dv), lambda i: (i, 0)),
                   pl.BlockSpec((tm, nh * dv), lambda i: (i, 0)), pl.BlockSpec((tm, LANES), lambda i: (i, 0)),
                   pl.BlockSpec((2 * nh, tm), lambda i: (0, i))],
        out_shape=[jax.ShapeDtypeStruct((t, o0), F32), jax.ShapeDtypeStruct((t, nh * dv), BF16),
                   jax.ShapeDtypeStruct((t, nh * dv), F32), jax.ShapeDtypeStruct((t, LANES), F32),
                   jax.ShapeDtypeStruct((2 * nh, t), F32)],
        compiler_params=_cparams("parallel"),
        name="ml_proj",
    )(h2d, g2, wqk, wv, wo, wg, wgt)

    bcol = jnp.pad(jnp.concatenate([b_i, b_f]), (0, LANES - 2 * nh)).reshape(1, LANES)
    brow = jnp.concatenate([b_i, b_f]).reshape(2 * nh, 1)
    nt = seq // lt
    rep = (jnp.arange(3 * LANES)[:, None] % LANES == jnp.arange(2 * nh * LANES)[None, :] // LANES).astype(BF16)
    row_spec = lambda w: pl.BlockSpec((lt, w), lambda b, i: (b * nt + i, 0))
    return pl.pallas_call(
        functools.partial(_ml_kernel, lt=lt, lc=lc),
        grid=(bsz, nt),
        in_specs=[row_spec(o0), row_spec(nh * dv), row_spec(nh * dv), row_spec(LANES),
                  pl.BlockSpec((2 * nh, lt), lambda b, i: (0, b * nt + i)), row_spec(d),
                  _const_spec((M_CONV, o0)), _const_spec((1, o0)), _const_spec((1, LANES)),
                  _const_spec((2 * nh, 1)), _const_spec((1, nh * dv)), _const_spec((nh * dv, d)),
                  _const_spec(rep.shape)],
        out_specs=row_spec(d),
        out_shape=jax.ShapeDtypeStruct((t, d), F32),
        scratch_shapes=[pltpu.VMEM((lt + 8, o0), F32), pltpu.VMEM((nh, LANES, 2 * LANES), F32),
                        pltpu.VMEM((nh, LANES), F32), pltpu.VMEM((lt, nh * dv), F32)],
        compiler_params=_cparams("parallel", "arbitrary"),
        name="ml_mix",
    )(qk, v, o, gc, gr, h2d, conv_w, conv_b.reshape(1, o0), bcol, brow,
      head_g.reshape(1, nh * dv), w_out.astype(BF16), rep)


S5_GROUP_CH = 16
S5_STATE = 64
S5_CHUNK_GROUPS = 4
S5_CHUNK = S5_CHUNK_GROUPS * S5_STATE


def _s5_disc_kernel(ar_ref, ai_ref, ldt_ref, arx_ref, aix_ref, br_ref, bi_ref,
                    abr_ref, abi_ref, bbr_ref, bbi_ref):
    dt = jnp.exp(ldt_ref[...])

    def abar(ar, ai):
        mag = jnp.exp(dt * ar)
        return mag * jnp.cos(dt * ai), mag * jnp.sin(dt * ai)

    abr_ref[...], abi_ref[...] = abar(ar_ref[...], ai_ref[...])
    ar, ai = arx_ref[...], aix_ref[...]
    xr, xi = abar(ar, ai)
    inv = 1.0 / (ar * ar + ai * ai)
    zr, zi = xr - 1.0, xi
    coef_re = (zr * ar + zi * ai) * inv
    coef_im = (zi * ar - zr * ai) * inv
    br, bi = br_ref[...], bi_ref[...]
    bbr_ref[...] = coef_re * br - coef_im * bi
    bbi_ref[...] = coef_re * bi + coef_im * br


def _s5_kernel(h_ref, g_ref, abr_ref, abi_ref, bsl_ref, csl_ref, dsk_ref, wglu_ref, bglu_ref,
               out_ref, sr_scr, si_scr, *, tt, bsz):
    d = h_ref.shape[1]
    nchunk = sr_scr.shape[0]
    per_slab = nchunk // (d // S5_CHUNK)

    @pl.when(pl.program_id(0) == 0)
    def _():
        sr_scr[...] = jnp.zeros_like(sr_scr)
        si_scr[...] = jnp.zeros_like(si_scr)

    x = h_ref[...]
    u = _rms(x, g_ref[...])
    ub = u.astype(BF16)
    ys = []
    for slab in range(d // S5_CHUNK):
        bu = _dot(ub[:, slab * S5_CHUNK:(slab + 1) * S5_CHUNK], bsl_ref[slab])
        states = []
        for j in range(per_slab):
            c = slab * per_slab + j
            bu_re = bu[:, 2 * j * S5_CHUNK:(2 * j + 1) * S5_CHUNK]
            bu_im = bu[:, (2 * j + 1) * S5_CHUNK:(2 * j + 2) * S5_CHUNK]
            ar, ai = abr_ref[c], abi_ref[c]
            sr, si = sr_scr[c], si_scr[c]
            s_re, s_im = [], []
            for t in range(tt):
                rows = slice(t * bsz, (t + 1) * bsz)
                sr, si = ar * sr - ai * si + bu_re[rows], ar * si + ai * sr + bu_im[rows]
                s_re.append(sr)
                s_im.append(si)
            sr_scr[c] = sr
            si_scr[c] = si
            states += [jnp.concatenate(s_re, axis=0).astype(BF16), jnp.concatenate(s_im, axis=0).astype(BF16)]
        ys.append(_dot(jnp.concatenate(states, axis=1), csl_ref[slab]))
    y = jnp.concatenate(ys, axis=1) + dsk_ref[...] * u
    gelu = 0.5 * y * (1.0 + jnp.tanh(math.sqrt(2.0 / math.pi) * (y + 0.044715 * (y * y * y))))
    z = _dot(gelu.astype(BF16), wglu_ref[...]) + bglu_ref[...]
    out_ref[...] = x + z[:, :d] * _sigmoid(z[:, d:])


def _s5_layer(h2d, bsz, seq, norm_g, a_re, a_im, log_dt, b_re, b_im, c_re, c_im, d_skip, w_glu, b_glu, *, tt=32):
    t, d = h2d.shape
    ng, ns, hg = a_re.shape[0], S5_STATE, S5_GROUP_CH
    cg = S5_CHUNK_GROUPS
    nchunk = ng // cg
    per_slab = S5_CHUNK // (cg * hg)

    rep = lambda a: jnp.repeat(a, hg, axis=1)
    abr, abi, bbr, bbi = pl.pallas_call(
        _s5_disc_kernel,
        out_shape=[jax.ShapeDtypeStruct((ng, ns), F32)] * 2 + [jax.ShapeDtypeStruct((ng, ns * hg), F32)] * 2,
        name="s5_disc",
    )(a_re, a_im, log_dt.reshape(ng, 1), rep(a_re), rep(a_im), b_re.reshape(ng, ns * hg), b_im.reshape(ng, ns * hg))

    eye = jnp.eye(cg, dtype=F32)
    place = jax.nn.one_hot(jnp.arange(nchunk) % per_slab, per_slab, dtype=F32)

    def b_blocks(bb):
        m = bb.reshape(nchunk, cg, ns, hg)
        blk = jnp.einsum('cgph,gq->cghqp', m, eye).reshape(nchunk, cg * hg, cg * ns)
        return jnp.einsum('crs,cj->cjrs', blk, place).reshape(nchunk, per_slab * cg * hg, cg * ns).astype(BF16)

    def c_blocks(cc):
        m = cc.reshape(nchunk, cg, hg, ns)
        blk = jnp.einsum('cghp,gq->cgpqh', m, eye).reshape(nchunk, cg * ns, cg * hg)
        return jnp.einsum('csr,cj->csjr', blk, place).reshape(nchunk, cg * ns, per_slab * cg * hg).astype(BF16)

    rows = tt * bsz
    nslab = nchunk // per_slab
    bsl = jnp.stack([b_blocks(bbr), b_blocks(bbi)], axis=2).reshape(nslab, per_slab, S5_CHUNK, 2 * S5_CHUNK)
    bsl = bsl.transpose(0, 2, 1, 3).reshape(nslab, S5_CHUNK, per_slab * 2 * S5_CHUNK)
    csl = jnp.stack([c_blocks(c_re), -c_blocks(c_im)], axis=1).reshape(nslab, per_slab * 2 * S5_CHUNK, S5_CHUNK)
    ht = h2d.reshape(bsz, seq, d).transpose(1, 0, 2).reshape(t, d)
    out = pl.pallas_call(
        functools.partial(_s5_kernel, tt=tt, bsz=bsz),
        grid=(seq // tt,),
        in_specs=[pl.BlockSpec((rows, d), lambda i: (i, 0)), _const_spec((1, d)),
                  _const_spec((nchunk, 1, S5_CHUNK)), _const_spec((nchunk, 1, S5_CHUNK)),
                  _const_spec(bsl.shape), _const_spec(csl.shape),
                  _const_spec((1, d)), _const_spec((d, 2 * d)), _const_spec((1, 2 * d))],
        out_specs=pl.BlockSpec((rows, d), lambda i: (i, 0)),
        out_shape=jax.ShapeDtypeStruct((t, d), F32),
        scratch_shapes=[pltpu.VMEM((nchunk, bsz, S5_CHUNK), F32), pltpu.VMEM((nchunk, bsz, S5_CHUNK), F32)],
        compiler_params=_cparams("arbitrary"),
        name="s5_mix",
    )(ht, norm_g.reshape(1, d), abr.reshape(nchunk, 1, S5_CHUNK), abi.reshape(nchunk, 1, S5_CHUNK),
      bsl, csl, d_skip.reshape(1, d), w_glu.astype(BF16), b_glu.reshape(1, 2 * d))
    return out.reshape(seq, bsz, d).transpose(1, 0, 2).reshape(t, d)


F_HEADS = 16
F_HEAD_DIM = 64
F_DIAG_ROWS = 128
LOG2E = math.log2(math.e)


def _fox_proj_kernel(h_ref, g_ref, wq_ref, wk_ref, wv_ref, wo_ref, wgt_ref,
                     q_ref, k_ref, v_ref, o_ref, gr_ref):
    xn = _rms(h_ref[...], g_ref[...]).astype(BF16)
    q_ref[...] = (_dot(xn, wq_ref[...]) * (F_HEAD_DIM ** -0.5 * LOG2E)).astype(BF16)
    k_ref[...] = _dot(xn, wk_ref[...]).astype(BF16)
    v_ref[...] = _dot(xn, wv_ref[...]).astype(BF16)
    o_ref[...] = _dot(xn, wo_ref[...])
    gr_ref[...] = _dot_nt(wgt_ref[...], xn)


def _fox_kernel(q_ref, k_ref, v_ref, o_ref, gr_ref, h_ref, brow_ref, wout_ref,
                out_ref, cr_scr, obuf, m_scr, l_scr, acc_scr, *, tq, tk, seq):
    nh, dh = F_HEADS, F_HEAD_DIM
    i = pl.program_id(1)
    lane = lax.broadcasted_iota(jnp.int32, (1, LANES), 1)
    hmask = [lane < dh, lane >= dh]

    @pl.when(i == 0)
    def _():
        r_i = lax.broadcasted_iota(jnp.int32, (LANES, LANES), 0)
        c_i = lax.broadcasted_iota(jnp.int32, (LANES, LANES), 1)
        tri_u = (r_i <= c_i).astype(BF16)
        carry_r = jnp.zeros((nh, 1), F32)
        per = tk // LANES
        for blk in range(seq // LANES):
            rows = slice(blk * LANES, (blk + 1) * LANES)
            cr = _dot_exact_r(_log_sigmoid(gr_ref[:, rows] + brow_ref[...]), tri_u) + carry_r
            cr_scr[blk // per, :, (blk % per) * LANES:(blk % per + 1) * LANES] = cr * LOG2E
            carry_r = cr[:, LANES - 1:LANES]

    m_scr[...] = jnp.full(m_scr.shape, NEG_BIG, F32)
    l_scr[...] = jnp.zeros_like(l_scr)
    acc_scr[...] = jnp.zeros_like(acc_scr)

    def step(j, qrow0, nrows, width, masked):
        k0 = pl.multiple_of(j * tk, tk)
        crow = cr_scr[j][:, :width]
        qrows = slice(qrow0, qrow0 + nrows)
        if masked:
            causal = (lax.broadcasted_iota(jnp.int32, (nrows, width), 1)
                      <= lax.broadcasted_iota(jnp.int32, (nrows, width), 0) + qrow0)
        for p in range(nh // 2):
            cols = slice(p * LANES, (p + 1) * LANES)
            kp = k_ref[pl.ds(k0, width), cols]
            vp = v_ref[pl.ds(k0, width), cols]
            qp = q_ref[qrows, cols]
            q2 = jnp.concatenate([jnp.where(hmask[e], qp, jnp.zeros_like(qp)) for e in range(2)], axis=0)
            s2 = _dot_nt(q2, kp)
            dm, ls, pms = [], [], []
            for e in range(2):
                h = 2 * p + e
                s = s2[e * nrows:(e + 1) * nrows] - crow[h:h + 1, :]
                if masked:
                    s = jnp.where(causal, s, NEG_BIG)
                m_prev = m_scr[h, qrows, :]
                m_next = jnp.maximum(m_prev, jnp.max(s, axis=-1, keepdims=True))
                m_scr[h, qrows, :] = m_next
                pm = jnp.exp2(s - jnp.concatenate([m_next] * (width // LANES), axis=1))
                dm.append(m_prev - m_next)
                ls.append(jnp.sum(pm, axis=-1, keepdims=True))
                pms.append(pm.astype(BF16))
            pv = _dot(jnp.concatenate(pms, axis=0), vp)
            alpha = jnp.exp2(jnp.where(hmask[0], dm[0], dm[1]))
            l_scr[p, qrows, :] = alpha * l_scr[p, qrows, :] + jnp.where(hmask[0], ls[0], ls[1])
            acc_scr[p, qrows, :] = alpha * acc_scr[p, qrows, :] + jnp.where(hmask[0], pv[:nrows], pv[nrows:])

    def body(j, carry):
        step(j, 0, tq, tk, False)
        return carry

    lax.fori_loop(0, i, body, 0)
    for r in range(tq // F_DIAG_ROWS):
        step(i, r * F_DIAG_ROWS, F_DIAG_ROWS, (r + 1) * F_DIAG_ROWS, True)
    for p in range(nh // 2):
        cols = slice(p * LANES, (p + 1) * LANES)
        obuf[:, cols] = acc_scr[p] / l_scr[p] * _sigmoid(o_ref[:, cols])

    out_ref[...] = h_ref[...] + _dot(obuf[...].astype(BF16), wout_ref[...])


def _fox_layer(h2d, bsz, seq, norm_g, w_in, b_f, w_out, *, tm=1024, tq=256):
    t, d = h2d.shape
    nh, dh = F_HEADS, F_HEAD_DIM
    hd = nh * dh
    wq, wk, wv = (w_in[:, j * hd:(j + 1) * hd].astype(BF16) for j in range(3))
    wf = w_in[:, 3 * hd:3 * hd + nh]
    wo = w_in[:, 3 * hd + nh:].astype(BF16)
    wgt = jnp.transpose(wf).astype(BF16)
    tile = lambda w: pl.BlockSpec((tm, w), lambda i: (i, 0))
    q, k, v, o, gr = pl.pallas_call(
        _fox_proj_kernel,
        grid=(t // tm,),
        in_specs=[tile(d), _const_spec((1, d)), _const_spec(wq.shape), _const_spec(wk.shape),
                  _const_spec(wv.shape), _const_spec(wo.shape), _const_spec(wgt.shape)],
        out_specs=[tile(hd), tile(hd), tile(hd), tile(hd), pl.BlockSpec((nh, tm), lambda i: (0, i))],
        out_shape=[jax.ShapeDtypeStruct((t, hd), BF16)] * 3 + [jax.ShapeDtypeStruct((t, hd), F32),
                   jax.ShapeDtypeStruct((nh, t), F32)],
        compiler_params=_cparams("parallel"),
        name="fox_proj",
    )(h2d, norm_g.reshape(1, d), wq, wk, wv, wo, wgt)

    nt = seq // tq
    q_spec = lambda w: pl.BlockSpec((tq, w), lambda b, i: (b * nt + i, 0))
    seq_spec = lambda w: pl.BlockSpec((seq, w), lambda b, i: (b, 0))
    return pl.pallas_call(
        functools.partial(_fox_kernel, tq=tq, tk=tq, seq=seq),
        grid=(bsz, nt),
        in_specs=[q_spec(hd), seq_spec(hd), seq_spec(hd), q_spec(hd),
                  pl.BlockSpec((nh, seq), lambda b, i: (0, b)), q_spec(d),
                  _const_spec((nh, 1)), _const_spec((hd, d))],
        out_specs=q_spec(d),
        out_shape=jax.ShapeDtypeStruct((t, d), F32),
        scratch_shapes=[pltpu.VMEM((nt, nh, tq), F32), pltpu.VMEM((tq, hd), F32),
                        pltpu.VMEM((nh, tq, LANES), F32), pltpu.VMEM((nh // 2, tq, LANES), F32),
                        pltpu.VMEM((nh // 2, tq, LANES), F32)],
        compiler_params=_cparams("parallel", "arbitrary"),
        name="fox_attn",
    )(q, k, v, o, gr, h2d, b_f.reshape(nh, 1), w_out.astype(BF16))


R_HEADS = 16
R_HEAD_DIM = 64
R_LN_EPS = 64e-5
R_SUB = 16
R_HALF = R_SUB // 2


def _head_sum(x, lane):
    lo = lane < R_HEAD_DIM
    s0 = jnp.sum(jnp.where(lo, x, 0.0), axis=-1, keepdims=True)
    s1 = jnp.sum(jnp.where(lo, 0.0, x), axis=-1, keepdims=True)
    return jnp.where(lo, s0, s1)


def _rw_proj_kernel(h_ref, g_ref, mu_ref, wr_ref, wk_ref, wv_ref, wd_ref, wa_ref, wg_ref, wup_ref, aup_ref,
                    gup_ref, w0_ref, a0_ref, kk_ref, ka_ref,
                    r_ref, lw_ref, k_ref, v_ref, al_ref, be_ref, g_out_ref, ext_scr, *, tm, seq):
    d = h_ref.shape[1]
    i = pl.program_id(0)
    xn = _rms(h_ref[...], g_ref[...])
    @pl.when((i * tm) % seq == 0)
    def _():
        ext_scr[7:8, :] = jnp.zeros((1, d), F32)

    @pl.when((i * tm) % seq != 0)
    def _():
        ext_scr[7:8, :] = ext_scr[7 + tm:8 + tm, :]

    ext_scr[8:8 + tm, :] = xn
    xb = xn.astype(BF16)
    dxb = (ext_scr[7:7 + tm, :] - xn).astype(BF16)
    mub = mu_ref[...].astype(BF16)
    mix = lambda j: xb + dxb * mub[j:j + 1, :]
    r_ref[...] = _dot(mix(0), wr_ref[...]).astype(r_ref.dtype)
    w_in = w0_ref[...] + _dot(jnp.tanh(_dot(mix(1), wd_ref[...])).astype(BF16), wup_ref[...])
    lw_ref[...] = -math.exp(-0.5) * _sigmoid(w_in)
    k = _dot(mix(2), wk_ref[...])
    v_ref[...] = _dot(mix(3), wv_ref[...]).astype(v_ref.dtype)
    a = _sigmoid(a0_ref[...] + _dot(_dot(mix(4), wa_ref[...]).astype(BF16), aup_ref[...]))
    g_out_ref[...] = _dot(_sigmoid(_dot(mix(5), wg_ref[...])).astype(BF16), gup_ref[...]).astype(g_out_ref.dtype)
    k_ref[...] = (k * (1.0 + (a - 1.0) * ka_ref[...])).astype(k_ref.dtype)
    lane = lax.broadcasted_iota(jnp.int32, (1, LANES), 1)
    for p in range(d // LANES):
        cols = slice(p * LANES, (p + 1) * LANES)
        kk = k[:, cols] * kk_ref[:, cols]
        kk = kk * jnp.minimum(lax.rsqrt(_head_sum(kk * kk, lane)), 1e12)
        al_ref[:, cols] = kk.astype(al_ref.dtype)
        be_ref[:, cols] = (kk * a[:, cols]).astype(be_ref.dtype)


def _head_sum_mxu(x, ones_bd):
    hi = x.astype(BF16)
    lo = (x - hi.astype(F32)).astype(BF16)
    return _dot(jnp.concatenate([hi, lo], axis=1), ones_bd)


def _rw_kernel(r_ref, lw_ref, k_ref, v_ref, al_ref, be_ref, g_ref, h_ref, rk_ref, lng_ref, lnb_ref, wout_ref,
               sel_ref, out_ref, ht_scr, ybuf, aab_scr, aoff_scr, arkb_scr, x_scr, coef_scr, *, lt, lc):
    d = h_ref.shape[1]
    dh = R_HEAD_DIM
    npair = d // LANES
    nb = lc // R_SUB

    @pl.when(pl.program_id(1) == 0)
    def _():
        ht_scr[...] = jnp.zeros_like(ht_scr)

    lane = lax.broadcasted_iota(jnp.int32, (1, LANES), 1)
    hmask = [lane < dh, lane >= dh]
    lane2 = lax.broadcasted_iota(jnp.int32, (1, 2 * LANES), 1) % LANES
    hmask2 = [lane2 < dh, lane2 >= dh]
    r_i = lax.broadcasted_iota(jnp.int32, (lc, lc), 0)
    c_i = lax.broadcasted_iota(jnp.int32, (lc, lc), 1)
    tri_l = (c_i <= r_i).astype(BF16)
    r_p2 = lax.broadcasted_iota(jnp.int32, (lc, LANES), 0)
    c_p2 = lax.broadcasted_iota(jnp.int32, (lc, LANES), 1) % lc
    strict2 = c_p2 < r_p2
    incl2 = c_p2 <= r_p2
    below2 = (c_p2 // R_SUB) < (r_p2 // R_SUB)
    rr = lax.broadcasted_iota(jnp.int32, (LANES, LANES), 0)
    cc = lax.broadcasted_iota(jnp.int32, (LANES, LANES), 1)
    bdiag = (rr < dh) == (cc < dh)

    def chunk(c, carry):
        rows = slice(c * lc, (c + 1) * lc)
        lw = lw_ref[rows, :]
        cum = _dot_exact_l(tri_l, lw)
        cum_last = cum[lc - 1:lc, :]
        e_pos = jnp.exp(cum)
        e_neg = jnp.exp(-cum)
        e_rem = jnp.exp(cum_last - cum)
        r_t = r_ref[rows, :] * e_pos
        a_t = al_ref[rows, :] * jnp.exp(cum - lw)
        k_all = k_ref[rows, :]
        b_all = be_ref[rows, :]
        k_t, b_t = k_all * e_neg, b_all * e_neg
        k_h, b_h = k_all * e_rem, b_all * e_rem
        gam = jnp.exp(cum_last)

        nheads = 2 * npair
        vps = [v_ref[rows, p * LANES:(p + 1) * LANES].astype(BF16) for p in range(npair)]
        vp2s = [jnp.concatenate([vp, vp], axis=0) for vp in vps]

        scs = []
        for p in range(npair):
            cols = slice(p * LANES, (p + 1) * LANES)
            lhs = jnp.concatenate([a_t[:, cols], r_t[:, cols]], axis=0).astype(BF16)
            b_p, k_p = b_t[:, cols], k_t[:, cols]
            rhs = jnp.concatenate([jnp.where(hmask[0], b_p, 0.0), jnp.where(hmask[1], b_p, 0.0),
                                   jnp.where(hmask[0], k_p, 0.0), jnp.where(hmask[1], k_p, 0.0)], axis=0)
            scs.append(_dot_nt(lhs, rhs.astype(BF16)))
        aaks = []
        for p in range(npair):
            sc = scs[p]
            top_b, top_k = sc[:lc, :LANES], sc[:lc, LANES:]
            bot_b, bot_k = sc[lc:, :LANES], sc[lc:, LANES:]
            aab_scr[p] = jnp.where(strict2, top_b, 0.0)
            aoff_scr[p] = jnp.where(below2, top_b, 0.0).astype(BF16)
            arkb_scr[p] = jnp.concatenate([jnp.where(incl2, bot_k, 0.0), jnp.where(incl2, bot_b, 0.0)],
                                          axis=1).astype(BF16)
            a_ak = jnp.where(strict2, top_k, 0.0)
            aaks.append(jnp.concatenate([jnp.where(hmask[0], a_ak, 0.0), jnp.where(hmask[1], a_ak, 0.0)],
                                        axis=0).astype(BF16))
        for p in range(npair):
            cols = slice(p * LANES, (p + 1) * LANES)
            av = _dot(aaks[p], vp2s[p])
            for e in range(2):
                av_e = jnp.where(hmask[e], av[e * lc:(e + 1) * lc], 0.0)
                x_scr[2 * p + e] = jnp.where(hmask[e], a_t[:, cols], 0.0) + pltpu.roll(av_e, dh, axis=1)

        for bi in range(nb):
            brows = slice(bi * R_SUB, (bi + 1) * R_SUB)
            a_blk = aab_scr[:, brows, :].reshape(npair * R_SUB, LANES)
            a_hi = a_blk.astype(BF16)
            a_lo = (a_blk - a_hi.astype(F32)).astype(BF16)
            coef_scr[...] = _dot(jnp.concatenate([a_hi, a_lo], axis=1), sel_ref[bi])
            xs = [x_scr[h, brows, :] for h in range(nheads)]
            if bi > 0:
                for p in range(npair):
                    slab = aoff_scr[p, brows, :]
                    lhs = jnp.concatenate([jnp.where(hmask[0], slab, jnp.zeros_like(slab)),
                                           jnp.where(hmask[1], slab, jnp.zeros_like(slab))], axis=0)
                    xst = x_scr[2 * p:2 * p + 2].reshape(2 * lc, LANES).astype(BF16)
                    corr = _dot(lhs, xst)
                    for e in range(2):
                        xs[2 * p + e] = xs[2 * p + e] - corr[e * R_SUB:(e + 1) * R_SUB]
            x_lo = [x[:R_HALF] for x in xs]
            x_hi = [x[R_HALF:] for x in xs]
            for s in range(R_HALF - 1):
                for h in range(nheads):
                    p, e = h // 2, h % 2
                    tile = e * (R_HALF - 1) + s
                    coef = coef_scr[p * R_SUB:(p + 1) * R_SUB, tile * LANES:(tile + 1) * LANES]
                    row = x_lo[h][s:s + 1, :]
                    x_lo[h] = x_lo[h] - coef[:R_HALF] * row
                    x_hi[h] = x_hi[h] - coef[R_HALF:] * row
            for s in range(R_HALF - 1, R_SUB - 1):
                for h in range(nheads):
                    p, e = h // 2, h % 2
                    col = e * lc + bi * R_SUB + s
                    coef = aab_scr[p, bi * R_SUB + R_HALF:(bi + 1) * R_SUB, col:col + 1]
                    row = x_lo[h][s:s + 1, :] if s < R_HALF else x_hi[h][s - R_HALF:s - R_HALF + 1, :]
                    x_hi[h] = x_hi[h] - coef * row
            for h in range(nheads):
                x_scr[h, bi * R_SUB:bi * R_SUB + R_HALF, :] = x_lo[h]
                x_scr[h, bi * R_SUB + R_HALF:(bi + 1) * R_SUB, :] = x_hi[h]

        hts, ubs, yhs = [], [], []
        for p in range(npair):
            cols = slice(p * LANES, (p + 1) * LANES)
            x0, x1 = x_scr[2 * p], x_scr[2 * p + 1]
            w_pair = jnp.where(hmask[0], x0, x1)
            uv_pair = pltpu.roll(jnp.where(hmask[0], x1, x0), dh, axis=1)
            ht = ht_scr[p]
            hts.append(ht)
            wr = jnp.concatenate([w_pair, r_t[:, cols]], axis=0).astype(BF16)
            wh = _dot_nt(wr, ht.astype(BF16))
            ubs.append((wh[:lc] + uv_pair).astype(BF16))
            yhs.append(wh[lc:])
        for p in range(npair):
            cols = slice(p * LANES, (p + 1) * LANES)
            arkb = arkb_scr[p]
            lhs = jnp.concatenate([jnp.where(hmask2[e], arkb, jnp.zeros_like(arkb)) for e in range(2)], axis=0)
            rhs = jnp.concatenate([vp2s[p], -ubs[p], -ubs[p]], axis=0)
            intra = _dot(lhs, rhs)
            ybuf[rows, cols] = yhs[p] + jnp.where(hmask[0], intra[:lc], intra[lc:])
        for p in range(npair):
            cols = slice(p * LANES, (p + 1) * LANES)
            vu = jnp.concatenate([vps[p], -ubs[p]], axis=0)
            kb = jnp.concatenate([k_h[:, cols], b_h[:, cols]], axis=0).astype(BF16)
            upd = lax.dot_general(vu, kb, (((0,), (0,)), ((), ())), preferred_element_type=F32)
            ht_scr[p] = hts[p] * gam[:, cols] + jnp.where(bdiag, upd, 0.0)
        return carry

    for c in range(lt // lc):
        chunk(c, 0)

    rr2 = lax.broadcasted_iota(jnp.int32, (2 * LANES, LANES), 0)
    cc2 = lax.broadcasted_iota(jnp.int32, (2 * LANES, LANES), 1)
    ones_bd = (((rr2 % LANES) < dh) == (cc2 < dh)).astype(BF16)
    for p in range(npair):
        cols = slice(p * LANES, (p + 1) * LANES)
        y = ybuf[:, cols]
        r, k, v = r_ref[:, cols], k_ref[:, cols], v_ref[:, cols]
        bonus = _head_sum(r * k * rk_ref[:, cols], lane) * v
        yc = y - _head_sum_mxu(y, ones_bd) * (1.0 / dh)
        yn = yc * lax.rsqrt(_head_sum_mxu(yc * yc, ones_bd) * (1.0 / dh) + R_LN_EPS)
        ybuf[:, cols] = (yn * lng_ref[:, cols] + lnb_ref[:, cols] + bonus) * g_ref[:, cols]
    out_ref[...] = h_ref[...] + _dot(ybuf[...].astype(BF16), wout_ref[...])


def _rwkv_layer(h2d, bsz, seq, norm_g, mu, w_in, w0, w_up, a0, a_up, g_up, k_k, k_a, r_k, ln_g, ln_b, w_out,
                *, tm=512, lt=512, lc=64, act_dtype=F32):
    t, d = h2d.shape
    c0 = 3 * d
    c1 = c0 + w_up.shape[0]
    c2 = c1 + a_up.shape[0]
    pad_c = lambda w, n: jnp.pad(w, ((0, 0), (0, n - w.shape[1]))).astype(BF16)
    pad_r = lambda w, n: jnp.pad(w, ((0, n - w.shape[0]), (0, 0))).astype(BF16)
    up = lambda n: -(-n // LANES) * LANES
    nw, na, ng = up(c1 - c0), up(c2 - c1), up(w_in.shape[1] - c2)
    wr, wk, wv = (w_in[:, j * d:(j + 1) * d].astype(BF16) for j in range(3))
    wd, wa, wg = pad_c(w_in[:, c0:c1], nw), pad_c(w_in[:, c1:c2], na), pad_c(w_in[:, c2:], ng)
    row = lambda a: a.reshape(1, d)
    tile = pl.BlockSpec((tm, d), lambda i: (i, 0))
    outs = pl.pallas_call(
        functools.partial(_rw_proj_kernel, tm=tm, seq=seq),
        grid=(t // tm,),
        in_specs=[tile, _const_spec((1, d)), _const_spec((6, d)), _const_spec((d, d)), _const_spec((d, d)),
                  _const_spec((d, d)), _const_spec((d, nw)), _const_spec((d, na)), _const_spec((d, ng)),
                  _const_spec((nw, d)), _const_spec((na, d)), _const_spec((ng, d)),
                  _const_spec((1, d)), _const_spec((1, d)), _const_spec((1, d)), _const_spec((1, d))],
        out_specs=[tile] * 7,
        out_shape=[jax.ShapeDtypeStruct((t, d), F32 if name == "lw" else act_dtype)
                   for name in ("r", "lw", "k", "v", "al", "be", "g")],
        scratch_shapes=[pltpu.VMEM((tm + 8, d), F32)],
        compiler_params=_cparams("arbitrary"),
        name="rw_proj",
    )(h2d, row(norm_g), mu, wr, wk, wv, wd, wa, wg, pad_r(w_up, nw), pad_r(a_up, na), pad_r(g_up, ng),
      row(w0), row(a0), row(k_k), row(k_a))
    r, lw, k, v, al, be, g = outs

    nt = seq // lt
    assert lc == R_HEAD_DIM, "pair tiles put the two heads' (lc, lc) matrices side by side on 128 lanes"
    nb, nmx = lc // R_SUB, R_HALF - 1
    nsel = 2 * nmx * LANES
    klane = jnp.arange(2 * LANES)[None, :, None] % LANES
    otile = jnp.arange(nsel)[None, None, :] // LANES
    sel = ((klane // lc == otile // nmx)
           & (klane % lc == jnp.arange(nb)[:, None, None] * R_SUB + otile % nmx)).astype(BF16)
    spec = pl.BlockSpec((lt, d), lambda b, i: (b * nt + i, 0))
    return pl.pallas_call(
        functools.partial(_rw_kernel, lt=lt, lc=lc),
        grid=(bsz, nt),
        in_specs=[spec] * 8 + [_const_spec((1, d))] * 3 + [_const_spec((d, d)), _const_spec(sel.shape)],
        out_specs=spec,
        out_shape=jax.ShapeDtypeStruct((t, d), F32),
        scratch_shapes=[pltpu.VMEM((d // LANES, LANES, LANES), F32), pltpu.VMEM((lt, d), F32),
                        pltpu.VMEM((d // LANES, lc, LANES), F32), pltpu.VMEM((d // LANES, lc, LANES), BF16),
                        pltpu.VMEM((d // LANES, lc, 2 * LANES), BF16),
                        pltpu.VMEM((R_HEADS, lc, LANES), F32), pltpu.VMEM((d // LANES * R_SUB, nsel), F32)],
        compiler_params=_cparams("parallel", "arbitrary"),
        name="rw_mix",
    )(r, lw, k, v, al, be, g, h2d, r_k.reshape(1, d), row(ln_g), row(ln_b), w_out.astype(BF16), sel)


def kernel(x, mlp_norm_g, mlp_w_up, mlp_w_down, final_norm_g, s5_norm_g, s5_a_re, s5_a_im, s5_log_dt, s5_b_re, s5_b_im, s5_c_re, s5_c_im, s5_d, s5_w_glu, s5_b_glu, ml_norm_g, ml_w_in, ml_conv_w, ml_conv_b, ml_b_i, ml_b_f, ml_head_g, ml_w_out, fox_norm_g, fox_w_in, fox_b_f, fox_w_out, rw_norm_g, rw_mu, rw_w_in, rw_w0, rw_w_up, rw_a0, rw_a_up, rw_g_up, rw_k_k, rw_k_a, rw_r_k, rw_ln_g, rw_ln_b, rw_w_out):
    bsz, seq, d = x.shape
    depth = mlp_w_up.shape[0]
    h = x.reshape(bsz * seq, d)
    for i in range(depth):
        kind, j = i % 4, i // 4
        if kind == 0:
            h = _s5_layer(h, bsz, seq, s5_norm_g[j], s5_a_re[j], s5_a_im[j], s5_log_dt[j], s5_b_re[j], s5_b_im[j],
                          s5_c_re[j], s5_c_im[j], s5_d[j], s5_w_glu[j], s5_b_glu[j])
        elif kind == 1:
            h = _mlstm_layer(h, bsz, seq, ml_norm_g[j], ml_w_in[j], ml_conv_w[j], ml_conv_b[j], ml_b_i[j],
                             ml_b_f[j], ml_head_g[j], ml_w_out[j])
        elif kind == 2:
            h = _fox_layer(h, bsz, seq, fox_norm_g[j], fox_w_in[j], fox_b_f[j], fox_w_out[j])
        else:
            h = _rwkv_layer(h, bsz, seq, rw_norm_g[j], rw_mu[j], rw_w_in[j], rw_w0[j], rw_w_up[j], rw_a0[j],
                            rw_a_up[j], rw_g_up[j], rw_k_k[j], rw_k_a[j], rw_r_k[j], rw_ln_g[j], rw_ln_b[j],
                            rw_w_out[j])
        h = _mlp_layer(h, mlp_norm_g[i], mlp_w_up[i], mlp_w_down[i],
                       final_g=final_norm_g if i == depth - 1 else None)
    return h.reshape(bsz, seq, d)
```

```python
import functools
import math

import jax
import jax.numpy as jnp
from jax import lax
from jax.experimental import pallas as pl
from jax.experimental.pallas import tpu as pltpu

F32 = jnp.float32
BF16 = jnp.bfloat16

NORM_EPS = 1e-6
NEG_BIG = -1e30
LANES = 128
VMEM_LIMIT = 56 * 1024 * 1024

M_HEADS = 8
M_QK_DIM = 64
M_V_DIM = 128
M_CONV = 4
M_NORM_EPS = 1e-6


def _cparams(*sem):
    return pltpu.CompilerParams(dimension_semantics=sem, vmem_limit_bytes=VMEM_LIMIT)


def _rms(x, g):
    return x * lax.rsqrt(jnp.mean(x * x, axis=-1, keepdims=True) + NORM_EPS) * g


def _dot(a, b):
    return jnp.dot(a, b, preferred_element_type=F32)


def _dot_nt(a, b):
    return lax.dot_general(a, b, (((1,), (1,)), ((), ())), preferred_element_type=F32)


def _split3(x):
    hi = x.astype(BF16)
    r = x - hi.astype(F32)
    mid = r.astype(BF16)
    lo = (r - mid.astype(F32)).astype(BF16)
    return hi, mid, lo


def _dot_exact_l(ones_mat, x):
    hi, mid, lo = _split3(x)
    return _dot(ones_mat, hi) + _dot(ones_mat, mid) + _dot(ones_mat, lo)


def _dot_exact_r(x, ones_mat):
    hi, mid, lo = _split3(x)
    return _dot(hi, ones_mat) + _dot(mid, ones_mat) + _dot(lo, ones_mat)


def _log_sigmoid(x):
    return jnp.minimum(x, 0.0) - jnp.log(1.0 + jnp.exp(-jnp.abs(x)))


def _sigmoid(x):
    return 1.0 / (1.0 + jnp.exp(-x))


def _const_spec(shape):
    nd = len(shape)
    return pl.BlockSpec(shape, lambda *_: (0,) * nd)


def _mlp_kernel(h_ref, g_ref, wup_ref, wdn_ref, fg_ref, out_ref, *, fc, final):
    x = h_ref[...]
    xn = _rms(x, g_ref[...]).astype(BF16)
    acc = x
    for c in range(wup_ref.shape[1] // fc):
        hid = jnp.maximum(_dot(xn, wup_ref[:, c * fc:(c + 1) * fc]), 0.0)
        acc = acc + _dot((hid * hid).astype(BF16), wdn_ref[c * fc:(c + 1) * fc, :])
    out_ref[...] = _rms(acc, fg_ref[...]) if final else acc


def _mlp_layer(h2d, norm_g, w_up, w_down, final_g=None, *, tm=1024, fc=512):
    t, d = h2d.shape
    ff = w_up.shape[1]
    fg = (norm_g if final_g is None else final_g).reshape(1, d)
    return pl.pallas_call(
        functools.partial(_mlp_kernel, fc=fc, final=final_g is not None),
        grid=(t // tm,),
        in_specs=[pl.BlockSpec((tm, d), lambda i: (i, 0)), _const_spec((1, d)),
                  _const_spec((d, ff)), _const_spec((ff, d)), _const_spec((1, d))],
        out_specs=pl.BlockSpec((tm, d), lambda i: (i, 0)),
        out_shape=jax.ShapeDtypeStruct((t, d), F32),
        compiler_params=_cparams("parallel"),
        name="mlp",
    )(h2d, norm_g.reshape(1, d), w_up.astype(BF16), w_down.astype(BF16), fg)


def _ml_kernel(h_ref, g_ref, wqk_ref, wv_ref, wo_ref, wg_ref, wgt_ref, cw_ref, cb_ref, bcol_ref, brow_ref,
               hg_ref, wout_ref, rep_ref, out_ref, ext_scr, c_scr, m_scr, hbuf, *, lt, lc):
    nh, dk, dv = M_HEADS, M_QK_DIM, M_V_DIM
    hd = nh * dk

    @pl.when(pl.program_id(1) == 0)
    def _():
        c_scr[...] = jnp.zeros_like(c_scr)
        m_scr[...] = jnp.zeros_like(m_scr)
        ext_scr[0:8, :] = jnp.zeros((8, 2 * hd), F32)

    xn = _rms(h_ref[...], g_ref[...]).astype(BF16)
    v_all = _dot(xn, wv_ref[...]).astype(BF16)
    o_all = _dot(xn, wo_ref[...])
    gc_all = _dot(xn, wg_ref[...])
    gr_all = _dot_nt(wgt_ref[...], xn)

    ext_scr[8:8 + lt, :] = _dot(xn, wqk_ref[...])
    conv = cb_ref[...] + jnp.zeros((lt, 2 * hd), F32)
    for kk in range(M_CONV):
        conv = conv + cw_ref[kk:kk + 1, :] * ext_scr[8 - (M_CONV - 1) + kk:8 - (M_CONV - 1) + kk + lt, :]
    ext_scr[0:8, :] = ext_scr[lt:lt + 8, :]
    act = conv * _sigmoid(conv)

    lane = lax.broadcasted_iota(jnp.int32, (1, LANES), 1)
    r_i = lax.broadcasted_iota(jnp.int32, (lc, lc), 0)
    c_i = lax.broadcasted_iota(jnp.int32, (lc, lc), 1)
    causal = c_i <= r_i
    tri_l = causal.astype(BF16)
    tri_u = (r_i <= c_i).astype(BF16)
    ones_blk = jnp.ones((lc, LANES), BF16)
    row16 = lax.broadcasted_iota(jnp.int32, (2 * nh, 1), 0)

    def both(x):
        return jnp.concatenate([x, x], axis=1)

    local = [[None] * nh for _ in range(lt // lc)]
    for c in range(lt // lc):
        rows = slice(c * lc, (c + 1) * lc)
        gcb = gc_all[rows, :] + bcol_ref[...]
        gate_c = jnp.where(lane < nh, gcb, jnp.where(lane < 2 * nh, _log_sigmoid(gcb), 0.0))
        cum_c = _dot_exact_l(tri_l, jnp.where(lane < nh, 0.0, gate_c))
        hi, mid, lo = _split3(jnp.where(lane < nh, gate_c, cum_c))
        rep = _dot(jnp.concatenate([hi, mid, lo], axis=1), rep_ref[...])
        grb = gr_all[:, rows] + brow_ref[...]
        gate_r = jnp.where(row16 < nh, grb, _log_sigmoid(grb))
        cum_r = _dot_exact_r(jnp.where(row16 < nh, 0.0, gate_r), tri_u)

        for h in range(nh):
            p, e = h // 2, h % 2
            head_mask = (lane >= e * dk) & (lane < (e + 1) * dk)
            q_m = (jnp.where(head_mask, act[rows, p * LANES:(p + 1) * LANES], 0.0) * (dk ** -0.5)).astype(BF16)
            k_m = jnp.where(head_mask, act[rows, hd + p * LANES:hd + (p + 1) * LANES], 0.0)
            v_aug = jnp.concatenate([v_all[rows, h * dv:(h + 1) * dv], ones_blk], axis=1)
            li_c = rep[:, h * LANES:(h + 1) * LANES]
            b_c = rep[:, (nh + h) * LANES:(nh + h + 1) * LANES]
            b_row = cum_r[nh + h:nh + h + 1, :]
            li_row = gate_r[h:h + 1, :]
            d_log = jnp.where(causal, b_c[:, :lc] - b_row + li_row, NEG_BIG)
            m_loc = jnp.max(d_log, axis=-1, keepdims=True) + jnp.zeros((lc, LANES), F32)
            s_loc = _dot_nt(q_m, k_m.astype(BF16)) * jnp.exp(d_log - m_loc[:, :lc])
            r_loc = _dot(s_loc.astype(BF16), v_aug)
            b_last = b_c[lc - 1:lc, :]
            m_loc_last = m_loc[lc - 1:lc, :]
            w_state = jnp.exp(b_last - b_c + li_c - m_loc_last)
            kv_loc = _dot(jnp.transpose(k_m * w_state).astype(BF16), v_aug)
            local[c][h] = (q_m, b_c, m_loc, r_loc, kv_loc, b_last, m_loc_last)

    for c in range(lt // lc):
        rows = slice(c * lc, (c + 1) * lc)
        for h in range(nh):
            q_m, b_c, m_loc, r_loc, kv_loc, b_last, m_loc_last = local[c][h]
            m_prev = m_scr[h:h + 1, :]
            g = b_c + m_prev
            m_t = jnp.maximum(g, m_loc)
            c_aug = c_scr[h]
            r = both(jnp.exp(g - m_t)) * _dot(q_m, c_aug.astype(BF16)) + both(jnp.exp(m_loc - m_t)) * r_loc
            hh = r[:, :dv] / jnp.maximum(jnp.abs(r[:, dv:]), jnp.exp(-m_t))

            m_new = m_t[lc - 1:lc, :]
            decay = jnp.exp(b_last + m_prev - m_new)
            c_scr[h] = both(decay) * c_aug + both(jnp.exp(m_loc_last - m_new)) * kv_loc
            m_scr[h:h + 1, :] = m_new

            yc = hh - jnp.mean(hh, axis=-1, keepdims=True)
            yn = yc * lax.rsqrt(jnp.mean(yc * yc, axis=-1, keepdims=True) + M_NORM_EPS)
            cols = slice(h * dv, (h + 1) * dv)
            hbuf[rows, cols] = yn * hg_ref[:, cols] * _sigmoid(o_all[rows, cols])

    out_ref[...] = h_ref[...] + _dot(hbuf[...].astype(BF16), wout_ref[...])


def _mlstm_layer(h2d, bsz, seq, norm_g, w_in, conv_w, conv_b, b_i, b_f, head_g, w_out, *, lt=512, lc=128):
    t, d = h2d.shape
    nh, dk, dv = M_HEADS, M_QK_DIM, M_V_DIM
    o0 = 2 * nh * dk
    o1 = o0 + nh * dv
    o3 = o1 + 2 * nh
    wqk = w_in[:, :o0].astype(BF16)
    wv = w_in[:, o0:o1].astype(BF16)
    wo = w_in[:, o3:].astype(BF16)
    wg = jnp.pad(w_in[:, o1:o3], ((0, 0), (0, LANES - 2 * nh))).astype(BF16)
    wgt = jnp.transpose(w_in[:, o1:o3]).astype(BF16)
    g2 = norm_g.reshape(1, d)

    bcol = jnp.pad(jnp.concatenate([b_i, b_f]), (0, LANES - 2 * nh)).reshape(1, LANES)
    brow = jnp.concatenate([b_i, b_f]).reshape(2 * nh, 1)
    nt = seq // lt
    rep = (jnp.arange(3 * LANES)[:, None] % LANES == jnp.arange(2 * nh * LANES)[None, :] // LANES).astype(BF16)
    row_spec = lambda w: pl.BlockSpec((lt, w), lambda b, i: (b * nt + i, 0))
    return pl.pallas_call(
        functools.partial(_ml_kernel, lt=lt, lc=lc),
        grid=(bsz, nt),
        in_specs=[row_spec(d), _const_spec((1, d)), _const_spec(wqk.shape), _const_spec(wv.shape),
                  _const_spec(wo.shape), _const_spec(wg.shape), _const_spec(wgt.shape),
                  _const_spec((M_CONV, o0)), _const_spec((1, o0)), _const_spec((1, LANES)),
                  _const_spec((2 * nh, 1)), _const_spec((1, nh * dv)), _const_spec((nh * dv, d)),
                  _const_spec(rep.shape)],
        out_specs=row_spec(d),
        out_shape=jax.ShapeDtypeStruct((t, d), F32),
        scratch_shapes=[pltpu.VMEM((lt + 8, o0), F32), pltpu.VMEM((nh, LANES, 2 * LANES), F32),
                        pltpu.VMEM((nh, LANES), F32), pltpu.VMEM((lt, nh * dv), F32)],
        compiler_params=_cparams("parallel", "arbitrary"),
        name="ml_mix",
    )(h2d, g2, wqk, wv, wo, wg, wgt, conv_w, conv_b.reshape(1, o0), bcol, brow,
      head_g.reshape(1, nh * dv), w_out.astype(BF16), rep)


S5_GROUP_CH = 16
S5_STATE = 64
S5_CHUNK_GROUPS = 4
S5_CHUNK = S5_CHUNK_GROUPS * S5_STATE


def _s5_disc_kernel(ar_ref, ai_ref, ldt_ref, arx_ref, aix_ref, br_ref, bi_ref,
                    abr_ref, abi_ref, bbr_ref, bbi_ref):
    dt = jnp.exp(ldt_ref[...])

    def abar(ar, ai):
        mag = jnp.exp(dt * ar)
        return mag * jnp.cos(dt * ai), mag * jnp.sin(dt * ai)

    abr_ref[...], abi_ref[...] = abar(ar_ref[...], ai_ref[...])
    ar, ai = arx_ref[...], aix_ref[...]
    xr, xi = abar(ar, ai)
    inv = 1.0 / (ar * ar + ai * ai)
    zr, zi = xr - 1.0, xi
    coef_re = (zr * ar + zi * ai) * inv
    coef_im = (zi * ar - zr * ai) * inv
    br, bi = br_ref[...], bi_ref[...]
    bbr_ref[...] = coef_re * br - coef_im * bi
    bbi_ref[...] = coef_re * bi + coef_im * br


def _s5_kernel(h_ref, g_ref, abr_ref, abi_ref, bsl_ref, csl_ref, dsk_ref, wglu_ref, bglu_ref,
               out_ref, sr_scr, si_scr, *, tt, bsz):
    d = h_ref.shape[1]
    nchunk = sr_scr.shape[0]
    per_slab = nchunk // (d // S5_CHUNK)

    @pl.when(pl.program_id(0) == 0)
    def _():
        sr_scr[...] = jnp.zeros_like(sr_scr)
        si_scr[...] = jnp.zeros_like(si_scr)

    x = h_ref[...]
    u = _rms(x, g_ref[...])
    ub = u.astype(BF16)
    ys = []
    for slab in range(d // S5_CHUNK):
        bu = _dot(ub[:, slab * S5_CHUNK:(slab + 1) * S5_CHUNK], bsl_ref[slab])
        states = []
        for j in range(per_slab):
            c = slab * per_slab + j
            bu_re = bu[:, 2 * j * S5_CHUNK:(2 * j + 1) * S5_CHUNK]
            bu_im = bu[:, (2 * j + 1) * S5_CHUNK:(2 * j + 2) * S5_CHUNK]
            ar, ai = abr_ref[c], abi_ref[c]
            sr, si = sr_scr[c], si_scr[c]
            s_re, s_im = [], []
            for t in range(tt):
                rows = slice(t * bsz, (t + 1) * bsz)
                sr, si = ar * sr - ai * si + bu_re[rows], ar * si + ai * sr + bu_im[rows]
                s_re.append(sr)
                s_im.append(si)
            sr_scr[c] = sr
            si_scr[c] = si
            states += [jnp.concatenate(s_re, axis=0).astype(BF16), jnp.concatenate(s_im, axis=0).astype(BF16)]
        ys.append(_dot(jnp.concatenate(states, axis=1), csl_ref[slab]))
    y = jnp.concatenate(ys, axis=1) + dsk_ref[...] * u
    gelu = 0.5 * y * (1.0 + jnp.tanh(math.sqrt(2.0 / math.pi) * (y + 0.044715 * (y * y * y))))
    z = _dot(gelu.astype(BF16), wglu_ref[...]) + bglu_ref[...]
    out_ref[...] = x + z[:, :d] * _sigmoid(z[:, d:])


def _s5_layer(h2d, bsz, seq, norm_g, a_re, a_im, log_dt, b_re, b_im, c_re, c_im, d_skip, w_glu, b_glu, *, tt=32):
    t, d = h2d.shape
    ng, ns, hg = a_re.shape[0], S5_STATE, S5_GROUP_CH
    cg = S5_CHUNK_GROUPS
    nchunk = ng // cg
    per_slab = S5_CHUNK // (cg * hg)

    rep = lambda a: jnp.repeat(a, hg, axis=1)
    abr, abi, bbr, bbi = pl.pallas_call(
        _s5_disc_kernel,
        out_shape=[jax.ShapeDtypeStruct((ng, ns), F32)] * 2 + [jax.ShapeDtypeStruct((ng, ns * hg), F32)] * 2,
        name="s5_disc",
    )(a_re, a_im, log_dt.reshape(ng, 1), rep(a_re), rep(a_im), b_re.reshape(ng, ns * hg), b_im.reshape(ng, ns * hg))

    eye = jnp.eye(cg, dtype=F32)
    place = jax.nn.one_hot(jnp.arange(nchunk) % per_slab, per_slab, dtype=F32)

    def b_blocks(bb):
        m = bb.reshape(nchunk, cg, ns, hg)
        blk = jnp.einsum('cgph,gq->cghqp', m, eye).reshape(nchunk, cg * hg, cg * ns)
        return jnp.einsum('crs,cj->cjrs', blk, place).reshape(nchunk, per_slab * cg * hg, cg * ns).astype(BF16)

    def c_blocks(cc):
        m = cc.reshape(nchunk, cg, hg, ns)
        blk = jnp.einsum('cghp,gq->cgpqh', m, eye).reshape(nchunk, cg * ns, cg * hg)
        return jnp.einsum('csr,cj->csjr', blk, place).reshape(nchunk, cg * ns, per_slab * cg * hg).astype(BF16)

    rows = tt * bsz
    nslab = nchunk // per_slab
    bsl = jnp.stack([b_blocks(bbr), b_blocks(bbi)], axis=2).reshape(nslab, per_slab, S5_CHUNK, 2 * S5_CHUNK)
    bsl = bsl.transpose(0, 2, 1, 3).reshape(nslab, S5_CHUNK, per_slab * 2 * S5_CHUNK)
    csl = jnp.stack([c_blocks(c_re), -c_blocks(c_im)], axis=1).reshape(nslab, per_slab * 2 * S5_CHUNK, S5_CHUNK)
    ht = h2d.reshape(bsz, seq, d).transpose(1, 0, 2).reshape(t, d)
    out = pl.pallas_call(
        functools.partial(_s5_kernel, tt=tt, bsz=bsz),
        grid=(seq // tt,),
        in_specs=[pl.BlockSpec((rows, d), lambda i: (i, 0)), _const_spec((1, d)),
                  _const_spec((nchunk, 1, S5_CHUNK)), _const_spec((nchunk, 1, S5_CHUNK)),
                  _const_spec(bsl.shape), _const_spec(csl.shape),
                  _const_spec((1, d)), _const_spec((d, 2 * d)), _const_spec((1, 2 * d))],
        out_specs=pl.BlockSpec((rows, d), lambda i: (i, 0)),
        out_shape=jax.ShapeDtypeStruct((t, d), F32),
        scratch_shapes=[pltpu.VMEM((nchunk, bsz, S5_CHUNK), F32), pltpu.VMEM((nchunk, bsz, S5_CHUNK), F32)],
        compiler_params=_cparams("arbitrary"),
        name="s5_mix",
    )(ht, norm_g.reshape(1, d), abr.reshape(nchunk, 1, S5_CHUNK), abi.reshape(nchunk, 1, S5_CHUNK),
      bsl, csl, d_skip.reshape(1, d), w_glu.astype(BF16), b_glu.reshape(1, 2 * d))
    return out.reshape(seq, bsz, d).transpose(1, 0, 2).reshape(t, d)


F_HEADS = 16
F_HEAD_DIM = 64
F_DIAG_ROWS = 128
LOG2E = math.log2(math.e)


def _fox_proj_kernel(h_ref, g_ref, wq_ref, wk_ref, wv_ref, wo_ref, wgt_ref,
                     q_ref, k_ref, v_ref, o_ref, gr_ref):
    xn = _rms(h_ref[...], g_ref[...]).astype(BF16)
    q_ref[...] = (_dot(xn, wq_ref[...]) * (F_HEAD_DIM ** -0.5 * LOG2E)).astype(BF16)
    k_ref[...] = _dot(xn, wk_ref[...]).astype(BF16)
    v_ref[...] = _dot(xn, wv_ref[...]).astype(BF16)
    o_ref[...] = _dot(xn, wo_ref[...])
    gr_ref[...] = _dot_nt(wgt_ref[...], xn)


def _fox_kernel(q_ref, k_ref, v_ref, o_ref, gr_ref, h_ref, brow_ref, wout_ref,
                out_ref, cr_scr, obuf, m_scr, l_scr, acc_scr, *, tq, tk, seq):
    nh, dh = F_HEADS, F_HEAD_DIM
    i = pl.program_id(1)
    lane = lax.broadcasted_iota(jnp.int32, (1, LANES), 1)
    hmask = [lane < dh, lane >= dh]

    @pl.when(i == 0)
    def _():
        r_i = lax.broadcasted_iota(jnp.int32, (LANES, LANES), 0)
        c_i = lax.broadcasted_iota(jnp.int32, (LANES, LANES), 1)
        tri_u = (r_i <= c_i).astype(BF16)
        carry_r = jnp.zeros((nh, 1), F32)
        per = tk // LANES
        for blk in range(seq // LANES):
            rows = slice(blk * LANES, (blk + 1) * LANES)
            cr = _dot_exact_r(_log_sigmoid(gr_ref[:, rows] + brow_ref[...]), tri_u) + carry_r
            cr_scr[blk // per, :, (blk % per) * LANES:(blk % per + 1) * LANES] = cr * LOG2E
            carry_r = cr[:, LANES - 1:LANES]

    m_scr[...] = jnp.full(m_scr.shape, NEG_BIG, F32)
    l_scr[...] = jnp.zeros_like(l_scr)
    acc_scr[...] = jnp.zeros_like(acc_scr)

    def step(j, qrow0, nrows, width, masked):
        k0 = pl.multiple_of(j * tk, tk)
        crow = cr_scr[j][:, :width]
        qrows = slice(qrow0, qrow0 + nrows)
        if masked:
            causal = (lax.broadcasted_iota(jnp.int32, (nrows, width), 1)
                      <= lax.broadcasted_iota(jnp.int32, (nrows, width), 0) + qrow0)
        for p in range(nh // 2):
            cols = slice(p * LANES, (p + 1) * LANES)
            kp = k_ref[pl.ds(k0, width), cols]
            vp = v_ref[pl.ds(k0, width), cols]
            qp = q_ref[qrows, cols]
            q2 = jnp.concatenate([jnp.where(hmask[e], qp, jnp.zeros_like(qp)) for e in range(2)], axis=0)
            s2 = _dot_nt(q2, kp)
            dm, ls, pms = [], [], []
            for e in range(2):
                h = 2 * p + e
                s = s2[e * nrows:(e + 1) * nrows] - crow[h:h + 1, :]
                if masked:
                    s = jnp.where(causal, s, NEG_BIG)
                m_prev = m_scr[h, qrows, :]
                m_next = jnp.maximum(m_prev, jnp.max(s, axis=-1, keepdims=True))
                m_scr[h, qrows, :] = m_next
                pm = jnp.exp2(s - jnp.concatenate([m_next] * (width // LANES), axis=1))
                dm.append(m_prev - m_next)
                ls.append(jnp.sum(pm, axis=-1, keepdims=True))
                pms.append(pm.astype(BF16))
            pv = _dot(jnp.concatenate(pms, axis=0), vp)
            alpha = jnp.exp2(jnp.where(hmask[0], dm[0], dm[1]))
            l_scr[p, qrows, :] = alpha * l_scr[p, qrows, :] + jnp.where(hmask[0], ls[0], ls[1])
            acc_scr[p, qrows, :] = alpha * acc_scr[p, qrows, :] + jnp.where(hmask[0], pv[:nrows], pv[nrows:])

    def body(j, carry):
        step(j, 0, tq, tk, False)
        return carry

    lax.fori_loop(0, i, body, 0)
    for r in range(tq // F_DIAG_ROWS):
        step(i, r * F_DIAG_ROWS, F_DIAG_ROWS, (r + 1) * F_DIAG_ROWS, True)
    for p in range(nh // 2):
        cols = slice(p * LANES, (p + 1) * LANES)
        obuf[:, cols] = acc_scr[p] / l_scr[p] * _sigmoid(o_ref[:, cols])

    out_ref[...] = h_ref[...] + _dot(obuf[...].astype(BF16), wout_ref[...])


def _fox_layer(h2d, bsz, seq, norm_g, w_in, b_f, w_out, *, tm=1024, tq=256):
    t, d = h2d.shape
    nh, dh = F_HEADS, F_HEAD_DIM
    hd = nh * dh
    wq, wk, wv = (w_in[:, j * hd:(j + 1) * hd].astype(BF16) for j in range(3))
    wf = w_in[:, 3 * hd:3 * hd + nh]
    wo = w_in[:, 3 * hd + nh:].astype(BF16)
    wgt = jnp.transpose(wf).astype(BF16)
    tile = lambda w: pl.BlockSpec((tm, w), lambda i: (i, 0))
    q, k, v, o, gr = pl.pallas_call(
        _fox_proj_kernel,
        grid=(t // tm,),
        in_specs=[tile(d), _const_spec((1, d)), _const_spec(wq.shape), _const_spec(wk.shape),
                  _const_spec(wv.shape), _const_spec(wo.shape), _const_spec(wgt.shape)],
        out_specs=[tile(hd), tile(hd), tile(hd), tile(hd), pl.BlockSpec((nh, tm), lambda i: (0, i))],
        out_shape=[jax.ShapeDtypeStruct((t, hd), BF16)] * 3 + [jax.ShapeDtypeStruct((t, hd), F32),
                   jax.ShapeDtypeStruct((nh, t), F32)],
        compiler_params=_cparams("parallel"),
        name="fox_proj",
    )(h2d, norm_g.reshape(1, d), wq, wk, wv, wo, wgt)

    nt = seq // tq
    q_spec = lambda w: pl.BlockSpec((tq, w), lambda b, i: (b * nt + i, 0))
    seq_spec = lambda w: pl.BlockSpec((seq, w), lambda b, i: (b, 0))
    return pl.pallas_call(
        functools.partial(_fox_kernel, tq=tq, tk=tq, seq=seq),
        grid=(bsz, nt),
        in_specs=[q_spec(hd), seq_spec(hd), seq_spec(hd), q_spec(hd),
                  pl.BlockSpec((nh, seq), lambda b, i: (0, b)), q_spec(d),
                  _const_spec((nh, 1)), _const_spec((hd, d))],
        out_specs=q_spec(d),
        out_shape=jax.ShapeDtypeStruct((t, d), F32),
        scratch_shapes=[pltpu.VMEM((nt, nh, tq), F32), pltpu.VMEM((tq, hd), F32),
                        pltpu.VMEM((nh, tq, LANES), F32), pltpu.VMEM((nh // 2, tq, LANES), F32),
                        pltpu.VMEM((nh // 2, tq, LANES), F32)],
        compiler_params=_cparams("parallel", "arbitrary"),
        name="fox_attn",
    )(q, k, v, o, gr, h2d, b_f.reshape(nh, 1), w_out.astype(BF16))


R_HEADS = 16
R_HEAD_DIM = 64
R_LN_EPS = 64e-5
R_SUB = 16
R_HALF = R_SUB // 2


def _head_sum(x, lane):
    lo = lane < R_HEAD_DIM
    s0 = jnp.sum(jnp.where(lo, x, 0.0), axis=-1, keepdims=True)
    s1 = jnp.sum(jnp.where(lo, 0.0, x), axis=-1, keepdims=True)
    return jnp.where(lo, s0, s1)


def _rw_proj_kernel(h_ref, g_ref, mu_ref, wr_ref, wk_ref, wv_ref, wd_ref, wa_ref, wg_ref, wup_ref, aup_ref,
                    gup_ref, w0_ref, a0_ref, kk_ref, ka_ref,
                    r_ref, lw_ref, k_ref, v_ref, al_ref, be_ref, g_out_ref, ext_scr, *, tm, seq):
    d = h_ref.shape[1]
    i = pl.program_id(0)
    xn = _rms(h_ref[...], g_ref[...])
    @pl.when((i * tm) % seq == 0)
    def _():
        ext_scr[7:8, :] = jnp.zeros((1, d), F32)

    @pl.when((i * tm) % seq != 0)
    def _():
        ext_scr[7:8, :] = ext_scr[7 + tm:8 + tm, :]

    ext_scr[8:8 + tm, :] = xn
    xb = xn.astype(BF16)
    dxb = (ext_scr[7:7 + tm, :] - xn).astype(BF16)
    mub = mu_ref[...].astype(BF16)
    mix = lambda j: xb + dxb * mub[j:j + 1, :]
    r_ref[...] = _dot(mix(0), wr_ref[...]).astype(r_ref.dtype)
    w_in = w0_ref[...] + _dot(jnp.tanh(_dot(mix(1), wd_ref[...])).astype(BF16), wup_ref[...])
    lw_ref[...] = -math.exp(-0.5) * _sigmoid(w_in)
    k = _dot(mix(2), wk_ref[...])
    v_ref[...] = _dot(mix(3), wv_ref[...]).astype(v_ref.dtype)
    a = _sigmoid(a0_ref[...] + _dot(_dot(mix(4), wa_ref[...]).astype(BF16), aup_ref[...]))
    g_out_ref[...] = _dot(_sigmoid(_dot(mix(5), wg_ref[...])).astype(BF16), gup_ref[...]).astype(g_out_ref.dtype)
    k_ref[...] = (k * (1.0 + (a - 1.0) * ka_ref[...])).astype(k_ref.dtype)
    lane = lax.broadcasted_iota(jnp.int32, (1, LANES), 1)
    for p in range(d // LANES):
        cols = slice(p * LANES, (p + 1) * LANES)
        kk = k[:, cols] * kk_ref[:, cols]
        kk = kk * jnp.minimum(lax.rsqrt(_head_sum(kk * kk, lane)), 1e12)
        al_ref[:, cols] = kk.astype(al_ref.dtype)
        be_ref[:, cols] = (kk * a[:, cols]).astype(be_ref.dtype)


def _head_sum_mxu(x, ones_bd):
    hi = x.astype(BF16)
    lo = (x - hi.astype(F32)).astype(BF16)
    return _dot(jnp.concatenate([hi, lo], axis=1), ones_bd)


def _rw_kernel(r_ref, lw_ref, k_ref, v_ref, al_ref, be_ref, g_ref, h_ref, rk_ref, lng_ref, lnb_ref, wout_ref,
               sel_ref, out_ref, ht_scr, ybuf, aab_scr, aoff_scr, arkb_scr, x_scr, coef_scr, *, lt, lc):
    d = h_ref.shape[1]
    dh = R_HEAD_DIM
    npair = d // LANES
    nb = lc // R_SUB

    @pl.when(pl.program_id(1) == 0)
    def _():
        ht_scr[...] = jnp.zeros_like(ht_scr)

    lane = lax.broadcasted_iota(jnp.int32, (1, LANES), 1)
    hmask = [lane < dh, lane >= dh]
    lane2 = lax.broadcasted_iota(jnp.int32, (1, 2 * LANES), 1) % LANES
    hmask2 = [lane2 < dh, lane2 >= dh]
    r_i = lax.broadcasted_iota(jnp.int32, (lc, lc), 0)
    c_i = lax.broadcasted_iota(jnp.int32, (lc, lc), 1)
    tri_l = (c_i <= r_i).astype(BF16)
    r_p2 = lax.broadcasted_iota(jnp.int32, (lc, LANES), 0)
    c_p2 = lax.broadcasted_iota(jnp.int32, (lc, LANES), 1) % lc
    strict2 = c_p2 < r_p2
    incl2 = c_p2 <= r_p2
    below2 = (c_p2 // R_SUB) < (r_p2 // R_SUB)
    rr = lax.broadcasted_iota(jnp.int32, (LANES, LANES), 0)
    cc = lax.broadcasted_iota(jnp.int32, (LANES, LANES), 1)
    bdiag = (rr < dh) == (cc < dh)

    def chunk(c, carry):
        rows = slice(c * lc, (c + 1) * lc)
        lw = lw_ref[rows, :]
        cum = _dot_exact_l(tri_l, lw)
        cum_last = cum[lc - 1:lc, :]
        e_pos = jnp.exp(cum)
        e_neg = jnp.exp(-cum)
        e_rem = jnp.exp(cum_last - cum)
        r_t = r_ref[rows, :] * e_pos
        a_t = al_ref[rows, :] * jnp.exp(cum - lw)
        k_all = k_ref[rows, :]
        b_all = be_ref[rows, :]
        k_t, b_t = k_all * e_neg, b_all * e_neg
        k_h, b_h = k_all * e_rem, b_all * e_rem
        gam = jnp.exp(cum_last)

        nheads = 2 * npair
        vps = [v_ref[rows, p * LANES:(p + 1) * LANES].astype(BF16) for p in range(npair)]
        vp2s = [jnp.concatenate([vp, vp], axis=0) for vp in vps]

        scs = []
        for p in range(npair):
            cols = slice(p * LANES, (p + 1) * LANES)
            lhs = jnp.concatenate([a_t[:, cols], r_t[:, cols]], axis=0).astype(BF16)
            b_p, k_p = b_t[:, cols], k_t[:, cols]
            rhs = jnp.concatenate([jnp.where(hmask[0], b_p, 0.0), jnp.where(hmask[1], b_p, 0.0),
                                   jnp.where(hmask[0], k_p, 0.0), jnp.where(hmask[1], k_p, 0.0)], axis=0)
            scs.append(_dot_nt(lhs, rhs.astype(BF16)))
        aaks = []
        for p in range(npair):
            sc = scs[p]
            top_b, top_k = sc[:lc, :LANES], sc[:lc, LANES:]
            bot_b, bot_k = sc[lc:, :LANES], sc[lc:, LANES:]
            aab_scr[p] = jnp.where(strict2, top_b, 0.0)
            aoff_scr[p] = jnp.where(below2, top_b, 0.0).astype(BF16)
            arkb_scr[p] = jnp.concatenate([jnp.where(incl2, bot_k, 0.0), jnp.where(incl2, bot_b, 0.0)],
                                          axis=1).astype(BF16)
            a_ak = jnp.where(strict2, top_k, 0.0)
            aaks.append(jnp.concatenate([jnp.where(hmask[0], a_ak, 0.0), jnp.where(hmask[1], a_ak, 0.0)],
                                        axis=0).astype(BF16))
        for p in range(npair):
            cols = slice(p * LANES, (p + 1) * LANES)
            av = _dot(aaks[p], vp2s[p])
            for e in range(2):
                av_e = jnp.where(hmask[e], av[e * lc:(e + 1) * lc], 0.0)
                x_scr[2 * p + e] = jnp.where(hmask[e], a_t[:, cols], 0.0) + pltpu.roll(av_e, dh, axis=1)

        for bi in range(nb):
            brows = slice(bi * R_SUB, (bi + 1) * R_SUB)
            a_blk = aab_scr[:, brows, :].reshape(npair * R_SUB, LANES)
            a_hi = a_blk.astype(BF16)
            a_lo = (a_blk - a_hi.astype(F32)).astype(BF16)
            coef_scr[...] = _dot(jnp.concatenate([a_hi, a_lo], axis=1), sel_ref[bi])
            xs = [x_scr[h, brows, :] for h in range(nheads)]
            if bi > 0:
                for p in range(npair):
                    slab = aoff_scr[p, brows, :]
                    lhs = jnp.concatenate([jnp.where(hmask[0], slab, jnp.zeros_like(slab)),
                                           jnp.where(hmask[1], slab, jnp.zeros_like(slab))], axis=0)
                    xst = x_scr[2 * p:2 * p + 2].reshape(2 * lc, LANES).astype(BF16)
                    corr = _dot(lhs, xst)
                    for e in range(2):
                        xs[2 * p + e] = xs[2 * p + e] - corr[e * R_SUB:(e + 1) * R_SUB]
            x_lo = [x[:R_HALF] for x in xs]
            x_hi = [x[R_HALF:] for x in xs]
            for s in range(R_HALF - 1):
                for h in range(nheads):
                    p, e = h // 2, h % 2
                    tile = e * (R_HALF - 1) + s
                    coef = coef_scr[p * R_SUB:(p + 1) * R_SUB, tile * LANES:(tile + 1) * LANES]
                    row = x_lo[h][s:s + 1, :]
                    x_lo[h] = x_lo[h] - coef[:R_HALF] * row
                    x_hi[h] = x_hi[h] - coef[R_HALF:] * row
            for s in range(R_HALF - 1, R_SUB - 1):
                for h in range(nheads):
                    p, e = h // 2, h % 2
                    col = e * lc + bi * R_SUB + s
                    coef = aab_scr[p, bi * R_SUB + R_HALF:(bi + 1) * R_SUB, col:col + 1]
                    row = x_lo[h][s:s + 1, :] if s < R_HALF else x_hi[h][s - R_HALF:s - R_HALF + 1, :]
                    x_hi[h] = x_hi[h] - coef * row
            for h in range(nheads):
                x_scr[h, bi * R_SUB:bi * R_SUB + R_HALF, :] = x_lo[h]
                x_scr[h, bi * R_SUB + R_HALF:(bi + 1) * R_SUB, :] = x_hi[h]

        hts, ubs, yhs = [], [], []
        for p in range(npair):
            cols = slice(p * LANES, (p + 1) * LANES)
            x0, x1 = x_scr[2 * p], x_scr[2 * p + 1]
            w_pair = jnp.where(hmask[0], x0, x1)
            uv_pair = pltpu.roll(jnp.where(hmask[0], x1, x0), dh, axis=1)
            ht = ht_scr[p]
            hts.append(ht)
            wr = jnp.concatenate([w_pair, r_t[:, cols]], axis=0).astype(BF16)
            wh = _dot_nt(wr, ht.astype(BF16))
            ubs.append((wh[:lc] + uv_pair).astype(BF16))
            yhs.append(wh[lc:])
        for p in range(npair):
            cols = slice(p * LANES, (p + 1) * LANES)
            arkb = arkb_scr[p]
            lhs = jnp.concatenate([jnp.where(hmask2[e], arkb, jnp.zeros_like(arkb)) for e in range(2)], axis=0)
            rhs = jnp.concatenate([vp2s[p], -ubs[p], -ubs[p]], axis=0)
            intra = _dot(lhs, rhs)
            ybuf[rows, cols] = yhs[p] + jnp.where(hmask[0], intra[:lc], intra[lc:])
        for p in range(npair):
            cols = slice(p * LANES, (p + 1) * LANES)
            vu = jnp.concatenate([vps[p], -ubs[p]], axis=0)
            kb = jnp.concatenate([k_h[:, cols], b_h[:, cols]], axis=0).astype(BF16)
            upd = lax.dot_general(vu, kb, (((0,), (0,)), ((), ())), preferred_element_type=F32)
            ht_scr[p] = hts[p] * gam[:, cols] + jnp.where(bdiag, upd, 0.0)
        return carry

    for c in range(lt // lc):
        chunk(c, 0)

    rr2 = lax.broadcasted_iota(jnp.int32, (2 * LANES, LANES), 0)
    cc2 = lax.broadcasted_iota(jnp.int32, (2 * LANES, LANES), 1)
    ones_bd = (((rr2 % LANES) < dh) == (cc2 < dh)).astype(BF16)
    for p in range(npair):
        cols = slice(p * LANES, (p + 1) * LANES)
        y = ybuf[:, cols]
        r, k, v = r_ref[:, cols], k_ref[:, cols], v_ref[:, cols]
        bonus = _head_sum(r * k * rk_ref[:, cols], lane) * v
        yc = y - _head_sum_mxu(y, ones_bd) * (1.0 / dh)
        yn = yc * lax.rsqrt(_head_sum_mxu(yc * yc, ones_bd) * (1.0 / dh) + R_LN_EPS)
        ybuf[:, cols] = (yn * lng_ref[:, cols] + lnb_ref[:, cols] + bonus) * g_ref[:, cols]
    out_ref[...] = h_ref[...] + _dot(ybuf[...].astype(BF16), wout_ref[...])


def _rwkv_layer(h2d, bsz, seq, norm_g, mu, w_in, w0, w_up, a0, a_up, g_up, k_k, k_a, r_k, ln_g, ln_b, w_out,
                *, tm=512, lt=512, lc=64, act_dtype=F32):
    t, d = h2d.shape
    c0 = 3 * d
    c1 = c0 + w_up.shape[0]
    c2 = c1 + a_up.shape[0]
    pad_c = lambda w, n: jnp.pad(w, ((0, 0), (0, n - w.shape[1]))).astype(BF16)
    pad_r = lambda w, n: jnp.pad(w, ((0, n - w.shape[0]), (0, 0))).astype(BF16)
    up = lambda n: -(-n // LANES) * LANES
    nw, na, ng = up(c1 - c0), up(c2 - c1), up(w_in.shape[1] - c2)
    wr, wk, wv = (w_in[:, j * d:(j + 1) * d].astype(BF16) for j in range(3))
    wd, wa, wg = pad_c(w_in[:, c0:c1], nw), pad_c(w_in[:, c1:c2], na), pad_c(w_in[:, c2:], ng)
    row = lambda a: a.reshape(1, d)
    tile = pl.BlockSpec((tm, d), lambda i: (i, 0))
    outs = pl.pallas_call(
        functools.partial(_rw_proj_kernel, tm=tm, seq=seq),
        grid=(t // tm,),
        in_specs=[tile, _const_spec((1, d)), _const_spec((6, d)), _const_spec((d, d)), _const_spec((d, d)),
                  _const_spec((d, d)), _const_spec((d, nw)), _const_spec((d, na)), _const_spec((d, ng)),
                  _const_spec((nw, d)), _const_spec((na, d)), _const_spec((ng, d)),
                  _const_spec((1, d)), _const_spec((1, d)), _const_spec((1, d)), _const_spec((1, d))],
        out_specs=[tile] * 7,
        out_shape=[jax.ShapeDtypeStruct((t, d), F32 if name == "lw" else act_dtype)
                   for name in ("r", "lw", "k", "v", "al", "be", "g")],
        scratch_shapes=[pltpu.VMEM((tm + 8, d), F32)],
        compiler_params=_cparams("arbitrary"),
        name="rw_proj",
    )(h2d, row(norm_g), mu, wr, wk, wv, wd, wa, wg, pad_r(w_up, nw), pad_r(a_up, na), pad_r(g_up, ng),
      row(w0), row(a0), row(k_k), row(k_a))
    r, lw, k, v, al, be, g = outs

    nt = seq // lt
    assert lc == R_HEAD_DIM, "pair tiles put the two heads' (lc, lc) matrices side by side on 128 lanes"
    nb, nmx = lc // R_SUB, R_HALF - 1
    nsel = 2 * nmx * LANES
    klane = jnp.arange(2 * LANES)[None, :, None] % LANES
    otile = jnp.arange(nsel)[None, None, :] // LANES
    sel = ((klane // lc == otile // nmx)
           & (klane % lc == jnp.arange(nb)[:, None, None] * R_SUB + otile % nmx)).astype(BF16)
    spec = pl.BlockSpec((lt, d), lambda b, i: (b * nt + i, 0))
    return pl.pallas_call(
        functools.partial(_rw_kernel, lt=lt, lc=lc),
        grid=(bsz, nt),
        in_specs=[spec] * 8 + [_const_spec((1, d))] * 3 + [_const_spec((d, d)), _const_spec(sel.shape)],
        out_specs=spec,
        out_shape=jax.ShapeDtypeStruct((t, d), F32),
        scratch_shapes=[pltpu.VMEM((d // LANES, LANES, LANES), F32), pltpu.VMEM((lt, d), F32),
                        pltpu.VMEM((d // LANES, lc, LANES), F32), pltpu.VMEM((d // LANES, lc, LANES), BF16),
                        pltpu.VMEM((d // LANES, lc, 2 * LANES), BF16),
                        pltpu.VMEM((R_HEADS, lc, LANES), F32), pltpu.VMEM((d // LANES * R_SUB, nsel), F32)],
        compiler_params=_cparams("parallel", "arbitrary"),
        name="rw_mix",
    )(r, lw, k, v, al, be, g, h2d, r_k.reshape(1, d), row(ln_g), row(ln_b), w_out.astype(BF16), sel)


def kernel(x, mlp_norm_g, mlp_w_up, mlp_w_down, final_norm_g, s5_norm_g, s5_a_re, s5_a_im, s5_log_dt, s5_b_re, s5_b_im, s5_c_re, s5_c_im, s5_d, s5_w_glu, s5_b_glu, ml_norm_g, ml_w_in, ml_conv_w, ml_conv_b, ml_b_i, ml_b_f, ml_head_g, ml_w_out, fox_norm_g, fox_w_in, fox_b_f, fox_w_out, rw_norm_g, rw_mu, rw_w_in, rw_w0, rw_w_up, rw_a0, rw_a_up, rw_g_up, rw_k_k, rw_k_a, rw_r_k, rw_ln_g, rw_ln_b, rw_w_out):
    bsz, seq, d = x.shape
    depth = mlp_w_up.shape[0]
    h = x.reshape(bsz * seq, d)
    for i in range(depth):
        kind, j = i % 4, i // 4
        if kind == 0:
            h = _s5_layer(h, bsz, seq, s5_norm_g[j], s5_a_re[j], s5_a_im[j], s5_log_dt[j], s5_b_re[j], s5_b_im[j],
                          s5_c_re[j], s5_c_im[j], s5_d[j], s5_w_glu[j], s5_b_glu[j])
        elif kind == 1:
            h = _mlstm_layer(h, bsz, seq, ml_norm_g[j], ml_w_in[j], ml_conv_w[j], ml_conv_b[j], ml_b_i[j],
                             ml_b_f[j], ml_head_g[j], ml_w_out[j])
        elif kind == 2:
            h = _fox_layer(h, bsz, seq, fox_norm_g[j], fox_w_in[j], fox_b_f[j], fox_w_out[j])
        else:
            h = _rwkv_layer(h, bsz, seq, rw_norm_g[j], rw_mu[j], rw_w_in[j], rw_w0[j], rw_w_up[j], rw_a0[j],
                            rw_a_up[j], rw_g_up[j], rw_k_k[j], rw_k_a[j], rw_r_k[j], rw_ln_g[j], rw_ln_b[j],
                            rw_w_out[j])
        h = _mlp_layer(h, mlp_norm_g[i], mlp_w_up[i], mlp_w_down[i],
                       final_g=final_norm_g if i == depth - 1 else None)
    return h.reshape(bsz, seq, d)
```
